```python
import jax, jax.numpy as jnp
from jax import lax
import numpy as np

D_MODEL = 1024
BATCH = 16
SEQ = 4096
DEPTH = 1

GLA_HEADS = 4
GLA_QK_WIDTH = D_MODEL // 2
GLA_V_WIDTH = D_MODEL
GLA_DK = GLA_QK_WIDTH // GLA_HEADS
GLA_DV = GLA_V_WIDTH // GLA_HEADS
GLA_GATE_RANK = 16
GLA_TAU = 16.0
GLA_CHUNK = 64
GLA_NORM_EPS = 1e-5
RWKV_HEAD_DIM = 64
RWKV_WIDTH = D_MODEL
RWKV_HEADS = RWKV_WIDTH // RWKV_HEAD_DIM
RWKV_DECAY_LORA = 64
RWKV_ICLR_LORA = 64
RWKV_GATE_LORA = 128
RWKV_GN_EPS = 64e-5
N_EXPERTS = 32
TOP_K = 4
EXPERT_FF = D_MODEL
SWIGLU_ALPHA = 1.702
SWIGLU_LIMIT = 7.0
MOE_BLOCK = 256
RMS_EPS = 1e-6

GLA_COLS = 2 * GLA_QK_WIDTH + GLA_V_WIDTH + GLA_GATE_RANK + GLA_V_WIDTH
RWKV_COLS = 3 * RWKV_WIDTH + RWKV_DECAY_LORA + RWKV_ICLR_LORA + RWKV_GATE_LORA
GATE_COLS = 2 * D_MODEL
IN_WIDTH = GLA_COLS + RWKV_COLS + GATE_COLS

kernel_name = 'hybrid_gla_rwkv7_moe_block'


def _split(z, sizes):
    out, start = [], 0
    for s in sizes:
        out.append(z[..., start:start + s])
        start += s
    return out


def rms_norm(x, gain, eps=RMS_EPS):
    xf = x.astype(jnp.float32)
    y = xf * lax.rsqrt(jnp.mean(xf * xf, axis=-1, keepdims=True) + eps)
    return y.astype(x.dtype) * gain


def gla_chunked(q, k, v, log_a):
    f32 = jnp.float32
    Bn, S, H, DK = q.shape
    DV = v.shape[-1]
    C = GLA_CHUNK
    N = S // C
    q = q.astype(f32).reshape(Bn, N, C, H, DK) * (DK ** -0.5)
    k = k.astype(f32).reshape(Bn, N, C, H, DK)
    v = v.astype(f32).reshape(Bn, N, C, H, DV)
    b = jnp.cumsum(log_a.astype(f32).reshape(Bn, N, C, H, DK), axis=2)
    b_ref = b[:, :, C // 2:C // 2 + 1]
    b_end = b[:, :, C - 1:]
    causal = jnp.tril(jnp.ones((C, C), dtype=bool))
    scores = jnp.einsum('bnihd,bnjhd->bnhij', q * jnp.exp(b - b_ref), k * jnp.exp(b_ref - b))
    scores = jnp.where(causal, scores, 0.0)
    o_intra = jnp.einsum('bnhij,bnjhv->bnihv', scores, v)
    q_in = q * jnp.exp(b)
    k_st = k * jnp.exp(b_end - b)
    decay = jnp.exp(b_end[:, :, 0])

    def step(state, inp):
        q_n, k_n, v_n, d_n = inp
        o_n = jnp.einsum('bihd,bhdv->bihv', q_n, state)
        state = state * d_n[..., None] + jnp.einsum('bjhd,bjhv->bhdv', k_n, v_n)
        return state, o_n

    state0 = jnp.zeros((Bn, H, DK, DV), f32)
    xs = (jnp.moveaxis(q_in, 1, 0), jnp.moveaxis(k_st, 1, 0), jnp.moveaxis(v, 1, 0), jnp.moveaxis(decay, 1, 0))
    _, o_inter = lax.scan(step, state0, xs)
    o = o_intra + jnp.moveaxis(o_inter, 0, 1)
    return o.reshape(Bn, S, H, DV)


def gla_branch(z_gla, gate_up, gate_bias, norm_g):
    Bn, S, _ = z_gla.shape
    q, k, v, gate_lr, o_gate = _split(z_gla, (GLA_QK_WIDTH, GLA_QK_WIDTH, GLA_V_WIDTH, GLA_GATE_RANK, GLA_V_WIDTH))
    log_a = jax.nn.log_sigmoid((gate_lr @ gate_up + gate_bias).astype(jnp.float32)) / GLA_TAU
    o = gla_chunked(q.reshape(Bn, S, GLA_HEADS, GLA_DK), k.reshape(Bn, S, GLA_HEADS, GLA_DK),
                    v.reshape(Bn, S, GLA_HEADS, GLA_DV), log_a.reshape(Bn, S, GLA_HEADS, GLA_DK))
    o = o * lax.rsqrt(jnp.mean(o * o, axis=-1, keepdims=True) + GLA_NORM_EPS) * norm_g
    o = o.reshape(Bn, S, GLA_V_WIDTH) * jax.nn.silu(o_gate.astype(jnp.float32))
    return o.astype(z_gla.dtype)


def rwkv7_scan(r, w, k, v, a_vec, b_vec):
    f32 = jnp.float32
    Bn, S, H, N = r.shape

    def step(state, inp):
        r_t, w_t, k_t, v_t, a_t, b_t = inp
        sa = jnp.einsum('bhij,bhj->bhi', state, a_t)
        state = state * w_t[:, :, None, :] + sa[..., None] * b_t[:, :, None, :] + v_t[..., None] * k_t[:, :, None, :]
        y_t = jnp.einsum('bhij,bhj->bhi', state, r_t)
        return state, y_t

    state0 = jnp.zeros((Bn, H, N, N), f32)
    xs = tuple(jnp.moveaxis(t.astype(f32), 1, 0) for t in (r, w, k, v, a_vec, b_vec))
    _, ys = lax.scan(step, state0, xs)
    return jnp.moveaxis(ys, 0, 1)


def rwkv7_branch(z_rwkv, mu, w0, w2, a0, a2, g2, k_k, k_a, r_k, ln_w, ln_b):
    f32 = jnp.float32
    Bn, S, _ = z_rwkv.shape
    prev = jnp.pad(z_rwkv, ((0, 0), (1, 0), (0, 0)))[:, :-1]
    u = z_rwkv + mu * (prev - z_rwkv)
    r, k, v, xw, xa, xg = _split(u, (RWKV_WIDTH, RWKV_WIDTH, RWKV_WIDTH, RWKV_DECAY_LORA, RWKV_ICLR_LORA, RWKV_GATE_LORA))
    w_log = -jax.nn.softplus(-(w0 + jnp.tanh(xw) @ w2).astype(f32)) - 0.5
    decay = jnp.exp(-jnp.exp(w_log))
    a = jax.nn.sigmoid((a0 + xa @ a2).astype(f32))
    g = jax.nn.sigmoid(xg) @ g2
    hs = (Bn, S, RWKV_HEADS, RWKV_HEAD_DIM)
    kf = k.astype(f32)
    kk = (kf * k_k).reshape(hs)
    kk = kk / jnp.maximum(jnp.sqrt(jnp.sum(kk * kk, axis=-1, keepdims=True)), 1e-12)
    kmod = (kf * (1.0 + (a - 1.0) * k_a)).reshape(hs)
    rf = r.astype(f32).reshape(hs)
    vf = v.astype(f32).reshape(hs)
    y = rwkv7_scan(rf, decay.reshape(hs), kmod, vf, -kk, kk * a.reshape(hs))
    mean = jnp.mean(y, axis=-1, keepdims=True)
    var = jnp.mean(jnp.square(y - mean), axis=-1, keepdims=True)
    y = ((y - mean) * lax.rsqrt(var + RWKV_GN_EPS)).reshape(Bn, S, RWKV_WIDTH) * ln_w + ln_b
    y = y + (jnp.sum(rf * kmod * r_k, axis=-1, keepdims=True) * vf).reshape(Bn, S, RWKV_WIDTH)
    return (y * g).astype(z_rwkv.dtype)


def hybrid_mixer(h, w_in, gla_gate_up, gla_gate_bias, gla_norm_g, rwkv_mu, rwkv_w0, rwkv_w2, rwkv_a0, rwkv_a2,
                 rwkv_g2, rwkv_k_k, rwkv_k_a, rwkv_r_k, rwkv_ln_w, rwkv_ln_b, w_branch_a, w_branch_b, w_out):
    z = h @ w_in
    z_gla, z_rwkv, z_gate = _split(z, (GLA_COLS, RWKV_COLS, GATE_COLS))
    y_a = gla_branch(z_gla, gla_gate_up, gla_gate_bias, gla_norm_g)
    y_b = rwkv7_branch(z_rwkv, rwkv_mu, rwkv_w0, rwkv_w2, rwkv_a0, rwkv_a2, rwkv_g2, rwkv_k_k, rwkv_k_a,
                       rwkv_r_k, rwkv_ln_w, rwkv_ln_b)
    gates = jax.nn.sigmoid(z_gate.astype(jnp.float32)).astype(h.dtype)
    g_a, g_b = _split(gates, (D_MODEL, D_MODEL))
    merged = g_a * (y_a @ w_branch_a) + g_b * (y_b @ w_branch_b)
    return merged @ w_out


def clamped_swiglu(hid):
    hid = hid.reshape(hid.shape[:-1] + (EXPERT_FF, 2))
    glu = jnp.minimum(hid[..., 0], SWIGLU_LIMIT)
    lin = jnp.clip(hid[..., 1], -SWIGLU_LIMIT, SWIGLU_LIMIT)
    return glu * jax.nn.sigmoid(SWIGLU_ALPHA * glu) * (lin + 1.0)


def moe_ffn(h, router_w, router_b, w1, b1, w2, b2):
    Bn, S, D = h.shape
    T = Bn * S
    t = h.reshape(T, D)
    logits = (t @ router_w + router_b).astype(jnp.float32)
    top_vals, top_idx = lax.top_k(logits, TOP_K)
    gates = jax.nn.softmax(top_vals, axis=-1)
    TK = T * TOP_K
    e_flat = top_idx.reshape(TK)
    tok_flat = jnp.arange(TK, dtype=jnp.int32) // TOP_K
    gate_flat = gates.reshape(TK).astype(t.dtype)
    order = jnp.argsort(e_flat, stable=True)
    e_sorted = e_flat[order]
    tok_sorted = tok_flat[order]
    gate_sorted = gate_flat[order]
    counts = jnp.bincount(e_flat, length=N_EXPERTS)
    starts = jnp.cumsum(counts) - counts
    padded = (counts + MOE_BLOCK - 1) // MOE_BLOCK * MOE_BLOCK
    pad_ends = jnp.cumsum(padded)
    pad_starts = pad_ends - padded
    dest = pad_starts[e_sorted] + (jnp.arange(TK) - starts[e_sorted])
    NB = -(-TK // MOE_BLOCK) + N_EXPERTS
    rows_tok = jnp.zeros((NB * MOE_BLOCK,), jnp.int32).at[dest].set(tok_sorted)
    rows_gate = jnp.zeros((NB * MOE_BLOCK,), t.dtype).at[dest].set(gate_sorted)
    block_expert = jnp.minimum(jnp.searchsorted(pad_ends, jnp.arange(NB) * MOE_BLOCK, side='right'), N_EXPERTS - 1)

    def block_fn(args):
        tok_b, gate_b, e = args
        hid = t[tok_b] @ w1[e] + b1[e]
        out = clamped_swiglu(hid) @ w2[e] + b2[e]
        return out * gate_b[:, None]

    outs = lax.map(block_fn, (rows_tok.reshape(NB, MOE_BLOCK), rows_gate.reshape(NB, MOE_BLOCK), block_expert))
    y = jax.ops.segment_sum(outs.reshape(NB * MOE_BLOCK, D), rows_tok, num_segments=T)
    return y.reshape(Bn, S, D).astype(h.dtype)


def setup_inputs(seed: int = 0) -> dict:
    key = jax.random.key(seed)
    ks = jax.random.split(key, 32)
    f32 = jnp.float32
    L = DEPTH

    def nrm(k, shape, scale):
        return jax.random.normal(k, shape, f32) * scale

    ramp = (jnp.arange(RWKV_WIDTH, dtype=f32) / (RWKV_WIDTH - 1)) ** 0.85
    return {
        'x': nrm(ks[0], (BATCH, SEQ, D_MODEL), 1.0),
        'norm_mix_g': 1.0 + nrm(ks[1], (L, D_MODEL), 0.02),
        'w_in': nrm(ks[2], (L, D_MODEL, IN_WIDTH), D_MODEL ** -0.5),
        'gla_gate_up': nrm(ks[3], (L, GLA_GATE_RANK, GLA_QK_WIDTH), GLA_GATE_RANK ** -0.5),
        'gla_gate_bias': nrm(ks[4], (L, GLA_QK_WIDTH), 0.5),
        'gla_norm_g': 1.0 + nrm(ks[5], (L, GLA_DV), 0.02),
        'rwkv_mu': jax.random.uniform(ks[6], (L, RWKV_COLS), f32),
        'rwkv_w0': -6.5 + 5.0 * ramp + nrm(ks[7], (L, RWKV_WIDTH), 0.1),
        'rwkv_w2': nrm(ks[8], (L, RWKV_DECAY_LORA, RWKV_WIDTH), 0.1 * RWKV_DECAY_LORA ** -0.5),
        'rwkv_a0': nrm(ks[9], (L, RWKV_WIDTH), 0.1),
        'rwkv_a2': nrm(ks[10], (L, RWKV_ICLR_LORA, RWKV_WIDTH), RWKV_ICLR_LORA ** -0.5),
        'rwkv_g2': nrm(ks[11], (L, RWKV_GATE_LORA, RWKV_WIDTH), RWKV_GATE_LORA ** -0.5),
        'rwkv_k_k': 0.85 + nrm(ks[12], (L, RWKV_WIDTH), 0.02),
        'rwkv_k_a': 1.0 + nrm(ks[13], (L, RWKV_WIDTH), 0.02),
        'rwkv_r_k': -0.04 + nrm(ks[14], (L, RWKV_HEADS, RWKV_HEAD_DIM), 0.02),
        'rwkv_ln_w': 1.0 + nrm(ks[15], (L, RWKV_WIDTH), 0.02),
        'rwkv_ln_b': nrm(ks[16], (L, RWKV_WIDTH), 0.02),
        'w_branch_a': nrm(ks[17], (L, GLA_V_WIDTH, D_MODEL), GLA_V_WIDTH ** -0.5),
        'w_branch_b': nrm(ks[18], (L, RWKV_WIDTH, D_MODEL), RWKV_WIDTH ** -0.5),
        'w_out': nrm(ks[19], (L, D_MODEL, D_MODEL), D_MODEL ** -0.5),
        'norm_ffn_g': 1.0 + nrm(ks[20], (L, D_MODEL), 0.02),
        'router_w': nrm(ks[21], (L, D_MODEL, N_EXPERTS), D_MODEL ** -0.5),
        'router_b': nrm(ks[22], (L, N_EXPERTS), 0.01),
        'expert_w1': nrm(ks[23], (L, N_EXPERTS, D_MODEL, 2 * EXPERT_FF), D_MODEL ** -0.5),
        'expert_b1': nrm(ks[24], (L, N_EXPERTS, 2 * EXPERT_FF), 0.01),
        'expert_w2': nrm(ks[25], (L, N_EXPERTS, EXPERT_FF, D_MODEL), EXPERT_FF ** -0.5),
        'expert_b2': nrm(ks[26], (L, N_EXPERTS, D_MODEL), 0.01),
        'norm_final_g': 1.0 + nrm(ks[27], (D_MODEL,), 0.02),
    }


def reference(x, norm_mix_g, w_in, gla_gate_up, gla_gate_bias, gla_norm_g, rwkv_mu, rwkv_w0, rwkv_w2, rwkv_a0,
              rwkv_a2, rwkv_g2, rwkv_k_k, rwkv_k_a, rwkv_r_k, rwkv_ln_w, rwkv_ln_b, w_branch_a, w_branch_b, w_out,
              norm_ffn_g, router_w, router_b, expert_w1, expert_b1, expert_w2, expert_b2, norm_final_g):
    for l in range(DEPTH):
        h = rms_norm(x, norm_mix_g[l])
        x = x + hybrid_mixer(h, w_in[l], gla_gate_up[l], gla_gate_bias[l], gla_norm_g[l], rwkv_mu[l], rwkv_w0[l],
                             rwkv_w2[l], rwkv_a0[l], rwkv_a2[l], rwkv_g2[l], rwkv_k_k[l], rwkv_k_a[l], rwkv_r_k[l],
                             rwkv_ln_w[l], rwkv_ln_b[l], w_branch_a[l], w_branch_b[l], w_out[l])
        h = rms_norm(x, norm_ffn_g[l])
        x = x + moe_ffn(h, router_w[l], router_b[l], expert_w1[l], expert_b1[l], expert_w2[l], expert_b2[l])
    return rms_norm(x, norm_final_g)
```

```python
import functools

import jax
import jax.numpy as jnp
from jax import lax
from jax.experimental import pallas as pl
from jax.experimental.pallas import tpu as pltpu

F32 = jnp.float32
BF16 = jnp.bfloat16
HIGHEST = lax.Precision.HIGHEST

LANES = 128
VMEM_LIMIT = 56 * 1024 * 1024

RMS_EPS = 1e-6
GLA_HEADS = 4
GLA_DK = 128
GLA_DV = 256
GLA_GATE_RANK = 16
GLA_TAU = 16.0
GLA_CHUNK = 64
GLA_NORM_EPS = 1e-5
RWKV_HEAD_DIM = 64
RWKV_GN_EPS = 64e-5
TOP_K = 4
SWIGLU_ALPHA = 1.702
SWIGLU_LIMIT = 7.0
MOE_ROWS = 512


def _cparams(*sem):
    return pltpu.CompilerParams(dimension_semantics=sem, vmem_limit_bytes=VMEM_LIMIT)


def _rms(xf, gain):
    return xf * lax.rsqrt(jnp.mean(xf * xf, axis=-1, keepdims=True) + RMS_EPS) * gain


def _softplus(y):
    return jnp.maximum(y, 0.0) + jnp.log1p(jnp.exp(-jnp.abs(y)))


def _sigmoid(y):
    return 1.0 / (1.0 + jnp.exp(-y))


def _dot(a, b):
    return jnp.dot(a, b, preferred_element_type=F32)


def _dot_hi(a, b):
    return jnp.dot(a, b, preferred_element_type=F32, precision=HIGHEST)


def _gla_proj_kernel(x_ref, g_ref, w_ref, up_ref, bias_ref, q_ref, k_ref, v_ref, og_ref, la_ref):
    qk = q_ref.shape[-1]
    vw = v_ref.shape[-1]
    h = _rms(x_ref[0], g_ref[...]).astype(BF16)
    z = _dot(h, w_ref[...])
    q_ref[0] = z[:, :qk].astype(BF16)
    k_ref[0] = z[:, qk:2 * qk].astype(BF16)
    v_ref[0] = z[:, 2 * qk:2 * qk + vw].astype(BF16)
    og_ref[0] = z[:, 2 * qk + vw:2 * qk + 2 * vw].astype(BF16)
    lr = z[:, 2 * qk + 2 * vw:].astype(BF16)
    pre = _dot(lr, up_ref[...]) + bias_ref[...]
    la_ref[0] = -_softplus(-pre) * (1.0 / GLA_TAU)


def _gla_proj(x, gain, w, up, bias, tm):
    B, S, D = x.shape
    qk = bias.shape[-1]
    vw = (w.shape[1] - 2 * qk - LANES) // 2
    tok = lambda width: pl.BlockSpec((1, tm, width), lambda b, i: (b, i, 0))
    full = lambda a: pl.BlockSpec(a.shape, lambda b, i: (0,) * a.ndim)
    return pl.pallas_call(
        _gla_proj_kernel,
        grid=(B, S // tm),
        in_specs=[tok(D), full(gain), full(w), full(up), full(bias)],
        out_specs=[tok(qk), tok(qk), tok(vw), tok(vw), tok(qk)],
        out_shape=[jax.ShapeDtypeStruct((B, S, qk), BF16), jax.ShapeDtypeStruct((B, S, qk), BF16),
                   jax.ShapeDtypeStruct((B, S, vw), BF16), jax.ShapeDtypeStruct((B, S, vw), BF16),
                   jax.ShapeDtypeStruct((B, S, qk), F32)],
        compiler_params=_cparams("parallel", "parallel"),
        name="gla_proj",
    )(x, gain, w, up, bias)


def _gla_kernel(q_ref, k_ref, v_ref, og_ref, la_ref, ng_ref, y_ref, st_ref):
    C = GLA_CHUNK
    tc = q_ref.shape[1]

    @pl.when(pl.program_id(1) == 0)
    def _():
        st_ref[...] = jnp.zeros_like(st_ref)

    row = lax.broadcasted_iota(jnp.int32, (C, C), 0)
    col = lax.broadcasted_iota(jnp.int32, (C, C), 1)
    causal = row >= col
    tri = causal.astype(F32)
    scale = GLA_DK ** -0.5
    for c in range(tc // C):
        rows = slice(c * C, (c + 1) * C)
        for h in range(GLA_HEADS):
            kc = slice(h * GLA_DK, (h + 1) * GLA_DK)
            vc = slice(h * GLA_DV, (h + 1) * GLA_DV)
            b = _dot_hi(tri, la_ref[0, rows, kc])
            b_mid = b[C // 2:C // 2 + 1]
            b_end = b[C - 1:C]
            qh = q_ref[0, rows, kc].astype(F32) * scale
            kh = k_ref[0, rows, kc].astype(F32)
            vh = v_ref[0, rows, vc]
            qa = (qh * jnp.exp(b - b_mid)).astype(BF16)
            ka = (kh * jnp.exp(b_mid - b)).astype(BF16)
            scores = lax.dot_general(qa, ka, (((1,), (1,)), ((), ())), preferred_element_type=F32)
            scores = jnp.where(causal, scores, 0.0).astype(BF16)
            st = st_ref[h]
            q_in = (qh * jnp.exp(b)).astype(BF16)
            o = _dot(scores, vh) + lax.dot_general(q_in, st.astype(BF16), (((1,), (1,)), ((), ())),
                                                   preferred_element_type=F32)
            k_st = (kh * jnp.exp(b_end - b)).astype(BF16)
            st_ref[h] = st * jnp.exp(b_end) + lax.dot_general(
                vh, k_st, (((0,), (0,)), ((), ())), preferred_element_type=F32)
            o = o * lax.rsqrt(jnp.mean(o * o, axis=-1, keepdims=True) + GLA_NORM_EPS) * ng_ref[...]
            og = og_ref[0, rows, vc].astype(F32)
            y_ref[0, rows, vc] = (o * (og * _sigmoid(og))).astype(BF16)


def _gla(q, k, v, og, la, norm_g, tc):
    B, S, qk = q.shape
    vw = v.shape[-1]
    tok = lambda width: pl.BlockSpec((1, tc, width), lambda b, i: (b, i, 0))
    return pl.pallas_call(
        _gla_kernel,
        grid=(B, S // tc),
        in_specs=[tok(qk), tok(qk), tok(vw), tok(vw), tok(qk),
                  pl.BlockSpec(norm_g.shape, lambda b, i: (0, 0))],
        out_specs=tok(vw),
        out_shape=jax.ShapeDtypeStruct((B, S, vw), BF16),
        scratch_shapes=[pltpu.VMEM((GLA_HEADS, GLA_DV, GLA_DK), F32)],
        compiler_params=_cparams("parallel", "arbitrary"),
        name="gla_chunk",
    )(q, k, v, og, la, norm_g)


def _rwkv_proj_kernel(x_ref, g_ref, w_ref, mu_ref, w0_ref, w2_ref, a0_ref, a2_ref, g2_ref, kk_ref, ka_ref,
                      rk_ref, hsum_ref, hbc_ref,
                      r_ref, w_out_ref, k_ref, v_ref, a_ref, b_ref, gate_ref, bonus_ref, carry_ref):
    W = r_ref.shape[-1]
    tm = x_ref.shape[1]

    @pl.when(pl.program_id(1) == 0)
    def _():
        carry_ref[...] = jnp.zeros_like(carry_ref)

    h = _rms(x_ref[0], g_ref[...]).astype(BF16)
    z = _dot(h, w_ref[...])
    rolled = pltpu.roll(z, 1, axis=0)
    first = lax.broadcasted_iota(jnp.int32, z.shape, 0) == 0
    prev = jnp.where(first, carry_ref[0:1, :], rolled)
    carry_ref[0:1, :] = z[tm - 1:tm, :]
    u = z + mu_ref[...] * (prev - z)

    r = u[:, :W]
    k = u[:, W:2 * W]
    v = u[:, 2 * W:3 * W]
    lora = u[:, 3 * W:3 * W + LANES]
    xg = u[:, 3 * W + LANES:]
    w_log = -_softplus(-(w0_ref[...] + _dot(jnp.tanh(lora).astype(BF16), w2_ref[...]))) - 0.5
    decay = jnp.exp(-jnp.exp(w_log))
    a = _sigmoid(a0_ref[...] + _dot(lora.astype(BF16), a2_ref[...]))
    gate = _dot(_sigmoid(xg).astype(BF16), g2_ref[...])

    def head_sum(t):
        return _dot_hi(_dot_hi(t, hsum_ref[...]), hbc_ref[...])

    kk = k * kk_ref[...]
    kk = kk / jnp.maximum(jnp.sqrt(head_sum(kk * kk)), 1e-12)
    kmod = k * (1.0 + (a - 1.0) * ka_ref[...])
    r_ref[0] = r
    w_out_ref[0] = decay
    k_ref[0] = kmod
    v_ref[0] = v
    a_ref[0] = -kk
    b_ref[0] = kk * a
    gate_ref[0] = gate
    bonus_ref[0] = head_sum(r * kmod * rk_ref[...]) * v


def _rwkv_proj(x, gain, w, mu, w0, w2p, a0, a2p, g2, k_k, k_a, r_k, hsum, hbc, tm):
    B, S, D = x.shape
    W = w0.shape[-1]
    tok = lambda width: pl.BlockSpec((1, tm, width), lambda b, i: (b, i, 0))
    full = lambda a: pl.BlockSpec(a.shape, lambda b, i: (0,) * a.ndim)
    consts = (gain, w, mu, w0, w2p, a0, a2p, g2, k_k, k_a, r_k, hsum, hbc)
    return pl.pallas_call(
        _rwkv_proj_kernel,
        grid=(B, S // tm),
        in_specs=[tok(D)] + [full(c) for c in consts],
        out_specs=[tok(W)] * 8,
        out_shape=[jax.ShapeDtypeStruct((B, S, W), F32)] * 8,
        scratch_shapes=[pltpu.VMEM((8, w.shape[1]), F32)],
        compiler_params=_cparams("parallel", "arbitrary"),
        name="rwkv_proj",
    )(x, *consts)


def _rwkv_scan_kernel(r_ref, w_ref, k_ref, v_ref, a_ref, b_ref, y_ref, st_ref, sa_ref):
    tt, G, N, _ = r_ref.shape
    half = N // 2

    @pl.when(pl.program_id(0) == 0)
    def _():
        st_ref[...] = jnp.zeros_like(st_ref)

    def group_step(tg, carry):
        t = tg // G
        g = tg % G
        acc = [jnp.zeros((N, LANES), F32)] * 2
        for j in range(N):
            acc[j % 2] = acc[j % 2] + st_ref[g, j] * a_ref[t, g, j:j + 1, :]
        sa_ref[...] = acc[0] + acc[1]

        def half_step(ih, c):
            rows = pl.ds(pl.multiple_of(ih * half, half), half)
            sa_h = sa_ref[rows, :]
            vt = v_ref[t, g, rows, :]
            y = jnp.zeros((half, LANES), F32)
            for j in range(N):
                row = slice(j, j + 1)
                s_new = (st_ref[g, j, rows, :] * w_ref[t, g, row, :] + sa_h * b_ref[t, g, row, :]
                         + vt * k_ref[t, g, row, :])
                st_ref[g, j, rows, :] = s_new
                y = y + s_new * r_ref[t, g, row, :]
            y_ref[t, g, rows, :] = y
            return c

        return lax.fori_loop(0, 2, half_step, carry)

    lax.fori_loop(0, tt * G, group_step, 0)


def _rwkv_scan(r, w, k, v, a, b, tt):
    S, G, N, _ = r.shape
    blk = pl.BlockSpec((tt, G, N, LANES), lambda i: (i, 0, 0, 0))
    return pl.pallas_call(
        _rwkv_scan_kernel,
        grid=(S // tt,),
        in_specs=[blk] * 6,
        out_specs=blk,
        out_shape=jax.ShapeDtypeStruct((S, G, N, LANES), F32),
        scratch_shapes=[pltpu.VMEM((G, N, N, LANES), F32), pltpu.VMEM((N, LANES), F32)],
        compiler_params=_cparams("arbitrary"),
        name="rwkv_scan",
    )(r, w, k, v, a, b)


def _merge_kernel(x_ref, ya_ref, ys_ref, gate_ref, bonus_ref, g_mix_ref, wg_ref, lnw_ref, lnb_ref, hmean_ref,
                  hbc_ref, wa_ref, wb_ref, wo_ref, g_ffn_ref, rw_ref, rb_ref,
                  x1_ref, h2_ref, e_ref, rank_ref, p_ref, cnt_ref, carry_ref, *, n_experts):
    tm, D = x_ref.shape

    @pl.when(pl.program_id(0) == 0)
    def _():
        carry_ref[...] = jnp.zeros_like(carry_ref)

    x = x_ref[...]
    h = _rms(x, g_mix_ref[...]).astype(BF16)
    gates = _sigmoid(_dot(h, wg_ref[...]))

    y = ys_ref[...]
    mean = _dot_hi(_dot_hi(y, hmean_ref[...]), hbc_ref[...])
    d = y - mean
    var = _dot_hi(_dot_hi(d * d, hmean_ref[...]), hbc_ref[...])
    yb = d * lax.rsqrt(var + RWKV_GN_EPS) * lnw_ref[...] + lnb_ref[...] + bonus_ref[...]
    yb = (yb * gate_ref[...]).astype(BF16)

    merged = gates[:, :D] * _dot(ya_ref[...], wa_ref[...]) + gates[:, D:] * _dot(yb, wb_ref[...])
    x1 = x + _dot(merged.astype(BF16), wo_ref[...])
    x1_ref[...] = x1
    h2 = _rms(x1, g_ffn_ref[...])
    h2_ref[...] = h2.astype(BF16)

    lane = lax.broadcasted_iota(jnp.int32, (tm, LANES), 1)
    logits = jnp.where(lane < n_experts, _dot_hi(h2, rw_ref[...]) + rb_ref[...], -jnp.inf)
    vals, idxs, hots = [], [], []
    for _ in range(TOP_K):
        m = jnp.max(logits, axis=-1, keepdims=True)
        idx = jnp.min(jnp.where(logits == m, lane, LANES), axis=-1, keepdims=True)
        hot = lane == idx
        vals.append(m)
        idxs.append(idx)
        hots.append(hot)
        logits = jnp.where(hot, -jnp.inf, logits)
    exps = [jnp.exp(val - vals[0]) for val in vals]
    denom = functools.reduce(lambda s, e: s + e, exps)
    chosen = functools.reduce(lambda s, e: s + e, [hot.astype(F32) for hot in hots])
    r_i = lax.broadcasted_iota(jnp.int32, (tm, tm), 0)
    c_i = lax.broadcasted_iota(jnp.int32, (tm, tm), 1)
    before = _dot((r_i > c_i).astype(BF16), chosen.astype(BF16)) + carry_ref[0:1, :]
    e_out = jnp.zeros((tm, LANES), jnp.int32)
    rank_out = jnp.zeros((tm, LANES), jnp.int32)
    p_out = jnp.zeros((tm, LANES), F32)
    for s in range(TOP_K):
        rank = jnp.sum(jnp.where(hots[s], before, 0.0), axis=-1, keepdims=True)
        e_out = jnp.where(lane == s, idxs[s], e_out)
        rank_out = jnp.where(lane == s, rank.astype(jnp.int32), rank_out)
        p_out = jnp.where(lane == s, exps[s] / denom, p_out)
    e_ref[...] = e_out
    rank_ref[...] = rank_out
    p_ref[...] = p_out
    total = carry_ref[0:1, :] + jnp.sum(chosen, axis=0, keepdims=True)
    carry_ref[0:1, :] = total
    cnt_ref[...] = jnp.broadcast_to(total, cnt_ref.shape).astype(jnp.int32)


def _merge(x, ya, ys, gate, bonus, consts, n_experts, tm):
    T, D = x.shape
    W = ys.shape[-1]
    tok = lambda width: pl.BlockSpec((tm, width), lambda i: (i, 0))
    full = lambda a: pl.BlockSpec(a.shape, lambda i: (0,) * a.ndim)
    return pl.pallas_call(
        functools.partial(_merge_kernel, n_experts=n_experts),
        grid=(T // tm,),
        in_specs=[tok(D), tok(W), tok(W), tok(W), tok(W)] + [full(c) for c in consts],
        out_specs=[tok(D), tok(D), tok(LANES), tok(LANES), tok(LANES),
                   pl.BlockSpec((8, LANES), lambda i: (0, 0))],
        out_shape=[jax.ShapeDtypeStruct((T, D), F32), jax.ShapeDtypeStruct((T, D), BF16),
                   jax.ShapeDtypeStruct((T, LANES), jnp.int32), jax.ShapeDtypeStruct((T, LANES), jnp.int32),
                   jax.ShapeDtypeStruct((T, LANES), F32), jax.ShapeDtypeStruct((8, LANES), jnp.int32)],
        scratch_shapes=[pltpu.VMEM((8, LANES), F32)],
        compiler_params=_cparams("arbitrary"),
        name="merge_router",
    )(x, ya, ys, gate, bonus, *consts)


def _expert_kernel(be_ref, x_ref, w1g_ref, w1l_ref, b1g_ref, b1l_ref, w2_ref, b2_ref, o_ref):
    del be_ref
    x = x_ref[...]
    glu = jnp.minimum(_dot(x, w1g_ref[0]) + b1g_ref[0], SWIGLU_LIMIT)
    lin = jnp.clip(_dot(x, w1l_ref[0]) + b1l_ref[0], -SWIGLU_LIMIT, SWIGLU_LIMIT)
    act = glu * _sigmoid(SWIGLU_ALPHA * glu) * (lin + 1.0)
    o_ref[...] = _dot(act.astype(BF16), w2_ref[0]) + b2_ref[0]


def _experts(block_expert, xs, w1g, w1l, b1g, b1l, w2, b2):
    R, D = xs.shape
    FF = w2.shape[1]
    nb = R // MOE_ROWS
    per_e = lambda shape: pl.BlockSpec((1,) + shape, lambda i, be: (be[i], 0, 0))
    return pl.pallas_call(
        _expert_kernel,
        grid_spec=pltpu.PrefetchScalarGridSpec(
            num_scalar_prefetch=1,
            grid=(nb,),
            in_specs=[pl.BlockSpec((MOE_ROWS, D), lambda i, be: (i, 0)),
                      per_e((D, FF)), per_e((D, FF)), per_e((1, FF)), per_e((1, FF)),
                      per_e((FF, D)), per_e((1, D))],
            out_specs=pl.BlockSpec((MOE_ROWS, D), lambda i, be: (i, 0)),
        ),
        out_shape=jax.ShapeDtypeStruct((R, D), F32),
        compiler_params=_cparams("arbitrary"),
        name="expert_ffn",
    )(block_expert, xs, w1g, w1l, b1g, b1l, w2, b2)


def _final_kernel(x1_ref, rows_ref, p_ref, g_ref, o_ref):
    D = x1_ref.shape[-1]
    acc = x1_ref[...]
    p = p_ref[...]
    for s in range(TOP_K):
        acc = acc + p[:, s:s + 1] * rows_ref[:, s * D:(s + 1) * D]
    o_ref[...] = _rms(acc, g_ref[...])


def _final(x1, rows, p, gain, tm):
    T, D = x1.shape
    return pl.pallas_call(
        _final_kernel,
        grid=(T // tm,),
        in_specs=[pl.BlockSpec((tm, D), lambda i: (i, 0)), pl.BlockSpec((tm, TOP_K * D), lambda i: (i, 0)),
                  pl.BlockSpec((tm, LANES), lambda i: (i, 0)), pl.BlockSpec(gain.shape, lambda i: (0, 0))],
        out_specs=pl.BlockSpec((tm, D), lambda i: (i, 0)),
        out_shape=jax.ShapeDtypeStruct((T, D), F32),
        compiler_params=_cparams("parallel"),
        name="combine_norm",
    )(x1, rows, p, gain)


def _pick_tile(n, pref):
    t = min(n, pref)
    assert n % t == 0, (n, t)
    return t


def kernel(x, norm_mix_g, w_in, gla_gate_up, gla_gate_bias, gla_norm_g, rwkv_mu, rwkv_w0, rwkv_w2, rwkv_a0, rwkv_a2, rwkv_g2, rwkv_k_k, rwkv_k_a, rwkv_r_k, rwkv_ln_w, rwkv_ln_b, w_branch_a, w_branch_b, w_out, norm_ffn_g, router_w, router_b, expert_w1, expert_b1, expert_w2, expert_b2, norm_final_g):
    assert norm_mix_g.shape[0] == 1, "single-layer block"
    B, S, D = x.shape
    T = B * S
    qk = gla_gate_bias.shape[-1]
    vw = GLA_HEADS * GLA_DV
    W = rwkv_w0.shape[-1]
    H = W // RWKV_HEAD_DIM
    E = router_w.shape[-1]
    assert qk == GLA_HEADS * GLA_DK and B * H % LANES == 0 and E <= LANES
    row = lambda a: a.reshape(1, -1)

    wi = w_in[0]
    gla_cols = 2 * qk + vw + GLA_GATE_RANK + vw
    rwkv_cols = rwkv_mu.shape[-1]
    o_q, o_k, o_v, o_lr, o_og = 0, qk, 2 * qk, 2 * qk + vw, 2 * qk + vw + GLA_GATE_RANK
    w_gla = jnp.concatenate(
        [wi[:, o_q:o_lr], wi[:, o_og:gla_cols], wi[:, o_lr:o_og],
         jnp.zeros((D, LANES - GLA_GATE_RANK), F32)], axis=1).astype(BF16)
    up = jnp.concatenate([gla_gate_up[0], jnp.zeros((LANES - GLA_GATE_RANK, qk), F32)], axis=0).astype(BF16)
    w_rwkv = wi[:, gla_cols:gla_cols + rwkv_cols].astype(BF16)
    w_gate = wi[:, gla_cols + rwkv_cols:].astype(BF16)
    n_w, n_a = rwkv_w2.shape[1], rwkv_a2.shape[1]
    assert n_w + n_a == LANES
    w2p = jnp.concatenate([rwkv_w2[0], jnp.zeros((n_a, W), F32)], axis=0).astype(BF16)
    a2p = jnp.concatenate([jnp.zeros((n_w, W), F32), rwkv_a2[0]], axis=0).astype(BF16)
    head_of = jnp.arange(W) // RWKV_HEAD_DIM
    hsum = (head_of[:, None] == jnp.arange(LANES)[None, :]).astype(F32)
    hbc = hsum.T

    tm_gla = _pick_tile(S, 512)
    q, k, v, og, la = _gla_proj(x, row(norm_mix_g), w_gla, up, row(gla_gate_bias), tm_gla)
    ya = _gla(q, k, v, og, la, row(gla_norm_g), _pick_tile(S, 256))

    tm_r = _pick_tile(S, 256)
    outs = _rwkv_proj(x, row(norm_mix_g), w_rwkv, row(rwkv_mu), row(rwkv_w0), w2p, row(rwkv_a0), a2p,
                      rwkv_g2[0].astype(BF16), row(rwkv_k_k), row(rwkv_k_a), row(rwkv_r_k), hsum, hbc, tm_r)
    r_, w_, k_, v_, a_, b_, gate_, bonus_ = outs
    G = B * H // LANES
    bg = B // G
    to_scan = lambda t: t.reshape(G, bg, S, H, RWKV_HEAD_DIM).transpose(2, 0, 4, 1, 3).reshape(
        S, G, RWKV_HEAD_DIM, LANES)
    ys = _rwkv_scan(*(to_scan(t) for t in (r_, w_, k_, v_, a_, b_)), _pick_tile(S, 16))
    ys = ys.reshape(S, G, RWKV_HEAD_DIM, bg, H).transpose(1, 3, 0, 4, 2).reshape(T, W)

    rw = jnp.concatenate([router_w[0], jnp.zeros((D, LANES - E), F32)], axis=1)
    rb = jnp.concatenate([router_b[0], jnp.zeros((LANES - E,), F32)]).reshape(1, LANES)
    consts = (row(norm_mix_g), w_gate, row(rwkv_ln_w), row(rwkv_ln_b), hsum * (1.0 / RWKV_HEAD_DIM), hbc,
              w_branch_a[0].astype(BF16), w_branch_b[0].astype(BF16), w_out[0].astype(BF16),
              row(norm_ffn_g), rw, rb)
    tm_m = _pick_tile(T, 256)
    x1, h2, e_sel, rank, p_sel, counts = _merge(
        x.reshape(T, D), ya.reshape(T, vw), ys, gate_.reshape(T, W), bonus_.reshape(T, W), consts, E, tm_m)

    counts = counts[0, :E]
    padded = (counts + MOE_ROWS - 1) // MOE_ROWS * MOE_ROWS
    pad_ends = jnp.cumsum(padded)
    pad_starts = pad_ends - padded
    nb = -(-T * TOP_K // MOE_ROWS) + E
    dest = pad_starts[e_sel[:, :TOP_K]] + rank[:, :TOP_K]
    block_expert = jnp.minimum(jnp.searchsorted(pad_ends, jnp.arange(nb) * MOE_ROWS, side='right'),
                               E - 1).astype(jnp.int32)
    rows_tok = jnp.zeros((nb * MOE_ROWS,), jnp.int32).at[dest.reshape(-1)].set(
        jnp.arange(T * TOP_K, dtype=jnp.int32) // TOP_K)
    xs = h2[rows_tok]

    FF = expert_w2.shape[2]
    w1 = expert_w1[0].reshape(E, D, FF, 2)
    b1 = expert_b1[0].reshape(E, 1, FF, 2)
    outs_e = _experts(block_expert, xs, w1[..., 0].astype(BF16), w1[..., 1].astype(BF16), b1[..., 0], b1[..., 1],
                      expert_w2[0].astype(BF16), expert_b2[0].reshape(E, 1, D))

    picked = outs_e[dest.reshape(-1)].reshape(T, TOP_K * D)
    out = _final(x1, picked, p_sel, row(norm_final_g), _pick_tile(T, 256))
    return out.reshape(B, S, D)
```

```python
import functools

import jax
import jax.numpy as jnp
from jax import lax
from jax.experimental import pallas as pl
from jax.experimental.pallas import tpu as pltpu

F32 = jnp.float32
BF16 = jnp.bfloat16
HIGHEST = lax.Precision.HIGHEST

LANES = 128
VMEM_LIMIT = 56 * 1024 * 1024

RMS_EPS = 1e-6
GLA_HEADS = 4
GLA_DK = 128
GLA_DV = 256
GLA_GATE_RANK = 16
GLA_TAU = 16.0
GLA_CHUNK = 64
GLA_NORM_EPS = 1e-5
RWKV_HEAD_DIM = 64
RWKV_GN_EPS = 64e-5
TOP_K = 4
SWIGLU_ALPHA = 1.702
SWIGLU_LIMIT = 7.0
MOE_ROWS = 512


def _cparams(*sem):
    return pltpu.CompilerParams(dimension_semantics=sem, vmem_limit_bytes=VMEM_LIMIT)


def _rms(xf, gain):
    return xf * lax.rsqrt(jnp.mean(xf * xf, axis=-1, keepdims=True) + RMS_EPS) * gain


def _softplus(y):
    return jnp.maximum(y, 0.0) + jnp.log1p(jnp.exp(-jnp.abs(y)))


def _sigmoid(y):
    return 1.0 / (1.0 + jnp.exp(-y))


def _dot(a, b):
    return jnp.dot(a, b, preferred_element_type=F32)


def _dot_hi(a, b):
    return jnp.dot(a, b, preferred_element_type=F32, precision=HIGHEST)


def _gla_proj_kernel(x_ref, g_ref, w_ref, up_ref, bias_ref, q_ref, k_ref, v_ref, og_ref, la_ref):
    qk = q_ref.shape[-1]
    vw = v_ref.shape[-1]
    h = _rms(x_ref[0], g_ref[...]).astype(BF16)
    z = _dot(h, w_ref[...])
    q_ref[0] = z[:, :qk].astype(BF16)
    k_ref[0] = z[:, qk:2 * qk].astype(BF16)
    v_ref[0] = z[:, 2 * qk:2 * qk + vw].astype(BF16)
    og_ref[0] = z[:, 2 * qk + vw:2 * qk + 2 * vw].astype(BF16)
    lr = z[:, 2 * qk + 2 * vw:].astype(BF16)
    pre = _dot(lr, up_ref[...]) + bias_ref[...]
    la_ref[0] = -_softplus(-pre) * (1.0 / GLA_TAU)


def _gla_proj(x, gain, w, up, bias, tm):
    B, S, D = x.shape
    qk = bias.shape[-1]
    vw = (w.shape[1] - 2 * qk - LANES) // 2
    tok = lambda width: pl.BlockSpec((1, tm, width), lambda b, i: (b, i, 0))
    full = lambda a: pl.BlockSpec(a.shape, lambda b, i: (0,) * a.ndim)
    return pl.pallas_call(
        _gla_proj_kernel,
        grid=(B, S // tm),
        in_specs=[tok(D), full(gain), full(w), full(up), full(bias)],
        out_specs=[tok(qk), tok(qk), tok(vw), tok(vw), tok(qk)],
        out_shape=[jax.ShapeDtypeStruct((B, S, qk), BF16), jax.ShapeDtypeStruct((B, S, qk), BF16),
                   jax.ShapeDtypeStruct((B, S, vw), BF16), jax.ShapeDtypeStruct((B, S, vw), BF16),
                   jax.ShapeDtypeStruct((B, S, qk), F32)],
        compiler_params=_cparams("parallel", "parallel"),
        name="gla_proj",
    )(x, gain, w, up, bias)


def _gla_kernel(q_ref, k_ref, v_ref, og_ref, la_ref, ng_ref, y_ref, st_ref):
    C = GLA_CHUNK
    tc = q_ref.shape[1]

    @pl.when(pl.program_id(1) == 0)
    def _():
        st_ref[...] = jnp.zeros_like(st_ref)

    row = lax.broadcasted_iota(jnp.int32, (C, C), 0)
    col = lax.broadcasted_iota(jnp.int32, (C, C), 1)
    causal = row >= col
    tri = causal.astype(F32)
    scale = GLA_DK ** -0.5
    for c in range(tc // C):
        rows = slice(c * C, (c + 1) * C)
        for h in range(GLA_HEADS):
            kc = slice(h * GLA_DK, (h + 1) * GLA_DK)
            vc = slice(h * GLA_DV, (h + 1) * GLA_DV)
            b = _dot_hi(tri, la_ref[0, rows, kc])
            b_mid = b[C // 2:C // 2 + 1]
            b_end = b[C - 1:C]
            qh = q_ref[0, rows, kc].astype(F32) * scale
            kh = k_ref[0, rows, kc].astype(F32)
            vh = v_ref[0, rows, vc]
            qa = (qh * jnp.exp(b - b_mid)).astype(BF16)
            ka = (kh * jnp.exp(b_mid - b)).astype(BF16)
            scores = lax.dot_general(qa, ka, (((1,), (1,)), ((), ())), preferred_element_type=F32)
            scores = jnp.where(causal, scores, 0.0).astype(BF16)
            st = st_ref[h]
            q_in = (qh * jnp.exp(b)).astype(BF16)
            o = _dot(scores, vh) + lax.dot_general(q_in, st.astype(BF16), (((1,), (1,)), ((), ())),
                                                   preferred_element_type=F32)
            k_st = (kh * jnp.exp(b_end - b)).astype(BF16)
            st_ref[h] = st * jnp.exp(b_end) + lax.dot_general(
                vh, k_st, (((0,), (0,)), ((), ())), preferred_element_type=F32)
            o = o * lax.rsqrt(jnp.mean(o * o, axis=-1, keepdims=True) + GLA_NORM_EPS) * ng_ref[...]
            og = og_ref[0, rows, vc].astype(F32)
            y_ref[0, rows, vc] = (o * (og * _sigmoid(og))).astype(BF16)


def _gla(q, k, v, og, la, norm_g, tc):
    B, S, qk = q.shape
    vw = v.shape[-1]
    tok = lambda width: pl.BlockSpec((1, tc, width), lambda b, i: (b, i, 0))
    return pl.pallas_call(
        _gla_kernel,
        grid=(B, S // tc),
        in_specs=[tok(qk), tok(qk), tok(vw), tok(vw), tok(qk),
                  pl.BlockSpec(norm_g.shape, lambda b, i: (0, 0))],
        out_specs=tok(vw),
        out_shape=jax.ShapeDtypeStruct((B, S, vw), BF16),
        scratch_shapes=[pltpu.VMEM((GLA_HEADS, GLA_DV, GLA_DK), F32)],
        compiler_params=_cparams("parallel", "arbitrary"),
        name="gla_chunk",
    )(q, k, v, og, la, norm_g)


def _rwkv_proj_kernel(x_ref, g_ref, w_ref, mu_ref, w0_ref, w2_ref, a0_ref, a2_ref, g2_ref,
                      r_ref, w_out_ref, k_ref, v_ref, a_ref, gate_ref, carry_ref):
    W = r_ref.shape[-1]
    tm = x_ref.shape[1]

    @pl.when(pl.program_id(1) == 0)
    def _():
        carry_ref[...] = jnp.zeros_like(carry_ref)

    h = _rms(x_ref[0], g_ref[...]).astype(BF16)
    z = _dot(h, w_ref[...])
    rolled = pltpu.roll(z, 1, axis=0)
    first = lax.broadcasted_iota(jnp.int32, z.shape, 0) == 0
    prev = jnp.where(first, carry_ref[0:1, :], rolled)
    carry_ref[0:1, :] = z[tm - 1:tm, :]
    u = z + mu_ref[...] * (prev - z)

    r = u[:, :W]
    k = u[:, W:2 * W]
    v = u[:, 2 * W:3 * W]
    lora = u[:, 3 * W:3 * W + LANES]
    xg = u[:, 3 * W + LANES:]
    w_log = -_softplus(-(w0_ref[...] + _dot(jnp.tanh(lora).astype(BF16), w2_ref[...]))) - 0.5
    decay = jnp.exp(-jnp.exp(w_log))
    r_ref[0] = r
    w_out_ref[0] = decay
    k_ref[0] = k
    v_ref[0] = v
    a_ref[0] = _sigmoid(a0_ref[...] + _dot(lora.astype(BF16), a2_ref[...]))
    gate_ref[0] = _dot(_sigmoid(xg).astype(BF16), g2_ref[...]).astype(BF16)


def _rwkv_proj(x, gain, w, mu, w0, w2p, a0, a2p, g2, tm):
    B, S, D = x.shape
    W = w0.shape[-1]
    tok = lambda width: pl.BlockSpec((1, tm, width), lambda b, i: (b, i, 0))
    full = lambda a: pl.BlockSpec(a.shape, lambda b, i: (0,) * a.ndim)
    consts = (gain, w, mu, w0, w2p, a0, a2p, g2)
    return pl.pallas_call(
        _rwkv_proj_kernel,
        grid=(B, S // tm),
        in_specs=[tok(D)] + [full(c) for c in consts],
        out_specs=[tok(W)] * 6,
        out_shape=[jax.ShapeDtypeStruct((B, S, W), F32)] * 5 + [jax.ShapeDtypeStruct((B, S, W), BF16)],
        scratch_shapes=[pltpu.VMEM((8, w.shape[1]), F32)],
        compiler_params=_cparams("parallel", "arbitrary"),
        name="rwkv_proj",
    )(x, *consts)


def _rwkv_scan_kernel(r_ref, w_ref, k_ref, v_ref, a_ref, kk_ref, ka_ref, rk_ref, lnw_ref, lnb_ref,
                      y_ref, st_ref, sa_ref, av_ref, bv_ref, km_ref):
    tt, G, N, _ = r_ref.shape
    half = N // 2

    @pl.when(pl.program_id(0) == 0)
    def _():
        st_ref[...] = jnp.zeros_like(st_ref)

    def group_step(tg, carry):
        t = tg // G
        g = tg % G
        k_raw = k_ref[t, g]
        iclr = a_ref[t, g]
        kk = k_raw * kk_ref[...]
        kk = kk / jnp.maximum(jnp.sqrt(jnp.sum(kk * kk, axis=0, keepdims=True)), 1e-12)
        k_mod = k_raw * (1.0 + (iclr - 1.0) * ka_ref[...])
        av_ref[...] = -kk
        bv_ref[...] = kk * iclr
        km_ref[...] = k_mod

        acc = [jnp.zeros((N, LANES), F32)] * 2
        for j in range(N):
            acc[j % 2] = acc[j % 2] + st_ref[g, j] * av_ref[j:j + 1, :]
        sa_ref[...] = acc[0] + acc[1]

        def half_step(ih, c):
            rows = pl.ds(pl.multiple_of(ih * half, half), half)
            sa_h = sa_ref[rows, :]
            vt = v_ref[t, g, rows, :]
            y = jnp.zeros((half, LANES), F32)
            for j in range(N):
                row = slice(j, j + 1)
                s_new = (st_ref[g, j, rows, :] * w_ref[t, g, row, :] + sa_h * bv_ref[row, :]
                         + vt * km_ref[row, :])
                st_ref[g, j, rows, :] = s_new
                y = y + s_new * r_ref[t, g, row, :]
            y_ref[t, g, rows, :] = y
            return c

        lax.fori_loop(0, 2, half_step, 0)

        y = y_ref[t, g]
        mean = jnp.mean(y, axis=0, keepdims=True)
        d = y - mean
        var = jnp.mean(d * d, axis=0, keepdims=True)
        bonus = jnp.sum(r_ref[t, g] * k_mod * rk_ref[...], axis=0, keepdims=True)
        y_ref[t, g] = d * lax.rsqrt(var + RWKV_GN_EPS) * lnw_ref[...] + lnb_ref[...] + bonus * v_ref[t, g]
        return carry

    lax.fori_loop(0, tt * G, group_step, 0)


def _rwkv_scan(r, w, k, v, a, k_k, k_a, r_k, ln_w, ln_b, tt):
    S, G, N, _ = r.shape
    blk = pl.BlockSpec((tt, G, N, LANES), lambda i: (i, 0, 0, 0))
    par = pl.BlockSpec((N, LANES), lambda i: (0, 0))
    tile = pltpu.VMEM((N, LANES), F32)
    return pl.pallas_call(
        _rwkv_scan_kernel,
        grid=(S // tt,),
        in_specs=[blk] * 5 + [par] * 5,
        out_specs=blk,
        out_shape=jax.ShapeDtypeStruct((S, G, N, LANES), F32),
        scratch_shapes=[pltpu.VMEM((G, N, N, LANES), F32), tile, tile, tile, tile],
        compiler_params=_cparams("arbitrary"),
        name="rwkv_scan",
    )(r, w, k, v, a, k_k, k_a, r_k, ln_w, ln_b)


def _merge_kernel(x_ref, ya_ref, ys_ref, gate_ref, g_mix_ref, wg_ref, wa_ref, wb_ref, wo_ref, g_ffn_ref,
                  rw_ref, rb_ref,
                  x1_ref, h2_ref, e_ref, rank_ref, p_ref, cnt_ref, carry_ref, *, n_experts):
    tm, D = x_ref.shape

    @pl.when(pl.program_id(0) == 0)
    def _():
        carry_ref[...] = jnp.zeros_like(carry_ref)

    x = x_ref[...]
    h = _rms(x, g_mix_ref[...]).astype(BF16)
    gates = _sigmoid(_dot(h, wg_ref[...]))

    yb = (ys_ref[...] * gate_ref[...].astype(F32)).astype(BF16)

    merged = gates[:, :D] * _dot(ya_ref[...], wa_ref[...]) + gates[:, D:] * _dot(yb, wb_ref[...])
    x1 = x + _dot(merged.astype(BF16), wo_ref[...])
    x1_ref[...] = x1
    h2 = _rms(x1, g_ffn_ref[...])
    h2_ref[...] = h2.astype(BF16)

    lane = lax.broadcasted_iota(jnp.int32, (tm, LANES), 1)
    logits = jnp.where(lane < n_experts, _dot_hi(h2, rw_ref[...]) + rb_ref[...], -jnp.inf)
    vals, idxs, hots = [], [], []
    for _ in range(TOP_K):
        m = jnp.max(logits, axis=-1, keepdims=True)
        idx = jnp.min(jnp.where(logits == m, lane, LANES), axis=-1, keepdims=True)
        hot = lane == idx
        vals.append(m)
        idxs.append(idx)
        hots.append(hot)
        logits = jnp.where(hot, -jnp.inf, logits)
    exps = [jnp.exp(val - vals[0]) for val in vals]
    denom = functools.reduce(lambda s, e: s + e, exps)
    chosen = functools.reduce(lambda s, e: s + e, [hot.astype(F32) for hot in hots])
    r_i = lax.broadcasted_iota(jnp.int32, (tm, tm), 0)
    c_i = lax.broadcasted_iota(jnp.int32, (tm, tm), 1)
    before = _dot((r_i > c_i).astype(BF16), chosen.astype(BF16)) + carry_ref[0:1, :]
    e_out = jnp.zeros((tm, LANES), jnp.int32)
    rank_out = jnp.zeros((tm, LANES), jnp.int32)
    p_out = jnp.zeros((tm, LANES), F32)
    for s in range(TOP_K):
        rank = jnp.sum(jnp.where(hots[s], before, 0.0), axis=-1, keepdims=True)
        e_out = jnp.where(lane == s, idxs[s], e_out)
        rank_out = jnp.where(lane == s, rank.astype(jnp.int32), rank_out)
        p_out = jnp.where(lane == s, exps[s] / denom, p_out)
    e_ref[...] = e_out
    rank_ref[...] = rank_out
    p_ref[...] = p_out
    total = carry_ref[0:1, :] + jnp.sum(chosen, axis=0, keepdims=True)
    carry_ref[0:1, :] = total
    cnt_ref[...] = jnp.broadcast_to(total, cnt_ref.shape).astype(jnp.int32)


def _merge(x, ya, ys, gate, consts, n_experts, tm):
    T, D = x.shape
    W = ys.shape[-1]
    tok = lambda width: pl.BlockSpec((tm, width), lambda i: (i, 0))
    full = lambda a: pl.BlockSpec(a.shape, lambda i: (0,) * a.ndim)
    return pl.pallas_call(
        functools.partial(_merge_kernel, n_experts=n_experts),
        grid=(T // tm,),
        in_specs=[tok(D), tok(W), tok(W), tok(W)] + [full(c) for c in consts],
        out_specs=[tok(D), tok(D), tok(LANES), tok(LANES), tok(LANES),
                   pl.BlockSpec((8, LANES), lambda i: (0, 0))],
        out_shape=[jax.ShapeDtypeStruct((T, D), F32), jax.ShapeDtypeStruct((T, D), BF16),
                   jax.ShapeDtypeStruct((T, LANES), jnp.int32), jax.ShapeDtypeStruct((T, LANES), jnp.int32),
                   jax.ShapeDtypeStruct((T, LANES), F32), jax.ShapeDtypeStruct((8, LANES), jnp.int32)],
        scratch_shapes=[pltpu.VMEM((8, LANES), F32)],
        compiler_params=_cparams("arbitrary"),
        name="merge_router",
    )(x, ya, ys, gate, *consts)


def _split_w1_kernel(w_ref, glu_ref, lin_ref):
    cols = w_ref.shape[-1]
    src = lax.broadcasted_iota(jnp.int32, (cols, cols // 2), 0)
    dst = lax.broadcasted_iota(jnp.int32, (cols, cols // 2), 1)
    w = w_ref[0].astype(BF16)
    glu_ref[0] = _dot(w, (src == 2 * dst).astype(BF16)).astype(BF16)
    lin_ref[0] = _dot(w, (src == 2 * dst + 1).astype(BF16)).astype(BF16)


def _split_w1(w1, cols=512):
    E, D, FF2 = w1.shape
    out = jax.ShapeDtypeStruct((E, D, FF2 // 2), BF16)
    return pl.pallas_call(
        _split_w1_kernel,
        grid=(E, FF2 // cols),
        in_specs=[pl.BlockSpec((1, D, cols), lambda e, c: (e, 0, c))],
        out_specs=[pl.BlockSpec((1, D, cols // 2), lambda e, c: (e, 0, c))] * 2,
        out_shape=[out, out],
        compiler_params=_cparams("parallel", "parallel"),
        name="split_w1",
    )(w1)


def _expert_kernel(be_ref, nu_ref, x_ref, w1g_ref, w1l_ref, b1g_ref, b1l_ref, w2_ref, b2_ref, o_ref):
    del be_ref
    used = pl.program_id(0) < nu_ref[0]

    @pl.when(used)
    def _():
        x = x_ref[...]
        glu = jnp.minimum(_dot(x, w1g_ref[0]) + b1g_ref[0], SWIGLU_LIMIT)
        lin = jnp.clip(_dot(x, w1l_ref[0]) + b1l_ref[0], -SWIGLU_LIMIT, SWIGLU_LIMIT)
        act = glu * _sigmoid(SWIGLU_ALPHA * glu) * (lin + 1.0)
        o_ref[...] = _dot(act.astype(BF16), w2_ref[0]) + b2_ref[0]

    @pl.when(jnp.logical_not(used))
    def _():
        o_ref[...] = jnp.zeros_like(o_ref)


def _experts(block_expert, n_used, xs, w1g, w1l, b1g, b1l, w2, b2):
    R, D = xs.shape
    FF = w2.shape[1]
    nb = R // MOE_ROWS
    per_e = lambda shape: pl.BlockSpec((1,) + shape, lambda i, be, nu: (be[i], 0, 0))
    return pl.pallas_call(
        _expert_kernel,
        grid_spec=pltpu.PrefetchScalarGridSpec(
            num_scalar_prefetch=2,
            grid=(nb,),
            in_specs=[pl.BlockSpec((MOE_ROWS, D), lambda i, be, nu: (i, 0)),
                      per_e((D, FF)), per_e((D, FF)), per_e((1, FF)), per_e((1, FF)),
                      per_e((FF, D)), per_e((1, D))],
            out_specs=pl.BlockSpec((MOE_ROWS, D), lambda i, be, nu: (i, 0)),
        ),
        out_shape=jax.ShapeDtypeStruct((R, D), F32),
        compiler_params=_cparams("arbitrary"),
        name="expert_ffn",
    )(block_expert, n_used, xs, w1g, w1l, b1g, b1l, w2, b2)


def _final_kernel(x1_ref, rows_ref, p_ref, g_ref, o_ref):
    D = x1_ref.shape[-1]
    acc = x1_ref[...]
    p = p_ref[...]
    for s in range(TOP_K):
        acc = acc + p[:, s:s + 1] * rows_ref[:, s * D:(s + 1) * D]
    o_ref[...] = _rms(acc, g_ref[...])


def _final(x1, rows, p, gain, tm):
    T, D = x1.shape
    return pl.pallas_call(
        _final_kernel,
        grid=(T // tm,),
        in_specs=[pl.BlockSpec((tm, D), lambda i: (i, 0)), pl.BlockSpec((tm, TOP_K * D), lambda i: (i, 0)),
                  pl.BlockSpec((tm, LANES), lambda i: (i, 0)), pl.BlockSpec(gain.shape, lambda i: (0, 0))],
        out_specs=pl.BlockSpec((tm, D), lambda i: (i, 0)),
        out_shape=jax.ShapeDtypeStruct((T, D), F32),
        compiler_params=_cparams("parallel"),
        name="combine_norm",
    )(x1, rows, p, gain)


def _pick_tile(n, pref):
    t = min(n, pref)
    assert n % t == 0, (n, t)
    return t


def kernel(x, norm_mix_g, w_in, gla_gate_up, gla_gate_bias, gla_norm_g, rwkv_mu, rwkv_w0, rwkv_w2, rwkv_a0, rwkv_a2, rwkv_g2, rwkv_k_k, rwkv_k_a, rwkv_r_k, rwkv_ln_w, rwkv_ln_b, w_branch_a, w_branch_b, w_out, norm_ffn_g, router_w, router_b, expert_w1, expert_b1, expert_w2, expert_b2, norm_final_g):
    assert norm_mix_g.shape[0] == 1, "single-layer block"
    B, S, D = x.shape
    T = B * S
    qk = gla_gate_bias.shape[-1]
    vw = GLA_HEADS * GLA_DV
    W = rwkv_w0.shape[-1]
    H = W // RWKV_HEAD_DIM
    E = router_w.shape[-1]
    assert qk == GLA_HEADS * GLA_DK and B * H % LANES == 0 and E <= LANES
    row = lambda a: a.reshape(1, -1)

    wi = w_in[0]
    gla_cols = 2 * qk + vw + GLA_GATE_RANK + vw
    rwkv_cols = rwkv_mu.shape[-1]
    o_q, o_k, o_v, o_lr, o_og = 0, qk, 2 * qk, 2 * qk + vw, 2 * qk + vw + GLA_GATE_RANK
    w_gla = jnp.concatenate(
        [wi[:, o_q:o_lr], wi[:, o_og:gla_cols], wi[:, o_lr:o_og],
         jnp.zeros((D, LANES - GLA_GATE_RANK), F32)], axis=1).astype(BF16)
    up = jnp.concatenate([gla_gate_up[0], jnp.zeros((LANES - GLA_GATE_RANK, qk), F32)], axis=0).astype(BF16)
    w_rwkv = wi[:, gla_cols:gla_cols + rwkv_cols].astype(BF16)
    w_gate = wi[:, gla_cols + rwkv_cols:].astype(BF16)
    n_w, n_a = rwkv_w2.shape[1], rwkv_a2.shape[1]
    assert n_w + n_a == LANES
    w2p = jnp.concatenate([rwkv_w2[0], jnp.zeros((n_a, W), F32)], axis=0).astype(BF16)
    a2p = jnp.concatenate([jnp.zeros((n_w, W), F32), rwkv_a2[0]], axis=0).astype(BF16)

    tm_gla = _pick_tile(S, 512)
    q, k, v, og, la = _gla_proj(x, row(norm_mix_g), w_gla, up, row(gla_gate_bias), tm_gla)
    ya = _gla(q, k, v, og, la, row(gla_norm_g), _pick_tile(S, 256))

    tm_r = _pick_tile(S, 256)
    r_, w_, k_, v_, a_, gate_ = _rwkv_proj(x, row(norm_mix_g), w_rwkv, row(rwkv_mu), row(rwkv_w0), w2p,
                                           row(rwkv_a0), a2p, rwkv_g2[0].astype(BF16), tm_r)
    G = B * H // LANES
    bg = B // G
    to_scan = lambda t: t.reshape(G, bg, S, H, RWKV_HEAD_DIM).transpose(2, 0, 4, 1, 3).reshape(
        S, G, RWKV_HEAD_DIM, LANES)
    par = lambda p: jnp.tile(p.reshape(H, RWKV_HEAD_DIM).T, (1, bg))
    ys = _rwkv_scan(*(to_scan(t) for t in (r_, w_, k_, v_, a_)), par(rwkv_k_k), par(rwkv_k_a), par(rwkv_r_k),
                    par(rwkv_ln_w), par(rwkv_ln_b), _pick_tile(S, 16))
    ys = ys.reshape(S, G, RWKV_HEAD_DIM, bg, H).transpose(1, 3, 0, 4, 2).reshape(T, W)

    rw = jnp.concatenate([router_w[0], jnp.zeros((D, LANES - E), F32)], axis=1)
    rb = jnp.concatenate([router_b[0], jnp.zeros((LANES - E,), F32)]).reshape(1, LANES)
    consts = (row(norm_mix_g), w_gate, w_branch_a[0].astype(BF16), w_branch_b[0].astype(BF16),
              w_out[0].astype(BF16), row(norm_ffn_g), rw, rb)
    tm_m = _pick_tile(T, 256)
    x1, h2, e_sel, rank, p_sel, counts = _merge(
        x.reshape(T, D), ya.reshape(T, vw), ys, gate_.reshape(T, W), consts, E, tm_m)

    counts = counts[0, :E]
    padded = (counts + MOE_ROWS - 1) // MOE_ROWS * MOE_ROWS
    pad_ends = jnp.cumsum(padded)
    pad_starts = pad_ends - padded
    nb = -(-T * TOP_K // MOE_ROWS) + E
    dest = pad_starts[e_sel[:, :TOP_K]] + rank[:, :TOP_K]
    block_start = jnp.arange(nb, dtype=jnp.int32) * MOE_ROWS
    block_expert = jnp.minimum(jnp.sum(pad_ends[None, :] <= block_start[:, None], axis=1), E - 1).astype(jnp.int32)
    n_used = (pad_ends[-1:] // MOE_ROWS).astype(jnp.int32)
    rows_tok = jnp.zeros((nb * MOE_ROWS,), jnp.int32).at[dest.reshape(-1)].set(
        jnp.arange(T * TOP_K, dtype=jnp.int32) // TOP_K)
    xs = h2[rows_tok]

    FF = expert_w2.shape[2]
    w1g, w1l = _split_w1(expert_w1[0])
    b1 = expert_b1[0].reshape(E, 1, FF, 2)
    outs_e = _experts(block_expert, n_used, xs, w1g, w1l, b1[..., 0], b1[..., 1],
                      expert_w2[0].astype(BF16), expert_b2[0].reshape(E, 1, D))

    picked = outs_e[dest.reshape(-1)].reshape(T, TOP_K * D)
    out = _final(x1, picked, p_sel, row(norm_final_g), _pick_tile(T, 256))
    return out.reshape(B, S, D)
```

```python
import functools

import jax
import jax.numpy as jnp
from jax import lax
from jax.experimental import pallas as pl
from jax.experimental.pallas import tpu as pltpu

F32 = jnp.float32
BF16 = jnp.bfloat16
U32 = jnp.uint32
HIGHEST = lax.Precision.HIGHEST

LANES = 128
VMEM_LIMIT = 56 * 1024 * 1024

RMS_EPS = 1e-6
GLA_HEADS = 4
GLA_DK = 128
GLA_DV = 256
GLA_GATE_RANK = 16
GLA_TAU = 16.0
GLA_CHUNK = 64
GLA_NORM_EPS = 1e-5
RWKV_HEAD_DIM = 64
RWKV_GN_EPS = 64e-5
TOP_K = 4
SWIGLU_ALPHA = 1.702
SWIGLU_LIMIT = 7.0
MOE_ROWS = 512
ISSUE_GROUP = 4


def _cparams(*sem):
    return pltpu.CompilerParams(dimension_semantics=sem, vmem_limit_bytes=VMEM_LIMIT)


def _rms(xf, gain):
    return xf * lax.rsqrt(jnp.mean(xf * xf, axis=-1, keepdims=True) + RMS_EPS) * gain


def _softplus(y):
    return jnp.maximum(y, 0.0) + jnp.log1p(jnp.exp(-jnp.abs(y)))


def _sigmoid(y):
    return 1.0 / (1.0 + jnp.exp(-y))


def _dot(a, b):
    return jnp.dot(a, b, preferred_element_type=F32)


def _dot_hi(a, b):
    return jnp.dot(a, b, preferred_element_type=F32, precision=HIGHEST)


def _pack_rows(x):
    n = x.shape[-1] // 2
    bits = lambda t: lax.bitcast_convert_type(t.astype(BF16).astype(F32), U32)
    return (bits(x[:, :n]) >> 16) | bits(x[:, n:])


def _unpack_rows(w):
    lo = lax.bitcast_convert_type(w << 16, F32)
    hi = lax.bitcast_convert_type(w & jnp.uint32(0xFFFF0000), F32)
    return lo, hi


def _store_packed(ref, x):
    packed = _pack_rows(x)
    for c in range(ref.shape[1]):
        ref[:, c, :] = packed[:, c * LANES:(c + 1) * LANES]


def _load_packed(ref):
    return _unpack_rows(jnp.concatenate([ref[:, c, :] for c in range(ref.shape[1])], axis=1))


def _gla_proj_kernel(x_ref, g_ref, w_ref, up_ref, bias_ref, q_ref, k_ref, v_ref, og_ref, la_ref):
    qk = q_ref.shape[-1]
    vw = v_ref.shape[-1]
    h = _rms(x_ref[0], g_ref[...]).astype(BF16)
    z = _dot(h, w_ref[...])
    q_ref[0] = z[:, :qk].astype(BF16)
    k_ref[0] = z[:, qk:2 * qk].astype(BF16)
    v_ref[0] = z[:, 2 * qk:2 * qk + vw].astype(BF16)
    og_ref[0] = z[:, 2 * qk + vw:2 * qk + 2 * vw].astype(BF16)
    lr = z[:, 2 * qk + 2 * vw:].astype(BF16)
    pre = _dot(lr, up_ref[...]) + bias_ref[...]
    la_ref[0] = -_softplus(-pre) * (1.0 / GLA_TAU)


def _gla_proj(x, gain, w, up, bias, tm):
    B, S, D = x.shape
    qk = bias.shape[-1]
    vw = (w.shape[1] - 2 * qk - LANES) // 2
    tok = lambda width: pl.BlockSpec((1, tm, width), lambda b, i: (b, i, 0))
    full = lambda a: pl.BlockSpec(a.shape, lambda b, i: (0,) * a.ndim)
    return pl.pallas_call(
        _gla_proj_kernel,
        grid=(B, S // tm),
        in_specs=[tok(D), full(gain), full(w), full(up), full(bias)],
        out_specs=[tok(qk), tok(qk), tok(vw), tok(vw), tok(qk)],
        out_shape=[jax.ShapeDtypeStruct((B, S, qk), BF16), jax.ShapeDtypeStruct((B, S, qk), BF16),
                   jax.ShapeDtypeStruct((B, S, vw), BF16), jax.ShapeDtypeStruct((B, S, vw), BF16),
                   jax.ShapeDtypeStruct((B, S, qk), F32)],
        compiler_params=_cparams("parallel", "parallel"),
        name="gla_proj",
    )(x, gain, w, up, bias)


def _gla_kernel(q_ref, k_ref, v_ref, og_ref, la_ref, ng_ref, y_ref, st_ref):
    C = GLA_CHUNK
    tc = q_ref.shape[1]

    @pl.when(pl.program_id(1) == 0)
    def _():
        st_ref[...] = jnp.zeros_like(st_ref)

    row = lax.broadcasted_iota(jnp.int32, (C, C), 0)
    col = lax.broadcasted_iota(jnp.int32, (C, C), 1)
    causal = row >= col
    tri = causal.astype(F32)
    scale = GLA_DK ** -0.5
    for c in range(tc // C):
        rows = slice(c * C, (c + 1) * C)
        for h in range(GLA_HEADS):
            kc = slice(h * GLA_DK, (h + 1) * GLA_DK)
            vc = slice(h * GLA_DV, (h + 1) * GLA_DV)
            b = _dot_hi(tri, la_ref[0, rows, kc])
            b_mid = b[C // 2:C // 2 + 1]
            b_end = b[C - 1:C]
            qh = q_ref[0, rows, kc].astype(F32) * scale
            kh = k_ref[0, rows, kc].astype(F32)
            vh = v_ref[0, rows, vc]
            qa = (qh * jnp.exp(b - b_mid)).astype(BF16)
            ka = (kh * jnp.exp(b_mid - b)).astype(BF16)
            scores = lax.dot_general(qa, ka, (((1,), (1,)), ((), ())), preferred_element_type=F32)
            scores = jnp.where(causal, scores, 0.0).astype(BF16)
            st = st_ref[h]
            q_in = (qh * jnp.exp(b)).astype(BF16)
            o = _dot(scores, vh) + lax.dot_general(q_in, st.astype(BF16), (((1,), (1,)), ((), ())),
                                                   preferred_element_type=F32)
            k_st = (kh * jnp.exp(b_end - b)).astype(BF16)
            st_ref[h] = st * jnp.exp(b_end) + lax.dot_general(
                vh, k_st, (((0,), (0,)), ((), ())), preferred_element_type=F32)
            o = o * lax.rsqrt(jnp.mean(o * o, axis=-1, keepdims=True) + GLA_NORM_EPS) * ng_ref[...]
            og = og_ref[0, rows, vc].astype(F32)
            y_ref[0, rows, vc] = (o * (og * _sigmoid(og))).astype(BF16)


def _gla(q, k, v, og, la, norm_g, tc):
    B, S, qk = q.shape
    vw = v.shape[-1]
    tok = lambda width: pl.BlockSpec((1, tc, width), lambda b, i: (b, i, 0))
    return pl.pallas_call(
        _gla_kernel,
        grid=(B, S // tc),
        in_specs=[tok(qk), tok(qk), tok(vw), tok(vw), tok(qk),
                  pl.BlockSpec(norm_g.shape, lambda b, i: (0, 0))],
        out_specs=tok(vw),
        out_shape=jax.ShapeDtypeStruct((B, S, vw), BF16),
        scratch_shapes=[pltpu.VMEM((GLA_HEADS, GLA_DV, GLA_DK), F32)],
        compiler_params=_cparams("parallel", "arbitrary"),
        name="gla_chunk",
    )(q, k, v, og, la, norm_g)


def _rwkv_proj_kernel(x_ref, g_ref, w_ref, mu_ref, w0_ref, w2_ref, a0_ref, a2_ref, g2_ref,
                      r_ref, w_out_ref, k_ref, v_ref, a_ref, gate_ref, carry_ref):
    W = r_ref.shape[-1]
    tm = x_ref.shape[1]

    @pl.when(pl.program_id(1) == 0)
    def _():
        carry_ref[...] = jnp.zeros_like(carry_ref)

    h = _rms(x_ref[0], g_ref[...]).astype(BF16)
    z = _dot(h, w_ref[...])
    rolled = pltpu.roll(z, 1, axis=0)
    first = lax.broadcasted_iota(jnp.int32, z.shape, 0) == 0
    prev = jnp.where(first, carry_ref[0:1, :], rolled)
    carry_ref[0:1, :] = z[tm - 1:tm, :]
    u = z + mu_ref[...] * (prev - z)

    r = u[:, :W]
    k = u[:, W:2 * W]
    v = u[:, 2 * W:3 * W]
    lora = u[:, 3 * W:3 * W + LANES]
    xg = u[:, 3 * W + LANES:]
    w_log = -_softplus(-(w0_ref[...] + _dot(jnp.tanh(lora).astype(BF16), w2_ref[...]))) - 0.5
    decay = jnp.exp(-jnp.exp(w_log))
    r_ref[0] = r
    w_out_ref[0] = decay
    k_ref[0] = k
    v_ref[0] = v
    a_ref[0] = _sigmoid(a0_ref[...] + _dot(lora.astype(BF16), a2_ref[...]))
    gate_ref[0] = _dot(_sigmoid(xg).astype(BF16), g2_ref[...]).astype(BF16)


def _rwkv_proj(x, gain, w, mu, w0, w2p, a0, a2p, g2, tm):
    B, S, D = x.shape
    W = w0.shape[-1]
    tok = lambda width: pl.BlockSpec((1, tm, width), lambda b, i: (b, i, 0))
    full = lambda a: pl.BlockSpec(a.shape, lambda b, i: (0,) * a.ndim)
    consts = (gain, w, mu, w0, w2p, a0, a2p, g2)
    return pl.pallas_call(
        _rwkv_proj_kernel,
        grid=(B, S // tm),
        in_specs=[tok(D)] + [full(c) for c in consts],
        out_specs=[tok(W)] * 6,
        out_shape=[jax.ShapeDtypeStruct((B, S, W), F32)] * 5 + [jax.ShapeDtypeStruct((B, S, W), BF16)],
        scratch_shapes=[pltpu.VMEM((8, w.shape[1]), F32)],
        compiler_params=_cparams("parallel", "arbitrary"),
        name="rwkv_proj",
    )(x, *consts)


def _rwkv_scan_kernel(r_ref, w_ref, k_ref, v_ref, a_ref, kk_ref, ka_ref, rk_ref, lnw_ref, lnb_ref,
                      y_ref, st_ref, sa_ref, av_ref, bv_ref, km_ref):
    tt, G, N, _ = r_ref.shape
    half = N // 2

    @pl.when(pl.program_id(0) == 0)
    def _():
        st_ref[...] = jnp.zeros_like(st_ref)

    def group_step(tg, carry):
        t = tg // G
        g = tg % G
        k_raw = k_ref[t, g]
        iclr = a_ref[t, g]
        kk = k_raw * kk_ref[...]
        kk = kk / jnp.maximum(jnp.sqrt(jnp.sum(kk * kk, axis=0, keepdims=True)), 1e-12)
        k_mod = k_raw * (1.0 + (iclr - 1.0) * ka_ref[...])
        av_ref[...] = -kk
        bv_ref[...] = kk * iclr
        km_ref[...] = k_mod

        acc = [jnp.zeros((N, LANES), F32)] * 2
        for j in range(N):
            acc[j % 2] = acc[j % 2] + st_ref[g, j] * av_ref[j:j + 1, :]
        sa_ref[...] = acc[0] + acc[1]

        def half_step(ih, c):
            rows = pl.ds(pl.multiple_of(ih * half, half), half)
            sa_h = sa_ref[rows, :]
            vt = v_ref[t, g, rows, :]
            y = jnp.zeros((half, LANES), F32)
            for j in range(N):
                row = slice(j, j + 1)
                s_new = (st_ref[g, j, rows, :] * w_ref[t, g, row, :] + sa_h * bv_ref[row, :]
                         + vt * km_ref[row, :])
                st_ref[g, j, rows, :] = s_new
                y = y + s_new * r_ref[t, g, row, :]
            y_ref[t, g, rows, :] = y
            return c

        lax.fori_loop(0, 2, half_step, 0)

        y = y_ref[t, g]
        mean = jnp.mean(y, axis=0, keepdims=True)
        d = y - mean
        var = jnp.mean(d * d, axis=0, keepdims=True)
        bonus = jnp.sum(r_ref[t, g] * k_mod * rk_ref[...], axis=0, keepdims=True)
        y_ref[t, g] = d * lax.rsqrt(var + RWKV_GN_EPS) * lnw_ref[...] + lnb_ref[...] + bonus * v_ref[t, g]
        return carry

    lax.fori_loop(0, tt * G, group_step, 0)


def _rwkv_scan(r, w, k, v, a, k_k, k_a, r_k, ln_w, ln_b, tt):
    S, G, N, _ = r.shape
    blk = pl.BlockSpec((tt, G, N, LANES), lambda i: (i, 0, 0, 0))
    par = pl.BlockSpec((N, LANES), lambda i: (0, 0))
    tile = pltpu.VMEM((N, LANES), F32)
    return pl.pallas_call(
        _rwkv_scan_kernel,
        grid=(S // tt,),
        in_specs=[blk] * 5 + [par] * 5,
        out_specs=blk,
        out_shape=jax.ShapeDtypeStruct((S, G, N, LANES), F32),
        scratch_shapes=[pltpu.VMEM((G, N, N, LANES), F32), tile, tile, tile, tile],
        compiler_params=_cparams("arbitrary"),
        name="rwkv_scan",
    )(r, w, k, v, a, k_k, k_a, r_k, ln_w, ln_b)


def _merge_kernel(x_ref, ya_ref, ys_ref, gate_ref, g_mix_ref, wg_ref, wa_ref, wb_ref, wo_ref, g_ffn_ref,
                  rw_ref, rb_ref,
                  x1_ref, h2_ref, e_ref, rank_ref, p_ref, cnt_ref, carry_ref, *, n_experts):
    tm, D = x_ref.shape

    @pl.when(pl.program_id(0) == 0)
    def _():
        carry_ref[...] = jnp.zeros_like(carry_ref)

    x = x_ref[...]
    h = _rms(x, g_mix_ref[...]).astype(BF16)
    gates = _sigmoid(_dot(h, wg_ref[...]))

    yb = (ys_ref[...] * gate_ref[...].astype(F32)).astype(BF16)

    merged = gates[:, :D] * _dot(ya_ref[...], wa_ref[...]) + gates[:, D:] * _dot(yb, wb_ref[...])
    x1 = x + _dot(merged.astype(BF16), wo_ref[...])
    x1_ref[...] = x1
    h2 = _rms(x1, g_ffn_ref[...])
    _store_packed(h2_ref, h2)

    lane = lax.broadcasted_iota(jnp.int32, (tm, LANES), 1)
    logits = jnp.where(lane < n_experts, _dot_hi(h2, rw_ref[...]) + rb_ref[...], -jnp.inf)
    vals, idxs, hots = [], [], []
    for _ in range(TOP_K):
        m = jnp.max(logits, axis=-1, keepdims=True)
        idx = jnp.min(jnp.where(logits == m, lane, LANES), axis=-1, keepdims=True)
        hot = lane == idx
        vals.append(m)
        idxs.append(idx)
        hots.append(hot)
        logits = jnp.where(hot, -jnp.inf, logits)
    exps = [jnp.exp(val - vals[0]) for val in vals]
    denom = functools.reduce(lambda s, e: s + e, exps)
    chosen = functools.reduce(lambda s, e: s + e, [hot.astype(F32) for hot in hots])
    r_i = lax.broadcasted_iota(jnp.int32, (tm, tm), 0)
    c_i = lax.broadcasted_iota(jnp.int32, (tm, tm), 1)
    before = _dot((r_i > c_i).astype(BF16), chosen.astype(BF16)) + carry_ref[0:1, :]
    e_out = jnp.zeros((tm, LANES), jnp.int32)
    rank_out = jnp.zeros((tm, LANES), jnp.int32)
    p_out = jnp.zeros((tm, LANES), F32)
    for s in range(TOP_K):
        rank = jnp.sum(jnp.where(hots[s], before, 0.0), axis=-1, keepdims=True)
        e_out = jnp.where(lane == s, idxs[s], e_out)
        rank_out = jnp.where(lane == s, rank.astype(jnp.int32), rank_out)
        p_out = jnp.where(lane == s, exps[s] / denom, p_out)
    e_ref[...] = e_out
    rank_ref[...] = rank_out
    p_ref[...] = p_out
    total = carry_ref[0:1, :] + jnp.sum(chosen, axis=0, keepdims=True)
    carry_ref[0:1, :] = total
    cnt_ref[...] = jnp.broadcast_to(total, cnt_ref.shape).astype(jnp.int32)


def _merge(x, ya, ys, gate, consts, n_experts, tm):
    T, D = x.shape
    W = ys.shape[-1]
    tok = lambda width: pl.BlockSpec((tm, width), lambda i: (i, 0))
    full = lambda a: pl.BlockSpec(a.shape, lambda i: (0,) * a.ndim)
    return pl.pallas_call(
        functools.partial(_merge_kernel, n_experts=n_experts),
        grid=(T // tm,),
        in_specs=[tok(D), tok(W), tok(W), tok(W)] + [full(c) for c in consts],
        out_specs=[tok(D), pl.BlockSpec((tm, D // 2 // LANES, LANES), lambda i: (i, 0, 0)),
                   tok(LANES), tok(LANES), tok(LANES), pl.BlockSpec((8, LANES), lambda i: (0, 0))],
        out_shape=[jax.ShapeDtypeStruct((T, D), F32), jax.ShapeDtypeStruct((T, D // 2 // LANES, LANES), U32),
                   jax.ShapeDtypeStruct((T, LANES), jnp.int32), jax.ShapeDtypeStruct((T, LANES), jnp.int32),
                   jax.ShapeDtypeStruct((T, LANES), F32), jax.ShapeDtypeStruct((8, LANES), jnp.int32)],
        scratch_shapes=[pltpu.VMEM((8, LANES), F32)],
        compiler_params=_cparams("arbitrary"),
        name="merge_router",
    )(x, ya, ys, gate, *consts)


def _idx_copy(dest_ref, idx_ref, sem, tile, n_idx):
    slot = tile % 2
    return pltpu.make_async_copy(dest_ref.at[pl.ds(pl.multiple_of(tile * n_idx, n_idx), n_idx)],
                                 idx_ref.at[pl.ds(pl.multiple_of(slot * n_idx, n_idx), n_idx)], sem.at[slot])


def _dispatch_kernel(dest_ref, rows_ref, init_ref, xs_ref, idx_ref, idx_sem, row_sem, *, tm):
    del init_ref
    i = pl.program_id(0)
    n = pl.num_programs(0)
    n_idx = tm * TOP_K
    idx_copy = functools.partial(_idx_copy, dest_ref, idx_ref, idx_sem, n_idx=n_idx)

    @pl.when(i == 0)
    def _():
        idx_copy(i).start()

    idx_copy(i).wait()

    @pl.when(i + 1 < n)
    def _():
        idx_copy(i + 1).start()

    base = (i % 2) * n_idx

    def issue(g, c):
        r0 = g * ISSUE_GROUP
        dst = [idx_ref[base + r0 * TOP_K + k] for k in range(ISSUE_GROUP * TOP_K)]
        for k, d in enumerate(dst):
            pltpu.make_async_copy(rows_ref.at[i * tm + r0 + k // TOP_K], xs_ref.at[d], row_sem).start(priority=k % 2)
        return c

    lax.fori_loop(0, tm // ISSUE_GROUP, issue, 0)

    tile_wait = pltpu.make_async_copy(rows_ref.at[pl.ds(0, n_idx)], xs_ref.at[pl.ds(0, n_idx)], row_sem)

    @pl.when(i > 0)
    def _():
        tile_wait.wait()

    @pl.when(i == n - 1)
    def _():
        tile_wait.wait()


def _dispatch(dest, rows, n_rows, tm):
    T = rows.shape[0]
    packed = (n_rows,) + rows.shape[1:]
    any_spec = pl.BlockSpec(memory_space=pl.ANY)
    return pl.pallas_call(
        functools.partial(_dispatch_kernel, tm=tm),
        grid=(T // tm,),
        in_specs=[any_spec, any_spec, any_spec],
        out_specs=any_spec,
        out_shape=jax.ShapeDtypeStruct(packed, U32),
        input_output_aliases={2: 0},
        scratch_shapes=[pltpu.SMEM((2 * tm * TOP_K,), jnp.int32), pltpu.SemaphoreType.DMA((2,)),
                        pltpu.SemaphoreType.DMA],
        compiler_params=_cparams("arbitrary"),
        name="moe_dispatch",
    )(dest, rows, jnp.zeros(packed, U32))


def _split_w1_kernel(w_ref, glu_ref, lin_ref):
    cols = w_ref.shape[-1]
    src = lax.broadcasted_iota(jnp.int32, (cols, cols // 2), 0)
    dst = lax.broadcasted_iota(jnp.int32, (cols, cols // 2), 1)
    w = w_ref[0].astype(BF16)
    glu_ref[0] = _dot(w, (src == 2 * dst).astype(BF16)).astype(BF16)
    lin_ref[0] = _dot(w, (src == 2 * dst + 1).astype(BF16)).astype(BF16)


def _split_w1(w1, cols=512):
    E, D, FF2 = w1.shape
    out = jax.ShapeDtypeStruct((E, D, FF2 // 2), BF16)
    return pl.pallas_call(
        _split_w1_kernel,
        grid=(E, FF2 // cols),
        in_specs=[pl.BlockSpec((1, D, cols), lambda e, c: (e, 0, c))],
        out_specs=[pl.BlockSpec((1, D, cols // 2), lambda e, c: (e, 0, c))] * 2,
        out_shape=[out, out],
        compiler_params=_cparams("parallel", "parallel"),
        name="split_w1",
    )(w1)


def _expert_kernel(be_ref, nu_ref, x_ref, w1g_ref, w1l_ref, b1g_ref, b1l_ref, w2_ref, b2_ref, o_ref):
    del be_ref
    used = pl.program_id(0) < nu_ref[0]

    @pl.when(used)
    def _():
        lo, hi = _load_packed(x_ref)
        x = jnp.concatenate([lo.astype(BF16), hi.astype(BF16)], axis=1)
        glu = jnp.minimum(_dot(x, w1g_ref[0]) + b1g_ref[0], SWIGLU_LIMIT)
        lin = jnp.clip(_dot(x, w1l_ref[0]) + b1l_ref[0], -SWIGLU_LIMIT, SWIGLU_LIMIT)
        act = glu * _sigmoid(SWIGLU_ALPHA * glu) * (lin + 1.0)
        _store_packed(o_ref, _dot(act.astype(BF16), w2_ref[0]) + b2_ref[0])

    @pl.when(jnp.logical_not(used))
    def _():
        o_ref[...] = jnp.zeros_like(o_ref)


def _experts(block_expert, n_used, xs, w1g, w1l, b1g, b1l, w2, b2):
    R = xs.shape[0]
    D, FF = w1g.shape[1:]
    nb = R // MOE_ROWS
    per_e = lambda shape: pl.BlockSpec((1,) + shape, lambda i, be, nu: (be[i], 0, 0))
    rows = pl.BlockSpec((MOE_ROWS,) + xs.shape[1:], lambda i, be, nu: (i, 0, 0))
    return pl.pallas_call(
        _expert_kernel,
        grid_spec=pltpu.PrefetchScalarGridSpec(
            num_scalar_prefetch=2,
            grid=(nb,),
            in_specs=[rows, per_e((D, FF)), per_e((D, FF)), per_e((1, FF)), per_e((1, FF)),
                      per_e((FF, D)), per_e((1, D))],
            out_specs=rows,
        ),
        out_shape=jax.ShapeDtypeStruct(xs.shape, U32),
        compiler_params=_cparams("arbitrary"),
        name="expert_ffn",
    )(block_expert, n_used, xs, w1g, w1l, b1g, b1l, w2, b2)


def _combine_kernel(dest_ref, rows_ref, x1_ref, p_ref, g_ref, o_ref, buf_ref, idx_ref, idx_sem, row_sem):
    tm, D = x1_ref.shape
    i = pl.program_id(0)
    n = pl.num_programs(0)
    n_idx = tm * TOP_K
    idx_copy = functools.partial(_idx_copy, dest_ref, idx_ref, idx_sem, n_idx=n_idx)

    def gather(tile):
        slot = tile % 2
        base = slot * n_idx

        def issue(g, c):
            r0 = g * ISSUE_GROUP
            src = [idx_ref[base + r0 * TOP_K + k] for k in range(ISSUE_GROUP * TOP_K)]
            for k, d in enumerate(src):
                pltpu.make_async_copy(rows_ref.at[d], buf_ref.at[slot, k % TOP_K, r0 + k // TOP_K],
                                      row_sem.at[slot]).start(priority=k % 2)
            return c

        lax.fori_loop(0, tm // ISSUE_GROUP, issue, 0)

    @pl.when(i == 0)
    def _():
        idx_copy(i).start()
        idx_copy(i).wait()
        gather(i)

        @pl.when(n > 1)
        def _():
            idx_copy(i + 1).start()

    @pl.when(i + 1 < n)
    def _():
        idx_copy(i + 1).wait()
        gather(i + 1)

    @pl.when(i + 2 < n)
    def _():
        idx_copy(i + 2).start()

    slot = i % 2
    for s in range(TOP_K):
        pltpu.make_async_copy(rows_ref.at[pl.ds(0, tm)], buf_ref.at[slot, s], row_sem.at[slot]).wait()

    half = D // 2
    x1 = x1_ref[...]
    acc_lo, acc_hi = x1[:, :half], x1[:, half:]
    p = p_ref[...]
    for s in range(TOP_K):
        lo, hi = _load_packed(buf_ref.at[slot, s])
        acc_lo = acc_lo + p[:, s:s + 1] * lo
        acc_hi = acc_hi + p[:, s:s + 1] * hi
    ms = (jnp.sum(acc_lo * acc_lo, axis=-1, keepdims=True) + jnp.sum(acc_hi * acc_hi, axis=-1, keepdims=True)) / D
    scale = lax.rsqrt(ms + RMS_EPS)
    o_ref[:, :half] = acc_lo * scale * g_ref[:, :half]
    o_ref[:, half:] = acc_hi * scale * g_ref[:, half:]


def _combine(dest, rows, x1, p, gain, tm):
    T, D = x1.shape
    any_spec = pl.BlockSpec(memory_space=pl.ANY)
    return pl.pallas_call(
        _combine_kernel,
        grid=(T // tm,),
        in_specs=[any_spec, any_spec, pl.BlockSpec((tm, D), lambda i: (i, 0)),
                  pl.BlockSpec((tm, LANES), lambda i: (i, 0)), pl.BlockSpec(gain.shape, lambda i: (0, 0))],
        out_specs=pl.BlockSpec((tm, D), lambda i: (i, 0)),
        out_shape=jax.ShapeDtypeStruct((T, D), F32),
        scratch_shapes=[pltpu.VMEM((2, TOP_K, tm) + rows.shape[1:], U32), pltpu.SMEM((2 * tm * TOP_K,), jnp.int32),
                        pltpu.SemaphoreType.DMA((2,)), pltpu.SemaphoreType.DMA((2,))],
        compiler_params=_cparams("arbitrary"),
        name="moe_combine_norm",
    )(dest, rows, x1, p, gain)


def _pick_tile(n, pref):
    t = min(n, pref)
    assert n % t == 0, (n, t)
    return t


def kernel(x, norm_mix_g, w_in, gla_gate_up, gla_gate_bias, gla_norm_g, rwkv_mu, rwkv_w0, rwkv_w2, rwkv_a0, rwkv_a2, rwkv_g2, rwkv_k_k, rwkv_k_a, rwkv_r_k, rwkv_ln_w, rwkv_ln_b, w_branch_a, w_branch_b, w_out, norm_ffn_g, router_w, router_b, expert_w1, expert_b1, expert_w2, expert_b2, norm_final_g):
    assert norm_mix_g.shape[0] == 1, "single-layer block"
    B, S, D = x.shape
    T = B * S
    qk = gla_gate_bias.shape[-1]
    vw = GLA_HEADS * GLA_DV
    W = rwkv_w0.shape[-1]
    H = W // RWKV_HEAD_DIM
    E = router_w.shape[-1]
    assert qk == GLA_HEADS * GLA_DK and B * H % LANES == 0 and E <= LANES
    row = lambda a: a.reshape(1, -1)

    wi = w_in[0]
    gla_cols = 2 * qk + vw + GLA_GATE_RANK + vw
    rwkv_cols = rwkv_mu.shape[-1]
    o_q, o_k, o_v, o_lr, o_og = 0, qk, 2 * qk, 2 * qk + vw, 2 * qk + vw + GLA_GATE_RANK
    w_gla = jnp.concatenate(
        [wi[:, o_q:o_lr], wi[:, o_og:gla_cols], wi[:, o_lr:o_og],
         jnp.zeros((D, LANES - GLA_GATE_RANK), F32)], axis=1).astype(BF16)
    up = jnp.concatenate([gla_gate_up[0], jnp.zeros((LANES - GLA_GATE_RANK, qk), F32)], axis=0).astype(BF16)
    w_rwkv = wi[:, gla_cols:gla_cols + rwkv_cols].astype(BF16)
    w_gate = wi[:, gla_cols + rwkv_cols:].astype(BF16)
    n_w, n_a = rwkv_w2.shape[1], rwkv_a2.shape[1]
    assert n_w + n_a == LANES
    w2p = jnp.concatenate([rwkv_w2[0], jnp.zeros((n_a, W), F32)], axis=0).astype(BF16)
    a2p = jnp.concatenate([jnp.zeros((n_w, W), F32), rwkv_a2[0]], axis=0).astype(BF16)

    tm_gla = _pick_tile(S, 512)
    q, k, v, og, la = _gla_proj(x, row(norm_mix_g), w_gla, up, row(gla_gate_bias), tm_gla)
    ya = _gla(q, k, v, og, la, row(gla_norm_g), _pick_tile(S, 256))

    tm_r = _pick_tile(S, 256)
    r_, w_, k_, v_, a_, gate_ = _rwkv_proj(x, row(norm_mix_g), w_rwkv, row(rwkv_mu), row(rwkv_w0), w2p,
                                           row(rwkv_a0), a2p, rwkv_g2[0].astype(BF16), tm_r)
    G = B * H // LANES
    bg = B // G
    to_scan = lambda t: t.reshape(G, bg, S, H, RWKV_HEAD_DIM).transpose(2, 0, 4, 1, 3).reshape(
        S, G, RWKV_HEAD_DIM, LANES)
    par = lambda p: jnp.tile(p.reshape(H, RWKV_HEAD_DIM).T, (1, bg))
    ys = _rwkv_scan(*(to_scan(t) for t in (r_, w_, k_, v_, a_)), par(rwkv_k_k), par(rwkv_k_a), par(rwkv_r_k),
                    par(rwkv_ln_w), par(rwkv_ln_b), _pick_tile(S, 16))
    ys = ys.reshape(S, G, RWKV_HEAD_DIM, bg, H).transpose(1, 3, 0, 4, 2).reshape(T, W)

    rw = jnp.concatenate([router_w[0], jnp.zeros((D, LANES - E), F32)], axis=1)
    rb = jnp.concatenate([router_b[0], jnp.zeros((LANES - E,), F32)]).reshape(1, LANES)
    consts = (row(norm_mix_g), w_gate, w_branch_a[0].astype(BF16), w_branch_b[0].astype(BF16),
              w_out[0].astype(BF16), row(norm_ffn_g), rw, rb)
    tm_m = _pick_tile(T, 256)
    x1, h2, e_sel, rank, p_sel, counts = _merge(
        x.reshape(T, D), ya.reshape(T, vw), ys, gate_.reshape(T, W), consts, E, tm_m)

    counts = counts[0, :E]
    padded = (counts + MOE_ROWS - 1) // MOE_ROWS * MOE_ROWS
    pad_ends = jnp.cumsum(padded)
    pad_starts = pad_ends - padded
    nb = -(-T * TOP_K // MOE_ROWS) + E
    dest = pad_starts[e_sel[:, :TOP_K]] + rank[:, :TOP_K]
    block_start = jnp.arange(nb, dtype=jnp.int32) * MOE_ROWS
    block_expert = jnp.minimum(jnp.sum(pad_ends[None, :] <= block_start[:, None], axis=1), E - 1).astype(jnp.int32)
    n_used = (pad_ends[-1:] // MOE_ROWS).astype(jnp.int32)
    dest = dest.reshape(-1).astype(jnp.int32)
    tm_d = _pick_tile(T, 256)
    xs = _dispatch(dest, h2, nb * MOE_ROWS, tm_d)

    FF = expert_w2.shape[2]
    w1g, w1l = _split_w1(expert_w1[0])
    b1 = expert_b1[0].reshape(E, 1, FF, 2)
    outs_e = _experts(block_expert, n_used, xs, w1g, w1l, b1[..., 0], b1[..., 1],
                      expert_w2[0].astype(BF16), expert_b2[0].reshape(E, 1, D))

    out = _combine(dest, outs_e, x1, p_sel, row(norm_final_g), tm_d)
    return out.reshape(B, S, D)
```

```python
import functools

import jax
import jax.numpy as jnp
from jax import lax
from jax.experimental import pallas as pl
from jax.experimental.pallas import tpu as pltpu

F32 = jnp.float32
BF16 = jnp.bfloat16
U32 = jnp.uint32
HIGHEST = lax.Precision.HIGHEST

LANES = 128
VMEM_LIMIT = 56 * 1024 * 1024

RMS_EPS = 1e-6
GLA_HEADS = 4
GLA_DK = 128
GLA_DV = 256
GLA_GATE_RANK = 16
GLA_TAU = 16.0
GLA_CHUNK = 64
GLA_NORM_EPS = 1e-5
RWKV_HEAD_DIM = 64
RWKV_GN_EPS = 64e-5
TOP_K = 4
SWIGLU_ALPHA = 1.702
SWIGLU_LIMIT = 7.0
MOE_ROWS = 512
ISSUE_GROUP = 4


def _cparams(*sem):
    return pltpu.CompilerParams(dimension_semantics=sem, vmem_limit_bytes=VMEM_LIMIT)


def _rms(xf, gain):
    return xf * lax.rsqrt(jnp.mean(xf * xf, axis=-1, keepdims=True) + RMS_EPS) * gain


def _softplus(y):
    return jnp.maximum(y, 0.0) + jnp.log1p(jnp.exp(-jnp.abs(y)))


def _sigmoid(y):
    return 1.0 / (1.0 + jnp.exp(-y))


def _dot(a, b):
    return jnp.dot(a, b, preferred_element_type=F32)


def _dot_hi(a, b):
    return jnp.dot(a, b, preferred_element_type=F32, precision=HIGHEST)


def _pack_rows(x):
    n = x.shape[-1] // 2
    bits = lambda t: lax.bitcast_convert_type(t.astype(BF16).astype(F32), U32)
    return (bits(x[:, :n]) >> 16) | bits(x[:, n:])


def _unpack_rows(w):
    lo = lax.bitcast_convert_type(w << 16, F32)
    hi = lax.bitcast_convert_type(w & jnp.uint32(0xFFFF0000), F32)
    return lo, hi


def _store_packed(ref, x):
    packed = _pack_rows(x)
    for c in range(ref.shape[1]):
        ref[:, c, :] = packed[:, c * LANES:(c + 1) * LANES]


def _load_packed(ref):
    return _unpack_rows(jnp.concatenate([ref[:, c, :] for c in range(ref.shape[1])], axis=1))


def _gla_proj_kernel(x_ref, g_ref, w_ref, up_ref, bias_ref, q_ref, k_ref, v_ref, og_ref, la_ref):
    qk = q_ref.shape[-1]
    vw = v_ref.shape[-1]
    h = _rms(x_ref[0], g_ref[...]).astype(BF16)
    z = _dot(h, w_ref[...])
    q_ref[0] = z[:, :qk].astype(BF16)
    k_ref[0] = z[:, qk:2 * qk].astype(BF16)
    v_ref[0] = z[:, 2 * qk:2 * qk + vw].astype(BF16)
    og_ref[0] = z[:, 2 * qk + vw:2 * qk + 2 * vw].astype(BF16)
    lr = z[:, 2 * qk + 2 * vw:].astype(BF16)
    pre = _dot(lr, up_ref[...]) + bias_ref[...]
    la_ref[0] = -_softplus(-pre) * (1.0 / GLA_TAU)


def _gla_proj(x, gain, w, up, bias, tm):
    B, S, D = x.shape
    qk = bias.shape[-1]
    vw = (w.shape[1] - 2 * qk - LANES) // 2
    tok = lambda width: pl.BlockSpec((1, tm, width), lambda b, i: (b, i, 0))
    full = lambda a: pl.BlockSpec(a.shape, lambda b, i: (0,) * a.ndim)
    return pl.pallas_call(
        _gla_proj_kernel,
        grid=(B, S // tm),
        in_specs=[tok(D), full(gain), full(w), full(up), full(bias)],
        out_specs=[tok(qk), tok(qk), tok(vw), tok(vw), tok(qk)],
        out_shape=[jax.ShapeDtypeStruct((B, S, qk), BF16), jax.ShapeDtypeStruct((B, S, qk), BF16),
                   jax.ShapeDtypeStruct((B, S, vw), BF16), jax.ShapeDtypeStruct((B, S, vw), BF16),
                   jax.ShapeDtypeStruct((B, S, qk), F32)],
        compiler_params=_cparams("parallel", "parallel"),
        name="gla_proj",
    )(x, gain, w, up, bias)


def _gla_kernel(q_ref, k_ref, v_ref, og_ref, la_ref, ng_ref, y_ref, st_ref):
    C = GLA_CHUNK
    tc = q_ref.shape[1]

    @pl.when(pl.program_id(1) == 0)
    def _():
        st_ref[...] = jnp.zeros_like(st_ref)

    row = lax.broadcasted_iota(jnp.int32, (C, C), 0)
    col = lax.broadcasted_iota(jnp.int32, (C, C), 1)
    causal = row >= col
    tri = causal.astype(F32)
    scale = GLA_DK ** -0.5
    for c in range(tc // C):
        rows = slice(c * C, (c + 1) * C)
        for h in range(GLA_HEADS):
            kc = slice(h * GLA_DK, (h + 1) * GLA_DK)
            vc = slice(h * GLA_DV, (h + 1) * GLA_DV)
            b = _dot_hi(tri, la_ref[0, rows, kc])
            b_mid = b[C // 2:C // 2 + 1]
            b_end = b[C - 1:C]
            qh = q_ref[0, rows, kc].astype(F32) * scale
            kh = k_ref[0, rows, kc].astype(F32)
            vh = v_ref[0, rows, vc]
            qa = (qh * jnp.exp(b - b_mid)).astype(BF16)
            ka = (kh * jnp.exp(b_mid - b)).astype(BF16)
            scores = lax.dot_general(qa, ka, (((1,), (1,)), ((), ())), preferred_element_type=F32)
            scores = jnp.where(causal, scores, 0.0).astype(BF16)
            st = st_ref[h]
            q_in = (qh * jnp.exp(b)).astype(BF16)
            o = _dot(scores, vh) + lax.dot_general(q_in, st.astype(BF16), (((1,), (1,)), ((), ())),
                                                   preferred_element_type=F32)
            k_st = (kh * jnp.exp(b_end - b)).astype(BF16)
            st_ref[h] = st * jnp.exp(b_end) + lax.dot_general(
                vh, k_st, (((0,), (0,)), ((), ())), preferred_element_type=F32)
            o = o * lax.rsqrt(jnp.mean(o * o, axis=-1, keepdims=True) + GLA_NORM_EPS) * ng_ref[...]
            og = og_ref[0, rows, vc].astype(F32)
            y_ref[0, rows, vc] = (o * (og * _sigmoid(og))).astype(BF16)


def _gla(q, k, v, og, la, norm_g, tc):
    B, S, qk = q.shape
    vw = v.shape[-1]
    tok = lambda width: pl.BlockSpec((1, tc, width), lambda b, i: (b, i, 0))
    return pl.pallas_call(
        _gla_kernel,
        grid=(B, S // tc),
        in_specs=[tok(qk), tok(qk), tok(vw), tok(vw), tok(qk),
                  pl.BlockSpec(norm_g.shape, lambda b, i: (0, 0))],
        out_specs=tok(vw),
        out_shape=jax.ShapeDtypeStruct((B, S, vw), BF16),
        scratch_shapes=[pltpu.VMEM((GLA_HEADS, GLA_DV, GLA_DK), F32)],
        compiler_params=_cparams("parallel", "arbitrary"),
        name="gla_chunk",
    )(q, k, v, og, la, norm_g)


def _rwkv_proj_kernel(x_ref, g_ref, w_ref, mu_ref, w0_ref, w2_ref, a0_ref, a2_ref, g2_ref,
                      r_ref, w_out_ref, k_ref, v_ref, a_ref, gate_ref, carry_ref):
    W = r_ref.shape[-1]
    tm = x_ref.shape[1]

    @pl.when(pl.program_id(1) == 0)
    def _():
        carry_ref[...] = jnp.zeros_like(carry_ref)

    h = _rms(x_ref[0], g_ref[...]).astype(BF16)
    z = _dot(h, w_ref[...])
    rolled = pltpu.roll(z, 1, axis=0)
    first = lax.broadcasted_iota(jnp.int32, z.shape, 0) == 0
    prev = jnp.where(first, carry_ref[0:1, :], rolled)
    carry_ref[0:1, :] = z[tm - 1:tm, :]
    u = z + mu_ref[...] * (prev - z)

    r = u[:, :W]
    k = u[:, W:2 * W]
    v = u[:, 2 * W:3 * W]
    lora = u[:, 3 * W:3 * W + LANES]
    xg = u[:, 3 * W + LANES:]
    w_log = -_softplus(-(w0_ref[...] + _dot(jnp.tanh(lora).astype(BF16), w2_ref[...]))) - 0.5
    decay = jnp.exp(-jnp.exp(w_log))
    r_ref[0] = r
    w_out_ref[0] = decay
    k_ref[0] = k
    v_ref[0] = v
    a_ref[0] = _sigmoid(a0_ref[...] + _dot(lora.astype(BF16), a2_ref[...]))
    gate_ref[0] = _dot(_sigmoid(xg).astype(BF16), g2_ref[...]).astype(BF16)


def _rwkv_proj(x, gain, w, mu, w0, w2p, a0, a2p, g2, tm):
    B, S, D = x.shape
    W = w0.shape[-1]
    tok = lambda width: pl.BlockSpec((1, tm, width), lambda b, i: (b, i, 0))
    full = lambda a: pl.BlockSpec(a.shape, lambda b, i: (0,) * a.ndim)
    consts = (gain, w, mu, w0, w2p, a0, a2p, g2)
    return pl.pallas_call(
        _rwkv_proj_kernel,
        grid=(B, S // tm),
        in_specs=[tok(D)] + [full(c) for c in consts],
        out_specs=[tok(W)] * 6,
        out_shape=[jax.ShapeDtypeStruct((B, S, W), F32)] * 5 + [jax.ShapeDtypeStruct((B, S, W), BF16)],
        scratch_shapes=[pltpu.VMEM((8, w.shape[1]), F32)],
        compiler_params=_cparams("parallel", "arbitrary"),
        name="rwkv_proj",
    )(x, *consts)


def _rwkv_scan_kernel(r_ref, w_ref, k_ref, v_ref, a_ref, kk_ref, ka_ref, rk_ref, lnw_ref, lnb_ref,
                      y_ref, st_ref, sa_ref, av_ref, bv_ref, km_ref):
    tt, G, N, _ = r_ref.shape
    half = N // 2

    @pl.when(pl.program_id(0) == 0)
    def _():
        st_ref[...] = jnp.zeros_like(st_ref)

    def group_step(tg, carry):
        t = tg // G
        g = tg % G
        k_raw = k_ref[t, g]
        iclr = a_ref[t, g]
        kk = k_raw * kk_ref[...]
        kk = kk / jnp.maximum(jnp.sqrt(jnp.sum(kk * kk, axis=0, keepdims=True)), 1e-12)
        k_mod = k_raw * (1.0 + (iclr - 1.0) * ka_ref[...])
        av_ref[...] = -kk
        bv_ref[...] = kk * iclr
        km_ref[...] = k_mod

        acc = [jnp.zeros((N, LANES), F32)] * 2
        for j in range(N):
            acc[j % 2] = acc[j % 2] + st_ref[g, j] * av_ref[j:j + 1, :]
        sa_ref[...] = acc[0] + acc[1]

        def half_step(ih, c):
            rows = pl.ds(pl.multiple_of(ih * half, half), half)
            sa_h = sa_ref[rows, :]
            vt = v_ref[t, g, rows, :]
            y = jnp.zeros((half, LANES), F32)
            for j in range(N):
                row = slice(j, j + 1)
                s_new = (st_ref[g, j, rows, :] * w_ref[t, g, row, :] + sa_h * bv_ref[row, :]
                         + vt * km_ref[row, :])
                st_ref[g, j, rows, :] = s_new
                y = y + s_new * r_ref[t, g, row, :]
            y_ref[t, g, rows, :] = y
            return c

        lax.fori_loop(0, 2, half_step, 0)

        y = y_ref[t, g]
        mean = jnp.mean(y, axis=0, keepdims=True)
        d = y - mean
        var = jnp.mean(d * d, axis=0, keepdims=True)
        bonus = jnp.sum(r_ref[t, g] * k_mod * rk_ref[...], axis=0, keepdims=True)
        y_ref[t, g] = d * lax.rsqrt(var + RWKV_GN_EPS) * lnw_ref[...] + lnb_ref[...] + bonus * v_ref[t, g]
        return carry

    lax.fori_loop(0, tt * G, group_step, 0)


def _rwkv_scan(r, w, k, v, a, k_k, k_a, r_k, ln_w, ln_b, tt):
    S, G, N, _ = r.shape
    blk = pl.BlockSpec((tt, G, N, LANES), lambda i: (i, 0, 0, 0))
    par = pl.BlockSpec((N, LANES), lambda i: (0, 0))
    tile = pltpu.VMEM((N, LANES), F32)
    return pl.pallas_call(
        _rwkv_scan_kernel,
        grid=(S // tt,),
        in_specs=[blk] * 5 + [par] * 5,
        out_specs=blk,
        out_shape=jax.ShapeDtypeStruct((S, G, N, LANES), F32),
        scratch_shapes=[pltpu.VMEM((G, N, N, LANES), F32), tile, tile, tile, tile],
        compiler_params=_cparams("arbitrary"),
        name="rwkv_scan",
    )(r, w, k, v, a, k_k, k_a, r_k, ln_w, ln_b)


def _merge_kernel(x_ref, ya_ref, ys_ref, gate_ref, g_mix_ref, wg_ref, wa_ref, wb_ref, wo_ref, g_ffn_ref,
                  rw_ref, rb_ref,
                  x1_ref, h2_ref, e_ref, rank_ref, p_ref, cnt_ref, carry_ref, *, n_experts):
    tm, D = x_ref.shape

    @pl.when(pl.program_id(0) == 0)
    def _():
        carry_ref[...] = jnp.zeros_like(carry_ref)

    x = x_ref[...]
    h = _rms(x, g_mix_ref[...]).astype(BF16)
    gates = _sigmoid(_dot(h, wg_ref[...]))

    yb = (ys_ref[...] * gate_ref[...].astype(F32)).astype(BF16)

    merged = gates[:, :D] * _dot(ya_ref[...], wa_ref[...]) + gates[:, D:] * _dot(yb, wb_ref[...])
    x1 = x + _dot(merged.astype(BF16), wo_ref[...])
    x1_ref[...] = x1
    h2 = _rms(x1, g_ffn_ref[...])
    _store_packed(h2_ref, h2)

    lane = lax.broadcasted_iota(jnp.int32, (tm, LANES), 1)
    logits = jnp.where(lane < n_experts, _dot_hi(h2, rw_ref[...]) + rb_ref[...], -jnp.inf)
    vals, idxs, hots = [], [], []
    for _ in range(TOP_K):
        m = jnp.max(logits, axis=-1, keepdims=True)
        idx = jnp.min(jnp.where(logits == m, lane, LANES), axis=-1, keepdims=True)
        hot = lane == idx
        vals.append(m)
        idxs.append(idx)
        hots.append(hot)
        logits = jnp.where(hot, -jnp.inf, logits)
    exps = [jnp.exp(val - vals[0]) for val in vals]
    denom = functools.reduce(lambda s, e: s + e, exps)
    chosen = functools.reduce(lambda s, e: s + e, [hot.astype(F32) for hot in hots])
    r_i = lax.broadcasted_iota(jnp.int32, (tm, tm), 0)
    c_i = lax.broadcasted_iota(jnp.int32, (tm, tm), 1)
    before = _dot((r_i > c_i).astype(BF16), chosen.astype(BF16)) + carry_ref[0:1, :]
    e_out = jnp.zeros((tm, LANES), jnp.int32)
    rank_out = jnp.zeros((tm, LANES), jnp.int32)
    p_out = jnp.zeros((tm, LANES), F32)
    for s in range(TOP_K):
        rank = jnp.sum(jnp.where(hots[s], before, 0.0), axis=-1, keepdims=True)
        e_out = jnp.where(lane == s, idxs[s], e_out)
        rank_out = jnp.where(lane == s, rank.astype(jnp.int32), rank_out)
        p_out = jnp.where(lane == s, exps[s] / denom, p_out)
    e_ref[...] = e_out
    rank_ref[...] = rank_out
    p_ref[...] = p_out
    total = carry_ref[0:1, :] + jnp.sum(chosen, axis=0, keepdims=True)
    carry_ref[0:1, :] = total
    cnt_ref[...] = jnp.broadcast_to(total, cnt_ref.shape).astype(jnp.int32)


def _merge(x, ya, ys, gate, consts, n_experts, tm):
    T, D = x.shape
    W = ys.shape[-1]
    tok = lambda width: pl.BlockSpec((tm, width), lambda i: (i, 0))
    full = lambda a: pl.BlockSpec(a.shape, lambda i: (0,) * a.ndim)
    return pl.pallas_call(
        functools.partial(_merge_kernel, n_experts=n_experts),
        grid=(T // tm,),
        in_specs=[tok(D), tok(W), tok(W), tok(W)] + [full(c) for c in consts],
        out_specs=[tok(D), pl.BlockSpec((tm, D // 2 // LANES, LANES), lambda i: (i, 0, 0)),
                   tok(LANES), tok(LANES), tok(LANES), pl.BlockSpec((8, LANES), lambda i: (0, 0))],
        out_shape=[jax.ShapeDtypeStruct((T, D), F32), jax.ShapeDtypeStruct((T, D // 2 // LANES, LANES), U32),
                   jax.ShapeDtypeStruct((T, LANES), jnp.int32), jax.ShapeDtypeStruct((T, LANES), jnp.int32),
                   jax.ShapeDtypeStruct((T, LANES), F32), jax.ShapeDtypeStruct((8, LANES), jnp.int32)],
        scratch_shapes=[pltpu.VMEM((8, LANES), F32)],
        compiler_params=_cparams("arbitrary"),
        name="merge_router",
    )(x, ya, ys, gate, *consts)


def _idx_copy(dest_ref, idx_ref, sem, tile, n_idx):
    slot = tile % 2
    return pltpu.make_async_copy(dest_ref.at[pl.ds(pl.multiple_of(tile * n_idx, n_idx), n_idx)],
                                 idx_ref.at[pl.ds(pl.multiple_of(slot * n_idx, n_idx), n_idx)], sem.at[slot])


def _dispatch_kernel(dest_ref, rows_ref, init_ref, xs_ref, idx_ref, idx_sem, row_sem, *, tm):
    del init_ref
    i = pl.program_id(0)
    n = pl.num_programs(0)
    n_idx = tm * TOP_K
    idx_copy = functools.partial(_idx_copy, dest_ref, idx_ref, idx_sem, n_idx=n_idx)

    @pl.when(i == 0)
    def _():
        idx_copy(i).start()

    idx_copy(i).wait()

    @pl.when(i + 1 < n)
    def _():
        idx_copy(i + 1).start()

    base = (i % 2) * n_idx

    def issue(g, c):
        r0 = g * ISSUE_GROUP
        dst = [idx_ref[base + r0 * TOP_K + k] for k in range(ISSUE_GROUP * TOP_K)]
        for k, d in enumerate(dst):
            pltpu.make_async_copy(rows_ref.at[r0 + k // TOP_K], xs_ref.at[d], row_sem).start(priority=k % 2)
        return c

    lax.fori_loop(0, tm // ISSUE_GROUP, issue, 0)

    for _ in range(TOP_K):
        pltpu.make_async_copy(rows_ref, xs_ref.at[pl.ds(0, tm)], row_sem).wait()


def _dispatch(dest, rows, n_rows, tm):
    T = rows.shape[0]
    packed = (n_rows,) + rows.shape[1:]
    any_spec = pl.BlockSpec(memory_space=pl.ANY)
    return pl.pallas_call(
        functools.partial(_dispatch_kernel, tm=tm),
        grid=(T // tm,),
        in_specs=[any_spec, pl.BlockSpec((tm,) + rows.shape[1:], lambda i: (i, 0, 0)), any_spec],
        out_specs=any_spec,
        out_shape=jax.ShapeDtypeStruct(packed, U32),
        input_output_aliases={2: 0},
        scratch_shapes=[pltpu.SMEM((2 * tm * TOP_K,), jnp.int32), pltpu.SemaphoreType.DMA((2,)),
                        pltpu.SemaphoreType.DMA],
        compiler_params=_cparams("arbitrary"),
        name="moe_dispatch",
    )(dest, rows, jnp.zeros(packed, U32))


def _split_w1_kernel(w_ref, glu_ref, lin_ref):
    cols = w_ref.shape[-1]
    src = lax.broadcasted_iota(jnp.int32, (cols, cols // 2), 0)
    dst = lax.broadcasted_iota(jnp.int32, (cols, cols // 2), 1)
    w = w_ref[0].astype(BF16)
    glu_ref[0] = _dot(w, (src == 2 * dst).astype(BF16)).astype(BF16)
    lin_ref[0] = _dot(w, (src == 2 * dst + 1).astype(BF16)).astype(BF16)


def _split_w1(w1, cols=512):
    E, D, FF2 = w1.shape
    out = jax.ShapeDtypeStruct((E, D, FF2 // 2), BF16)
    return pl.pallas_call(
        _split_w1_kernel,
        grid=(E, FF2 // cols),
        in_specs=[pl.BlockSpec((1, D, cols), lambda e, c: (e, 0, c))],
        out_specs=[pl.BlockSpec((1, D, cols // 2), lambda e, c: (e, 0, c))] * 2,
        out_shape=[out, out],
        compiler_params=_cparams("parallel", "parallel"),
        name="split_w1",
    )(w1)


def _expert_kernel(be_ref, nu_ref, x_ref, w1g_ref, w1l_ref, b1g_ref, b1l_ref, w2_ref, b2_ref, o_ref):
    del be_ref
    used = pl.program_id(0) < nu_ref[0]

    @pl.when(used)
    def _():
        lo, hi = _load_packed(x_ref)
        x = jnp.concatenate([lo.astype(BF16), hi.astype(BF16)], axis=1)
        glu = jnp.minimum(_dot(x, w1g_ref[0]) + b1g_ref[0], SWIGLU_LIMIT)
        lin = jnp.clip(_dot(x, w1l_ref[0]) + b1l_ref[0], -SWIGLU_LIMIT, SWIGLU_LIMIT)
        act = glu * _sigmoid(SWIGLU_ALPHA * glu) * (lin + 1.0)
        _store_packed(o_ref, _dot(act.astype(BF16), w2_ref[0]) + b2_ref[0])

    @pl.when(jnp.logical_not(used))
    def _():
        o_ref[...] = jnp.zeros_like(o_ref)


def _experts(block_expert, n_used, xs, w1g, w1l, b1g, b1l, w2, b2):
    R = xs.shape[0]
    D, FF = w1g.shape[1:]
    nb = R // MOE_ROWS
    per_e = lambda shape: pl.BlockSpec((1,) + shape, lambda i, be, nu: (be[i], 0, 0))
    rows = pl.BlockSpec((MOE_ROWS,) + xs.shape[1:], lambda i, be, nu: (i, 0, 0))
    return pl.pallas_call(
        _expert_kernel,
        grid_spec=pltpu.PrefetchScalarGridSpec(
            num_scalar_prefetch=2,
            grid=(nb,),
            in_specs=[rows, per_e((D, FF)), per_e((D, FF)), per_e((1, FF)), per_e((1, FF)),
                      per_e((FF, D)), per_e((1, D))],
            out_specs=rows,
        ),
        out_shape=jax.ShapeDtypeStruct(xs.shape, U32),
        compiler_params=_cparams("arbitrary"),
        name="expert_ffn",
    )(block_expert, n_used, xs, w1g, w1l, b1g, b1l, w2, b2)


def _combine_kernel(dest_ref, rows_ref, x1_ref, p_ref, g_ref, o_ref, buf_ref, idx_ref, idx_sem, row_sem):
    tm, D = x1_ref.shape
    i = pl.program_id(0)
    n = pl.num_programs(0)
    n_idx = tm * TOP_K
    idx_copy = functools.partial(_idx_copy, dest_ref, idx_ref, idx_sem, n_idx=n_idx)

    def gather(tile):
        slot = tile % 2
        base = slot * n_idx

        def issue(g, c):
            r0 = g * ISSUE_GROUP
            src = [idx_ref[base + r0 * TOP_K + k] for k in range(ISSUE_GROUP * TOP_K)]
            for k, d in enumerate(src):
                pltpu.make_async_copy(rows_ref.at[d], buf_ref.at[slot, k % TOP_K, r0 + k // TOP_K],
                                      row_sem.at[slot]).start(priority=k % 2)
            return c

        lax.fori_loop(0, tm // ISSUE_GROUP, issue, 0)

    @pl.when(i == 0)
    def _():
        idx_copy(i).start()
        idx_copy(i).wait()
        gather(i)

        @pl.when(n > 1)
        def _():
            idx_copy(i + 1).start()

    @pl.when(i + 1 < n)
    def _():
        idx_copy(i + 1).wait()
        gather(i + 1)

    @pl.when(i + 2 < n)
    def _():
        idx_copy(i + 2).start()

    slot = i % 2
    for s in range(TOP_K):
        pltpu.make_async_copy(rows_ref.at[pl.ds(0, tm)], buf_ref.at[slot, s], row_sem.at[slot]).wait()

    half = D // 2
    x1 = x1_ref[...]
    acc_lo, acc_hi = x1[:, :half], x1[:, half:]
    p = p_ref[...]
    for s in range(TOP_K):
        lo, hi = _load_packed(buf_ref.at[slot, s])
        acc_lo = acc_lo + p[:, s:s + 1] * lo
        acc_hi = acc_hi + p[:, s:s + 1] * hi
    ms = (jnp.sum(acc_lo * acc_lo, axis=-1, keepdims=True) + jnp.sum(acc_hi * acc_hi, axis=-1, keepdims=True)) / D
    scale = lax.rsqrt(ms + RMS_EPS)
    o_ref[:, :half] = acc_lo * scale * g_ref[:, :half]
    o_ref[:, half:] = acc_hi * scale * g_ref[:, half:]


def _combine(dest, rows, x1, p, gain, tm):
    T, D = x1.shape
    any_spec = pl.BlockSpec(memory_space=pl.ANY)
    return pl.pallas_call(
        _combine_kernel,
        grid=(T // tm,),
        in_specs=[any_spec, any_spec, pl.BlockSpec((tm, D), lambda i: (i, 0)),
                  pl.BlockSpec((tm, LANES), lambda i: (i, 0)), pl.BlockSpec(gain.shape, lambda i: (0, 0))],
        out_specs=pl.BlockSpec((tm, D), lambda i: (i, 0)),
        out_shape=jax.ShapeDtypeStruct((T, D), F32),
        scratch_shapes=[pltpu.VMEM((2, TOP_K, tm) + rows.shape[1:], U32), pltpu.SMEM((2 * tm * TOP_K,), jnp.int32),
                        pltpu.SemaphoreType.DMA((2,)), pltpu.SemaphoreType.DMA((2,))],
        compiler_params=_cparams("arbitrary"),
        name="moe_combine_norm",
    )(dest, rows, x1, p, gain)


def _pick_tile(n, pref):
    t = min(n, pref)
    assert n % t == 0, (n, t)
    return t


def kernel(x, norm_mix_g, w_in, gla_gate_up, gla_gate_bias, gla_norm_g, rwkv_mu, rwkv_w0, rwkv_w2, rwkv_a0, rwkv_a2, rwkv_g2, rwkv_k_k, rwkv_k_a, rwkv_r_k, rwkv_ln_w, rwkv_ln_b, w_branch_a, w_branch_b, w_out, norm_ffn_g, router_w, router_b, expert_w1, expert_b1, expert_w2, expert_b2, norm_final_g):
    assert norm_mix_g.shape[0] == 1, "single-layer block"
    B, S, D = x.shape
    T = B * S
    qk = gla_gate_bias.shape[-1]
    vw = GLA_HEADS * GLA_DV
    W = rwkv_w0.shape[-1]
    H = W // RWKV_HEAD_DIM
    E = router_w.shape[-1]
    assert qk == GLA_HEADS * GLA_DK and B * H % LANES == 0 and E <= LANES
    row = lambda a: a.reshape(1, -1)

    wi = w_in[0]
    gla_cols = 2 * qk + vw + GLA_GATE_RANK + vw
    rwkv_cols = rwkv_mu.shape[-1]
    o_q, o_k, o_v, o_lr, o_og = 0, qk, 2 * qk, 2 * qk + vw, 2 * qk + vw + GLA_GATE_RANK
    w_gla = jnp.concatenate(
        [wi[:, o_q:o_lr], wi[:, o_og:gla_cols], wi[:, o_lr:o_og],
         jnp.zeros((D, LANES - GLA_GATE_RANK), F32)], axis=1).astype(BF16)
    up = jnp.concatenate([gla_gate_up[0], jnp.zeros((LANES - GLA_GATE_RANK, qk), F32)], axis=0).astype(BF16)
    chan = jnp.arange(W)
    perm = (chan % H) * RWKV_HEAD_DIM + chan // H
    hm = lambda a: a[..., perm]
    hm3 = lambda a: jnp.concatenate([hm(a[..., :W]), hm(a[..., W:2 * W]), hm(a[..., 2 * W:3 * W]), a[..., 3 * W:]],
                                    axis=-1)
    w_rwkv = hm3(wi[:, gla_cols:gla_cols + rwkv_cols]).astype(BF16)
    w_gate = wi[:, gla_cols + rwkv_cols:].astype(BF16)
    n_w, n_a = rwkv_w2.shape[1], rwkv_a2.shape[1]
    assert n_w + n_a == LANES
    w2p = hm(jnp.concatenate([rwkv_w2[0], jnp.zeros((n_a, W), F32)], axis=0)).astype(BF16)
    a2p = hm(jnp.concatenate([jnp.zeros((n_w, W), F32), rwkv_a2[0]], axis=0)).astype(BF16)

    tm_gla = _pick_tile(S, 512)
    q, k, v, og, la = _gla_proj(x, row(norm_mix_g), w_gla, up, row(gla_gate_bias), tm_gla)
    ya = _gla(q, k, v, og, la, row(gla_norm_g), _pick_tile(S, 256))

    tm_r = _pick_tile(S, 256)
    r_, w_, k_, v_, a_, gate_ = _rwkv_proj(x, row(norm_mix_g), w_rwkv, hm3(row(rwkv_mu)), hm(row(rwkv_w0)), w2p,
                                           hm(row(rwkv_a0)), a2p, hm(rwkv_g2[0]).astype(BF16), tm_r)
    G = B * H // LANES
    bg = B // G
    to_scan = lambda t: t.reshape(G, bg, S, RWKV_HEAD_DIM, H).transpose(2, 0, 3, 1, 4).reshape(
        S, G, RWKV_HEAD_DIM, LANES)
    par = lambda p: jnp.tile(p.reshape(H, RWKV_HEAD_DIM).T, (1, bg))
    ys = _rwkv_scan(*(to_scan(t) for t in (r_, w_, k_, v_, a_)), par(rwkv_k_k), par(rwkv_k_a), par(rwkv_r_k),
                    par(rwkv_ln_w), par(rwkv_ln_b), _pick_tile(S, 16))
    ys = ys.reshape(S, G, RWKV_HEAD_DIM, bg, H).transpose(1, 3, 0, 2, 4).reshape(T, W)

    rw = jnp.concatenate([router_w[0], jnp.zeros((D, LANES - E), F32)], axis=1)
    rb = jnp.concatenate([router_b[0], jnp.zeros((LANES - E,), F32)]).reshape(1, LANES)
    consts = (row(norm_mix_g), w_gate, w_branch_a[0].astype(BF16), w_branch_b[0][perm, :].astype(BF16),
              w_out[0].astype(BF16), row(norm_ffn_g), rw, rb)
    tm_m = _pick_tile(T, 256)
    x1, h2, e_sel, rank, p_sel, counts = _merge(
        x.reshape(T, D), ya.reshape(T, vw), ys, gate_.reshape(T, W), consts, E, tm_m)

    counts = counts[0, :E]
    padded = (counts + MOE_ROWS - 1) // MOE_ROWS * MOE_ROWS
    pad_ends = jnp.cumsum(padded)
    pad_starts = pad_ends - padded
    nb = -(-T * TOP_K // MOE_ROWS) + E
    dest = pad_starts[e_sel[:, :TOP_K]] + rank[:, :TOP_K]
    block_start = jnp.arange(nb, dtype=jnp.int32) * MOE_ROWS
    block_expert = jnp.minimum(jnp.sum(pad_ends[None, :] <= block_start[:, None], axis=1), E - 1).astype(jnp.int32)
    n_used = (pad_ends[-1:] // MOE_ROWS).astype(jnp.int32)
    dest = dest.reshape(-1).astype(jnp.int32)
    tm_d = _pick_tile(T, 256)
    xs = _dispatch(dest, h2, nb * MOE_ROWS, _pick_tile(T, 512))

    FF = expert_w2.shape[2]
    w1g, w1l = _split_w1(expert_w1[0])
    b1 = expert_b1[0].reshape(E, 1, FF, 2)
    outs_e = _experts(block_expert, n_used, xs, w1g, w1l, b1[..., 0], b1[..., 1],
                      expert_w2[0].astype(BF16), expert_b2[0].reshape(E, 1, D))

    out = _combine(dest, outs_e, x1, p_sel, row(norm_final_g), tm_d)
    return out.reshape(B, S, D)
```

```python
import functools

import jax
import jax.numpy as jnp
from jax import lax
from jax.experimental import pallas as pl
from jax.experimental.pallas import tpu as pltpu

F32 = jnp.float32
BF16 = jnp.bfloat16
U32 = jnp.uint32
HIGHEST = lax.Precision.HIGHEST

LANES = 128
VMEM_LIMIT = 56 * 1024 * 1024

RMS_EPS = 1e-6
GLA_HEADS = 4
GLA_DK = 128
GLA_DV = 256
GLA_GATE_RANK = 16
GLA_TAU = 16.0
GLA_CHUNK = 64
GLA_NORM_EPS = 1e-5
RWKV_HEAD_DIM = 64
RWKV_GN_EPS = 64e-5
TOP_K = 4
SWIGLU_ALPHA = 1.702
SWIGLU_LIMIT = 7.0
MOE_ROWS = 512
ISSUE_GROUP = 4
MERGE_PARTS = 2


def _cparams(*sem):
    return pltpu.CompilerParams(dimension_semantics=sem, vmem_limit_bytes=VMEM_LIMIT)


def _rms(xf, gain):
    return xf * lax.rsqrt(jnp.mean(xf * xf, axis=-1, keepdims=True) + RMS_EPS) * gain


def _softplus(y):
    return jnp.maximum(y, 0.0) + jnp.log1p(jnp.exp(-jnp.abs(y)))


def _sigmoid(y):
    return 1.0 / (1.0 + jnp.exp(-y))


def _dot(a, b):
    return jnp.dot(a, b, preferred_element_type=F32)


def _dot_hi(a, b):
    return jnp.dot(a, b, preferred_element_type=F32, precision=HIGHEST)


def _pack_rows(x):
    n = x.shape[-1] // 2
    bits = lambda t: lax.bitcast_convert_type(t.astype(BF16).astype(F32), U32)
    return (bits(x[:, :n]) >> 16) | bits(x[:, n:])


def _unpack_rows(w):
    lo = lax.bitcast_convert_type(w << 16, F32)
    hi = lax.bitcast_convert_type(w & jnp.uint32(0xFFFF0000), F32)
    return lo, hi


def _store_packed(ref, x):
    packed = _pack_rows(x)
    for c in range(ref.shape[1]):
        ref[:, c, :] = packed[:, c * LANES:(c + 1) * LANES]


def _load_packed(ref):
    return _unpack_rows(jnp.concatenate([ref[:, c, :] for c in range(ref.shape[1])], axis=1))


def _gla_proj_kernel(x_ref, g_ref, w_ref, up_ref, bias_ref, q_ref, k_ref, v_ref, og_ref, la_ref):
    qk = q_ref.shape[-1]
    vw = v_ref.shape[-1]
    h = _rms(x_ref[0], g_ref[...]).astype(BF16)
    z = _dot(h, w_ref[...])
    q_ref[0] = z[:, :qk].astype(BF16)
    k_ref[0] = z[:, qk:2 * qk].astype(BF16)
    v_ref[0] = z[:, 2 * qk:2 * qk + vw].astype(BF16)
    og_ref[0] = z[:, 2 * qk + vw:2 * qk + 2 * vw].astype(BF16)
    lr = z[:, 2 * qk + 2 * vw:].astype(BF16)
    pre = _dot(lr, up_ref[...]) + bias_ref[...]
    la_ref[0] = -_softplus(-pre) * (1.0 / GLA_TAU)


def _gla_proj(x, gain, w, up, bias, tm):
    B, S, D = x.shape
    qk = bias.shape[-1]
    vw = (w.shape[1] - 2 * qk - LANES) // 2
    tok = lambda width: pl.BlockSpec((1, tm, width), lambda b, i: (b, i, 0))
    full = lambda a: pl.BlockSpec(a.shape, lambda b, i: (0,) * a.ndim)
    return pl.pallas_call(
        _gla_proj_kernel,
        grid=(B, S // tm),
        in_specs=[tok(D), full(gain), full(w), full(up), full(bias)],
        out_specs=[tok(qk), tok(qk), tok(vw), tok(vw), tok(qk)],
        out_shape=[jax.ShapeDtypeStruct((B, S, qk), BF16), jax.ShapeDtypeStruct((B, S, qk), BF16),
                   jax.ShapeDtypeStruct((B, S, vw), BF16), jax.ShapeDtypeStruct((B, S, vw), BF16),
                   jax.ShapeDtypeStruct((B, S, qk), F32)],
        compiler_params=_cparams("parallel", "parallel"),
        name="gla_proj",
    )(x, gain, w, up, bias)


def _gla_kernel(q_ref, k_ref, v_ref, og_ref, la_ref, ng_ref, y_ref, st_ref, qa_ref, ka_ref, qi_ref, ks_ref, dec_ref,
                sc_ref):
    C = GLA_CHUNK
    tc = q_ref.shape[1]

    @pl.when(pl.program_id(1) == 0)
    def _():
        st_ref[...] = jnp.zeros_like(st_ref)

    row = lax.broadcasted_iota(jnp.int32, (tc, tc), 0)
    col = lax.broadcasted_iota(jnp.int32, (tc, tc), 1)
    same_chunk_before = jnp.logical_and(row >= col, row // C == col // C)
    b_all = _dot_hi(same_chunk_before.astype(F32), la_ref[0])
    scale = GLA_DK ** -0.5
    for c in range(tc // C):
        rows = slice(c * C, (c + 1) * C)
        b = b_all[rows]
        b_mid = b[C // 2:C // 2 + 1]
        b_end = b[C - 1:C]
        q = q_ref[0, rows, :].astype(F32) * scale
        k = k_ref[0, rows, :].astype(F32)
        qa_ref[rows, :] = (q * jnp.exp(b - b_mid)).astype(BF16)
        ka_ref[rows, :] = (k * jnp.exp(b_mid - b)).astype(BF16)
        qi_ref[rows, :] = (q * jnp.exp(b)).astype(BF16)
        ks_ref[rows, :] = (k * jnp.exp(b_end - b)).astype(BF16)
        dec_ref[c:c + 1, :] = jnp.exp(b_end)

    r_c = lax.broadcasted_iota(jnp.int32, (C, C), 0)
    c_c = lax.broadcasted_iota(jnp.int32, (C, C), 1)
    causal = r_c >= c_c
    for c in range(tc // C):
        rows = slice(c * C, (c + 1) * C)
        for h in range(GLA_HEADS):
            kc = slice(h * GLA_DK, (h + 1) * GLA_DK)
            scores = lax.dot_general(qa_ref[rows, kc], ka_ref[rows, kc], (((1,), (1,)), ((), ())),
                                     preferred_element_type=F32)
            sc_ref[rows, h * C:(h + 1) * C] = jnp.where(causal, scores, 0.0).astype(BF16)

    for c in range(tc // C):
        rows = slice(c * C, (c + 1) * C)
        for h in range(GLA_HEADS):
            kc = slice(h * GLA_DK, (h + 1) * GLA_DK)
            vc = slice(h * GLA_DV, (h + 1) * GLA_DV)
            vh = v_ref[0, rows, vc]
            st = st_ref[h]
            o = _dot(sc_ref[rows, h * C:(h + 1) * C], vh) + lax.dot_general(
                qi_ref[rows, kc], st.astype(BF16), (((1,), (1,)), ((), ())), preferred_element_type=F32)
            st_ref[h] = st * dec_ref[c:c + 1, kc] + lax.dot_general(
                vh, ks_ref[rows, kc], (((0,), (0,)), ((), ())), preferred_element_type=F32)
            o = o * lax.rsqrt(jnp.mean(o * o, axis=-1, keepdims=True) + GLA_NORM_EPS) * ng_ref[...]
            og = og_ref[0, rows, vc].astype(F32)
            y_ref[0, rows, vc] = (o * (og * _sigmoid(og))).astype(BF16)


def _gla(q, k, v, og, la, norm_g, tc):
    B, S, qk = q.shape
    vw = v.shape[-1]
    tok = lambda width: pl.BlockSpec((1, tc, width), lambda b, i: (b, i, 0))
    return pl.pallas_call(
        _gla_kernel,
        grid=(B, S // tc),
        in_specs=[tok(qk), tok(qk), tok(vw), tok(vw), tok(qk),
                  pl.BlockSpec(norm_g.shape, lambda b, i: (0, 0))],
        out_specs=tok(vw),
        out_shape=jax.ShapeDtypeStruct((B, S, vw), BF16),
        scratch_shapes=[pltpu.VMEM((GLA_HEADS, GLA_DV, GLA_DK), F32)] + [pltpu.VMEM((tc, qk), BF16)] * 4
        + [pltpu.VMEM((8, qk), F32), pltpu.VMEM((tc, GLA_HEADS * GLA_CHUNK), BF16)],
        compiler_params=_cparams("parallel", "arbitrary"),
        name="gla_chunk",
    )(q, k, v, og, la, norm_g)


def _rwkv_proj_kernel(x_ref, g_ref, w_ref, mu_ref, w0_ref, w2_ref, a0_ref, a2_ref, g2_ref,
                      r_ref, w_out_ref, k_ref, v_ref, a_ref, gate_ref, carry_ref):
    W = r_ref.shape[-1]
    tm = x_ref.shape[1]

    @pl.when(pl.program_id(1) == 0)
    def _():
        carry_ref[...] = jnp.zeros_like(carry_ref)

    h = _rms(x_ref[0], g_ref[...]).astype(BF16)
    z = _dot(h, w_ref[...])
    rolled = pltpu.roll(z, 1, axis=0)
    first = lax.broadcasted_iota(jnp.int32, z.shape, 0) == 0
    prev = jnp.where(first, carry_ref[0:1, :], rolled)
    carry_ref[0:1, :] = z[tm - 1:tm, :]
    u = z + mu_ref[...] * (prev - z)

    r = u[:, :W]
    k = u[:, W:2 * W]
    v = u[:, 2 * W:3 * W]
    lora = u[:, 3 * W:3 * W + LANES]
    xg = u[:, 3 * W + LANES:]
    w_log = -_softplus(-(w0_ref[...] + _dot(jnp.tanh(lora).astype(BF16), w2_ref[...]))) - 0.5
    decay = jnp.exp(-jnp.exp(w_log))
    r_ref[0] = r
    w_out_ref[0] = decay
    k_ref[0] = k
    v_ref[0] = v
    a_ref[0] = _sigmoid(a0_ref[...] + _dot(lora.astype(BF16), a2_ref[...]))
    gate_ref[0] = _dot(_sigmoid(xg).astype(BF16), g2_ref[...]).astype(BF16)


def _rwkv_proj(x, gain, w, mu, w0, w2p, a0, a2p, g2, tm):
    B, S, D = x.shape
    W = w0.shape[-1]
    tok = lambda width: pl.BlockSpec((1, tm, width), lambda b, i: (b, i, 0))
    full = lambda a: pl.BlockSpec(a.shape, lambda b, i: (0,) * a.ndim)
    consts = (gain, w, mu, w0, w2p, a0, a2p, g2)
    return pl.pallas_call(
        _rwkv_proj_kernel,
        grid=(B, S // tm),
        in_specs=[tok(D)] + [full(c) for c in consts],
        out_specs=[tok(W)] * 6,
        out_shape=[jax.ShapeDtypeStruct((B, S, W), F32)] * 5 + [jax.ShapeDtypeStruct((B, S, W), BF16)],
        scratch_shapes=[pltpu.VMEM((8, w.shape[1]), F32)],
        compiler_params=_cparams("parallel", "arbitrary"),
        name="rwkv_proj",
    )(x, *consts)


def _rwkv_scan_kernel(r_ref, w_ref, k_ref, v_ref, a_ref, kn_ref, an_ref, kk_ref, ka_ref, rk_ref, lnw_ref, lnb_ref,
                      y_ref, st_ref, sa_ref, av_ref, bv_ref, km_ref):
    tt, G, N, _ = r_ref.shape
    half = N // 2

    def step_vectors(k_raw, iclr):
        kk = k_raw * kk_ref[...]
        kk = kk / jnp.maximum(jnp.sqrt(jnp.sum(kk * kk, axis=0, keepdims=True)), 1e-12)
        return -kk, kk * iclr, k_raw * (1.0 + (iclr - 1.0) * ka_ref[...])

    @pl.when(pl.program_id(0) == 0)
    def _():
        st_ref[...] = jnp.zeros_like(st_ref)
        sa_ref[...] = jnp.zeros_like(sa_ref)
        for g in range(G):
            _, bv_ref[0, g], km_ref[0, g] = step_vectors(k_ref[0, g], a_ref[0, g])

    t0 = pl.program_id(0) * tt

    def group_step(tg, carry):
        t = tg // G
        g = tg % G
        cur = (t0 + t) % 2
        last = t + 1 >= tt
        t_nx = jnp.minimum(t + 1, tt - 1)
        k_nx = jnp.where(last, kn_ref[0, g], k_ref[t_nx, g])
        a_nx = jnp.where(last, an_ref[0, g], a_ref[t_nx, g])
        av_ref[...], bv_ref[1 - cur, g], km_ref[1 - cur, g] = step_vectors(k_nx, a_nx)

        def half_step(ih, c):
            rows = pl.ds(pl.multiple_of(ih * half, half), half)
            sa_h = sa_ref[g, rows, :]
            vt = v_ref[t, g, rows, :]
            y = jnp.zeros((half, LANES), F32)
            sa_nx = jnp.zeros((half, LANES), F32)
            for j in range(N):
                row = slice(j, j + 1)
                s_new = (st_ref[g, j, rows, :] * w_ref[t, g, row, :] + sa_h * bv_ref[cur, g, row, :]
                         + vt * km_ref[cur, g, row, :])
                st_ref[g, j, rows, :] = s_new
                y = y + s_new * r_ref[t, g, row, :]
                sa_nx = sa_nx + s_new * av_ref[row, :]
            y_ref[t, g, rows, :] = y
            sa_ref[g, rows, :] = sa_nx
            return c

        lax.fori_loop(0, 2, half_step, 0)

        y = y_ref[t, g]
        mean = jnp.mean(y, axis=0, keepdims=True)
        d = y - mean
        var = jnp.mean(d * d, axis=0, keepdims=True)
        bonus = jnp.sum(r_ref[t, g] * km_ref[cur, g] * rk_ref[...], axis=0, keepdims=True)
        y_ref[t, g] = d * lax.rsqrt(var + RWKV_GN_EPS) * lnw_ref[...] + lnb_ref[...] + bonus * v_ref[t, g]
        return carry

    lax.fori_loop(0, tt * G, group_step, 0)


def _rwkv_scan(r, w, k, v, a, k_k, k_a, r_k, ln_w, ln_b, tt):
    S, G, N, _ = r.shape
    blk = pl.BlockSpec((tt, G, N, LANES), lambda i: (i, 0, 0, 0))
    nxt = pl.BlockSpec((1, G, N, LANES), lambda i: (jnp.minimum((i + 1) * tt, S - 1), 0, 0, 0))
    par = pl.BlockSpec((N, LANES), lambda i: (0, 0))
    tile = pltpu.VMEM((N, LANES), F32)
    per_g = pltpu.VMEM((G, N, LANES), F32)
    two_g = pltpu.VMEM((2, G, N, LANES), F32)
    return pl.pallas_call(
        _rwkv_scan_kernel,
        grid=(S // tt,),
        in_specs=[blk] * 5 + [nxt] * 2 + [par] * 5,
        out_specs=blk,
        out_shape=jax.ShapeDtypeStruct((S, G, N, LANES), F32),
        scratch_shapes=[pltpu.VMEM((G, N, N, LANES), F32), per_g, tile, two_g, two_g],
        compiler_params=_cparams("arbitrary"),
        name="rwkv_scan",
    )(r, w, k, v, a, k, a, k_k, k_a, r_k, ln_w, ln_b)


def _merge_kernel(x_ref, ya_ref, ys_ref, gate_ref, g_mix_ref, wg_ref, wa_ref, wb_ref, wo_ref, g_ffn_ref,
                  rw_ref, rb_ref,
                  x1_ref, h2_ref, e_ref, rank_ref, p_ref, cnt_ref, carry_ref, *, n_experts):
    tm, D = x_ref.shape

    @pl.when(pl.program_id(0) == 0)
    def _():
        carry_ref[...] = jnp.zeros_like(carry_ref)

    hm = tm // MERGE_PARTS
    lane = lax.broadcasted_iota(jnp.int32, (hm, LANES), 1)

    def mix(rows):
        x = x_ref[rows, :]
        h = _rms(x, g_mix_ref[...]).astype(BF16)
        gates = _sigmoid(_dot(h, wg_ref[...]))
        yb = (ys_ref[rows, :] * gate_ref[rows, :].astype(F32)).astype(BF16)
        merged = gates[:, :D] * _dot(ya_ref[rows, :], wa_ref[...]) + gates[:, D:] * _dot(yb, wb_ref[...])
        x1 = x + _dot(merged.astype(BF16), wo_ref[...])
        x1_ref[rows, :] = x1
        h2 = _rms(x1, g_ffn_ref[...])
        _store_packed(h2_ref.at[rows], h2)
        return jnp.where(lane < n_experts, _dot_hi(h2, rw_ref[...]) + rb_ref[...], -jnp.inf)

    def route(rows, logits, seen):
        vals, idxs, hots = [], [], []
        for _ in range(TOP_K):
            m = jnp.max(logits, axis=-1, keepdims=True)
            idx = jnp.min(jnp.where(logits == m, lane, LANES), axis=-1, keepdims=True)
            hot = lane == idx
            vals.append(m)
            idxs.append(idx)
            hots.append(hot)
            logits = jnp.where(hot, -jnp.inf, logits)
        exps = [jnp.exp(val - vals[0]) for val in vals]
        denom = functools.reduce(lambda s, e: s + e, exps)
        chosen = functools.reduce(lambda s, e: s + e, [hot.astype(F32) for hot in hots])
        r_i = lax.broadcasted_iota(jnp.int32, (hm, hm), 0)
        c_i = lax.broadcasted_iota(jnp.int32, (hm, hm), 1)
        before = _dot((r_i > c_i).astype(BF16), chosen.astype(BF16)) + seen
        e_out = jnp.zeros((hm, LANES), jnp.int32)
        rank_out = jnp.zeros((hm, LANES), jnp.int32)
        p_out = jnp.zeros((hm, LANES), F32)
        for s in range(TOP_K):
            rank = jnp.sum(jnp.where(hots[s], before, 0.0), axis=-1, keepdims=True)
            e_out = jnp.where(lane == s, idxs[s], e_out)
            rank_out = jnp.where(lane == s, rank.astype(jnp.int32), rank_out)
            p_out = jnp.where(lane == s, exps[s] / denom, p_out)
        e_ref[rows, :] = e_out
        rank_ref[rows, :] = rank_out
        p_ref[rows, :] = p_out
        return seen + jnp.sum(chosen, axis=0, keepdims=True)

    parts = [slice(i * hm, (i + 1) * hm) for i in range(MERGE_PARTS)]
    logits = [mix(rows) for rows in parts]
    seen = carry_ref[0:1, :]
    for rows, lg in zip(parts, logits):
        seen = route(rows, lg, seen)
    carry_ref[0:1, :] = seen
    cnt_ref[...] = jnp.broadcast_to(seen, cnt_ref.shape).astype(jnp.int32)


def _merge(x, ya, ys, gate, consts, n_experts, tm):
    T, D = x.shape
    W = ys.shape[-1]
    tok = lambda width: pl.BlockSpec((tm, width), lambda i: (i, 0))
    full = lambda a: pl.BlockSpec(a.shape, lambda i: (0,) * a.ndim)
    return pl.pallas_call(
        functools.partial(_merge_kernel, n_experts=n_experts),
        grid=(T // tm,),
        in_specs=[tok(D), tok(W), tok(W), tok(W)] + [full(c) for c in consts],
        out_specs=[tok(D), pl.BlockSpec((tm, D // 2 // LANES, LANES), lambda i: (i, 0, 0)),
                   tok(LANES), tok(LANES), tok(LANES), pl.BlockSpec((8, LANES), lambda i: (0, 0))],
        out_shape=[jax.ShapeDtypeStruct((T, D), F32), jax.ShapeDtypeStruct((T, D // 2 // LANES, LANES), U32),
                   jax.ShapeDtypeStruct((T, LANES), jnp.int32), jax.ShapeDtypeStruct((T, LANES), jnp.int32),
                   jax.ShapeDtypeStruct((T, LANES), F32), jax.ShapeDtypeStruct((8, LANES), jnp.int32)],
        scratch_shapes=[pltpu.VMEM((8, LANES), F32)],
        compiler_params=_cparams("arbitrary"),
        name="merge_router",
    )(x, ya, ys, gate, *consts)


def _idx_copy(dest_ref, idx_ref, sem, tile, n_idx):
    slot = tile % 2
    return pltpu.make_async_copy(dest_ref.at[pl.ds(pl.multiple_of(tile * n_idx, n_idx), n_idx)],
                                 idx_ref.at[pl.ds(pl.multiple_of(slot * n_idx, n_idx), n_idx)], sem.at[slot])


def _dispatch_kernel(dest_ref, rows_ref, init_ref, xs_ref, idx_ref, idx_sem, row_sem, *, tm):
    del init_ref
    i = pl.program_id(0)
    n = pl.num_programs(0)
    n_idx = tm * TOP_K
    idx_copy = functools.partial(_idx_copy, dest_ref, idx_ref, idx_sem, n_idx=n_idx)

    @pl.when(i == 0)
    def _():
        idx_copy(i).start()

    idx_copy(i).wait()

    @pl.when(i + 1 < n)
    def _():
        idx_copy(i + 1).start()

    base = (i % 2) * n_idx

    def issue(g, c):
        r0 = g * ISSUE_GROUP
        dst = [idx_ref[base + r0 * TOP_K + k] for k in range(ISSUE_GROUP * TOP_K)]
        for k, d in enumerate(dst):
            pltpu.make_async_copy(rows_ref.at[r0 + k // TOP_K], xs_ref.at[d], row_sem).start(priority=k % 2)
        return c

    lax.fori_loop(0, tm // ISSUE_GROUP, issue, 0)

    for _ in range(TOP_K):
        pltpu.make_async_copy(rows_ref, xs_ref.at[pl.ds(0, tm)], row_sem).wait()


def _dispatch(dest, rows, n_rows, tm):
    T = rows.shape[0]
    packed = (n_rows,) + rows.shape[1:]
    any_spec = pl.BlockSpec(memory_space=pl.ANY)
    return pl.pallas_call(
        functools.partial(_dispatch_kernel, tm=tm),
        grid=(T // tm,),
        in_specs=[any_spec, pl.BlockSpec((tm,) + rows.shape[1:], lambda i: (i, 0, 0)), any_spec],
        out_specs=any_spec,
        out_shape=jax.ShapeDtypeStruct(packed, U32),
        input_output_aliases={2: 0},
        scratch_shapes=[pltpu.SMEM((2 * tm * TOP_K,), jnp.int32), pltpu.SemaphoreType.DMA((2,)),
                        pltpu.SemaphoreType.DMA],
        compiler_params=_cparams("arbitrary"),
        name="moe_dispatch",
    )(dest, rows, jnp.zeros(packed, U32))


def _split_w1_kernel(w_ref, glu_ref, lin_ref):
    cols = w_ref.shape[-1]
    src = lax.broadcasted_iota(jnp.int32, (cols, cols // 2), 0)
    dst = lax.broadcasted_iota(jnp.int32, (cols, cols // 2), 1)
    w = w_ref[0].astype(BF16)
    glu_ref[0] = _dot(w, (src == 2 * dst).astype(BF16)).astype(BF16)
    lin_ref[0] = _dot(w, (src == 2 * dst + 1).astype(BF16)).astype(BF16)


def _split_w1(w1, cols=512):
    E, D, FF2 = w1.shape
    out = jax.ShapeDtypeStruct((E, D, FF2 // 2), BF16)
    return pl.pallas_call(
        _split_w1_kernel,
        grid=(E, FF2 // cols),
        in_specs=[pl.BlockSpec((1, D, cols), lambda e, c: (e, 0, c))],
        out_specs=[pl.BlockSpec((1, D, cols // 2), lambda e, c: (e, 0, c))] * 2,
        out_shape=[out, out],
        compiler_params=_cparams("parallel", "parallel"),
        name="split_w1",
    )(w1)


def _expert_kernel(be_ref, nu_ref, x_ref, w1g_ref, w1l_ref, b1g_ref, b1l_ref, w2_ref, b2_ref, o_ref):
    del be_ref
    used = pl.program_id(0) < nu_ref[0]

    @pl.when(used)
    def _():
        lo, hi = _load_packed(x_ref)
        x = jnp.concatenate([lo.astype(BF16), hi.astype(BF16)], axis=1)
        glu = jnp.minimum(_dot(x, w1g_ref[0]) + b1g_ref[0], SWIGLU_LIMIT)
        lin = jnp.clip(_dot(x, w1l_ref[0]) + b1l_ref[0], -SWIGLU_LIMIT, SWIGLU_LIMIT)
        act = glu * _sigmoid(SWIGLU_ALPHA * glu) * (lin + 1.0)
        _store_packed(o_ref, _dot(act.astype(BF16), w2_ref[0]) + b2_ref[0])

    @pl.when(jnp.logical_not(used))
    def _():
        o_ref[...] = jnp.zeros_like(o_ref)


def _experts(block_expert, n_used, xs, w1g, w1l, b1g, b1l, w2, b2):
    R = xs.shape[0]
    D, FF = w1g.shape[1:]
    nb = R // MOE_ROWS
    per_e = lambda shape: pl.BlockSpec((1,) + shape, lambda i, be, nu: (be[i], 0, 0))
    rows = pl.BlockSpec((MOE_ROWS,) + xs.shape[1:], lambda i, be, nu: (i, 0, 0))
    return pl.pallas_call(
        _expert_kernel,
        grid_spec=pltpu.PrefetchScalarGridSpec(
            num_scalar_prefetch=2,
            grid=(nb,),
            in_specs=[rows, per_e((D, FF)), per_e((D, FF)), per_e((1, FF)), per_e((1, FF)),
                      per_e((FF, D)), per_e((1, D))],
            out_specs=rows,
        ),
        out_shape=jax.ShapeDtypeStruct(xs.shape, U32),
        compiler_params=_cparams("arbitrary"),
        name="expert_ffn",
    )(block_expert, n_used, xs, w1g, w1l, b1g, b1l, w2, b2)


def _combine_kernel(dest_ref, rows_ref, x1_ref, p_ref, g_ref, o_ref, buf_ref, idx_ref, idx_sem, row_sem):
    tm, D = x1_ref.shape
    i = pl.program_id(0)
    n = pl.num_programs(0)
    n_idx = tm * TOP_K
    idx_copy = functools.partial(_idx_copy, dest_ref, idx_ref, idx_sem, n_idx=n_idx)

    def gather(tile):
        slot = tile % 2
        base = slot * n_idx

        def issue(g, c):
            r0 = g * ISSUE_GROUP
            src = [idx_ref[base + r0 * TOP_K + k] for k in range(ISSUE_GROUP * TOP_K)]
            for k, d in enumerate(src):
                pltpu.make_async_copy(rows_ref.at[d], buf_ref.at[slot, k % TOP_K, r0 + k // TOP_K],
                                      row_sem.at[slot]).start(priority=k % 2)
            return c

        lax.fori_loop(0, tm // ISSUE_GROUP, issue, 0)

    @pl.when(i == 0)
    def _():
        idx_copy(i).start()
        idx_copy(i).wait()
        gather(i)

        @pl.when(n > 1)
        def _():
            idx_copy(i + 1).start()

    @pl.when(i + 1 < n)
    def _():
        idx_copy(i + 1).wait()
        gather(i + 1)

    @pl.when(i + 2 < n)
    def _():
        idx_copy(i + 2).start()

    slot = i % 2
    for s in range(TOP_K):
        pltpu.make_async_copy(rows_ref.at[pl.ds(0, tm)], buf_ref.at[slot, s], row_sem.at[slot]).wait()

    half = D // 2
    x1 = x1_ref[...]
    acc_lo, acc_hi = x1[:, :half], x1[:, half:]
    p = p_ref[...]
    for s in range(TOP_K):
        lo, hi = _load_packed(buf_ref.at[slot, s])
        acc_lo = acc_lo + p[:, s:s + 1] * lo
        acc_hi = acc_hi + p[:, s:s + 1] * hi
    ms = (jnp.sum(acc_lo * acc_lo, axis=-1, keepdims=True) + jnp.sum(acc_hi * acc_hi, axis=-1, keepdims=True)) / D
    scale = lax.rsqrt(ms + RMS_EPS)
    o_ref[:, :half] = acc_lo * scale * g_ref[:, :half]
    o_ref[:, half:] = acc_hi * scale * g_ref[:, half:]


def _combine(dest, rows, x1, p, gain, tm):
    T, D = x1.shape
    any_spec = pl.BlockSpec(memory_space=pl.ANY)
    return pl.pallas_call(
        _combine_kernel,
        grid=(T // tm,),
        in_specs=[any_spec, any_spec, pl.BlockSpec((tm, D), lambda i: (i, 0)),
                  pl.BlockSpec((tm, LANES), lambda i: (i, 0)), pl.BlockSpec(gain.shape, lambda i: (0, 0))],
        out_specs=pl.BlockSpec((tm, D), lambda i: (i, 0)),
        out_shape=jax.ShapeDtypeStruct((T, D), F32),
        scratch_shapes=[pltpu.VMEM((2, TOP_K, tm) + rows.shape[1:], U32), pltpu.SMEM((2 * tm * TOP_K,), jnp.int32),
                        pltpu.SemaphoreType.DMA((2,)), pltpu.SemaphoreType.DMA((2,))],
        compiler_params=_cparams("arbitrary"),
        name="moe_combine_norm",
    )(dest, rows, x1, p, gain)


def _pick_tile(n, pref):
    t = min(n, pref)
    assert n % t == 0, (n, t)
    return t


def kernel(x, norm_mix_g, w_in, gla_gate_up, gla_gate_bias, gla_norm_g, rwkv_mu, rwkv_w0, rwkv_w2, rwkv_a0, rwkv_a2, rwkv_g2, rwkv_k_k, rwkv_k_a, rwkv_r_k, rwkv_ln_w, rwkv_ln_b, w_branch_a, w_branch_b, w_out, norm_ffn_g, router_w, router_b, expert_w1, expert_b1, expert_w2, expert_b2, norm_final_g):
    assert norm_mix_g.shape[0] == 1, "single-layer block"
    B, S, D = x.shape
    T = B * S
    qk = gla_gate_bias.shape[-1]
    vw = GLA_HEADS * GLA_DV
    W = rwkv_w0.shape[-1]
    H = W // RWKV_HEAD_DIM
    E = router_w.shape[-1]
    assert qk == GLA_HEADS * GLA_DK and B * H % LANES == 0 and E <= LANES
    row = lambda a: a.reshape(1, -1)

    wi = w_in[0]
    gla_cols = 2 * qk + vw + GLA_GATE_RANK + vw
    rwkv_cols = rwkv_mu.shape[-1]
    o_q, o_k, o_v, o_lr, o_og = 0, qk, 2 * qk, 2 * qk + vw, 2 * qk + vw + GLA_GATE_RANK
    w_gla = jnp.concatenate(
        [wi[:, o_q:o_lr], wi[:, o_og:gla_cols], wi[:, o_lr:o_og],
         jnp.zeros((D, LANES - GLA_GATE_RANK), F32)], axis=1).astype(BF16)
    up = jnp.concatenate([gla_gate_up[0], jnp.zeros((LANES - GLA_GATE_RANK, qk), F32)], axis=0).astype(BF16)
    w_rwkv = wi[:, gla_cols:gla_cols + rwkv_cols].astype(BF16)
    w_gate = wi[:, gla_cols + rwkv_cols:].astype(BF16)
    n_w, n_a = rwkv_w2.shape[1], rwkv_a2.shape[1]
    assert n_w + n_a == LANES
    w2p = jnp.concatenate([rwkv_w2[0], jnp.zeros((n_a, W), F32)], axis=0).astype(BF16)
    a2p = jnp.concatenate([jnp.zeros((n_w, W), F32), rwkv_a2[0]], axis=0).astype(BF16)

    tm_gla = _pick_tile(S, 512)
    q, k, v, og, la = _gla_proj(x, row(norm_mix_g), w_gla, up, row(gla_gate_bias), tm_gla)
    ya = _gla(q, k, v, og, la, row(gla_norm_g), _pick_tile(S, 256))

    tm_r = _pick_tile(S, 256)
    r_, w_, k_, v_, a_, gate_ = _rwkv_proj(x, row(norm_mix_g), w_rwkv, row(rwkv_mu), row(rwkv_w0), w2p,
                                           row(rwkv_a0), a2p, rwkv_g2[0].astype(BF16), tm_r)
    G = B * H // LANES
    bg = B // G
    to_scan = lambda t: t.reshape(G, bg, S, H, RWKV_HEAD_DIM).transpose(2, 0, 4, 1, 3).reshape(
        S, G, RWKV_HEAD_DIM, LANES)
    par = lambda p: jnp.tile(p.reshape(H, RWKV_HEAD_DIM).T, (1, bg))
    ys = _rwkv_scan(*(to_scan(t) for t in (r_, w_, k_, v_, a_)), par(rwkv_k_k), par(rwkv_k_a), par(rwkv_r_k),
                    par(rwkv_ln_w), par(rwkv_ln_b), _pick_tile(S, 16))
    ys = ys.reshape(S, G, RWKV_HEAD_DIM, bg, H).transpose(1, 3, 0, 4, 2).reshape(T, W)

    rw = jnp.concatenate([router_w[0], jnp.zeros((D, LANES - E), F32)], axis=1)
    rb = jnp.concatenate([router_b[0], jnp.zeros((LANES - E,), F32)]).reshape(1, LANES)
    consts = (row(norm_mix_g), w_gate, w_branch_a[0].astype(BF16), w_branch_b[0].astype(BF16),
              w_out[0].astype(BF16), row(norm_ffn_g), rw, rb)
    tm_m = _pick_tile(T, 512)
    x1, h2, e_sel, rank, p_sel, counts = _merge(
        x.reshape(T, D), ya.reshape(T, vw), ys, gate_.reshape(T, W), consts, E, tm_m)

    counts = counts[0, :E]
    padded = (counts + MOE_ROWS - 1) // MOE_ROWS * MOE_ROWS
    pad_ends = jnp.cumsum(padded)
    pad_starts = pad_ends - padded
    nb = -(-T * TOP_K // MOE_ROWS) + E
    dest = pad_starts[e_sel[:, :TOP_K]] + rank[:, :TOP_K]
    block_start = jnp.arange(nb, dtype=jnp.int32) * MOE_ROWS
    block_expert = jnp.minimum(jnp.sum(pad_ends[None, :] <= block_start[:, None], axis=1), E - 1).astype(jnp.int32)
    n_used = (pad_ends[-1:] // MOE_ROWS).astype(jnp.int32)
    dest = dest.reshape(-1).astype(jnp.int32)
    tm_d = _pick_tile(T, 256)
    xs = _dispatch(dest, h2, nb * MOE_ROWS, _pick_tile(T, 512))

    FF = expert_w2.shape[2]
    w1g, w1l = _split_w1(expert_w1[0])
    b1 = expert_b1[0].reshape(E, 1, FF, 2)
    outs_e = _experts(block_expert, n_used, xs, w1g, w1l, b1[..., 0], b1[..., 1],
                      expert_w2[0].astype(BF16), expert_b2[0].reshape(E, 1, D))

    out = _combine(dest, outs_e, x1, p_sel, row(norm_final_g), tm_d)
    return out.reshape(B, S, D)
```

```python
import functools

import jax
import jax.numpy as jnp
from jax import lax
from jax.experimental import pallas as pl
from jax.experimental.pallas import tpu as pltpu

F32 = jnp.float32
BF16 = jnp.bfloat16
U32 = jnp.uint32
HIGHEST = lax.Precision.HIGHEST

LANES = 128
VMEM_LIMIT = 56 * 1024 * 1024

RMS_EPS = 1e-6
GLA_HEADS = 4
GLA_DK = 128
GLA_DV = 256
GLA_GATE_RANK = 16
GLA_TAU = 16.0
GLA_CHUNK = 64
GLA_NORM_EPS = 1e-5
RWKV_HEAD_DIM = 64
RWKV_GN_EPS = 64e-5
TOP_K = 4
SWIGLU_ALPHA = 1.702
SWIGLU_LIMIT = 7.0
MOE_ROWS = 512
ISSUE_GROUP = 4
MERGE_PARTS = 2


def _cparams(*sem):
    return pltpu.CompilerParams(dimension_semantics=sem, vmem_limit_bytes=VMEM_LIMIT)


def _rms(xf, gain):
    return xf * lax.rsqrt(jnp.mean(xf * xf, axis=-1, keepdims=True) + RMS_EPS) * gain


def _softplus(y):
    return jnp.maximum(y, 0.0) + jnp.log1p(jnp.exp(-jnp.abs(y)))


def _sigmoid(y):
    return 1.0 / (1.0 + jnp.exp(-y))


def _dot(a, b):
    return jnp.dot(a, b, preferred_element_type=F32)


def _dot_hi(a, b):
    return jnp.dot(a, b, preferred_element_type=F32, precision=HIGHEST)


def _pack_pair(lo, hi):
    bits = lambda t: lax.bitcast_convert_type(t.astype(BF16).astype(F32), U32)
    return (bits(lo) >> 16) | bits(hi)


def _pack_rows(x):
    n = x.shape[-1] // 2
    return _pack_pair(x[:, :n], x[:, n:])


def _unpack_rows(w):
    lo = lax.bitcast_convert_type(w << 16, F32)
    hi = lax.bitcast_convert_type(w & jnp.uint32(0xFFFF0000), F32)
    return lo, hi


def _store_packed(ref, x):
    packed = _pack_rows(x)
    for c in range(ref.shape[1]):
        ref[:, c, :] = packed[:, c * LANES:(c + 1) * LANES]


def _load_packed(ref):
    return _unpack_rows(jnp.concatenate([ref[:, c, :] for c in range(ref.shape[1])], axis=1))


def _gla_proj_kernel(x_ref, g_ref, w_ref, up_ref, bias_ref, q_ref, k_ref, v_ref, og_ref, la_ref):
    qk = q_ref.shape[-1]
    vw = v_ref.shape[-1]
    h = _rms(x_ref[0], g_ref[...]).astype(BF16)
    z = _dot(h, w_ref[...])
    q_ref[0] = z[:, :qk].astype(BF16)
    k_ref[0] = z[:, qk:2 * qk].astype(BF16)
    v_ref[0] = z[:, 2 * qk:2 * qk + vw].astype(BF16)
    og_ref[0] = z[:, 2 * qk + vw:2 * qk + 2 * vw].astype(BF16)
    lr = z[:, 2 * qk + 2 * vw:].astype(BF16)
    pre = _dot(lr, up_ref[...]) + bias_ref[...]
    la_ref[0] = -_softplus(-pre) * (1.0 / GLA_TAU)


def _gla_proj(x, gain, w, up, bias, tm):
    B, S, D = x.shape
    qk = bias.shape[-1]
    vw = (w.shape[1] - 2 * qk - LANES) // 2
    tok = lambda width: pl.BlockSpec((1, tm, width), lambda b, i: (b, i, 0))
    full = lambda a: pl.BlockSpec(a.shape, lambda b, i: (0,) * a.ndim)
    return pl.pallas_call(
        _gla_proj_kernel,
        grid=(B, S // tm),
        in_specs=[tok(D), full(gain), full(w), full(up), full(bias)],
        out_specs=[tok(qk), tok(qk), tok(vw), tok(vw), tok(qk)],
        out_shape=[jax.ShapeDtypeStruct((B, S, qk), BF16), jax.ShapeDtypeStruct((B, S, qk), BF16),
                   jax.ShapeDtypeStruct((B, S, vw), BF16), jax.ShapeDtypeStruct((B, S, vw), BF16),
                   jax.ShapeDtypeStruct((B, S, qk), F32)],
        compiler_params=_cparams("parallel", "parallel"),
        name="gla_proj",
    )(x, gain, w, up, bias)


def _gla_kernel(q_ref, k_ref, v_ref, og_ref, la_ref, ng_ref, y_ref, st_ref, qa_ref, ka_ref, qi_ref, ks_ref, dec_ref,
                sc_ref):
    C = GLA_CHUNK
    tc = q_ref.shape[1]

    @pl.when(pl.program_id(1) == 0)
    def _():
        st_ref[...] = jnp.zeros_like(st_ref)

    row = lax.broadcasted_iota(jnp.int32, (tc, tc), 0)
    col = lax.broadcasted_iota(jnp.int32, (tc, tc), 1)
    same_chunk_before = jnp.logical_and(row >= col, row // C == col // C)
    b_all = _dot_hi(same_chunk_before.astype(F32), la_ref[0])
    scale = GLA_DK ** -0.5
    for c in range(tc // C):
        rows = slice(c * C, (c + 1) * C)
        b = b_all[rows]
        b_mid = b[C // 2:C // 2 + 1]
        b_end = b[C - 1:C]
        q = q_ref[0, rows, :].astype(F32) * scale
        k = k_ref[0, rows, :].astype(F32)
        qa_ref[rows, :] = (q * jnp.exp(b - b_mid)).astype(BF16)
        ka_ref[rows, :] = (k * jnp.exp(b_mid - b)).astype(BF16)
        qi_ref[rows, :] = (q * jnp.exp(b)).astype(BF16)
        ks_ref[rows, :] = (k * jnp.exp(b_end - b)).astype(BF16)
        dec_ref[c:c + 1, :] = jnp.exp(b_end)

    r_c = lax.broadcasted_iota(jnp.int32, (C, C), 0)
    c_c = lax.broadcasted_iota(jnp.int32, (C, C), 1)
    causal = r_c >= c_c
    for c in range(tc // C):
        rows = slice(c * C, (c + 1) * C)
        for h in range(GLA_HEADS):
            kc = slice(h * GLA_DK, (h + 1) * GLA_DK)
            scores = lax.dot_general(qa_ref[rows, kc], ka_ref[rows, kc], (((1,), (1,)), ((), ())),
                                     preferred_element_type=F32)
            sc_ref[rows, h * C:(h + 1) * C] = jnp.where(causal, scores, 0.0).astype(BF16)

    for c in range(tc // C):
        rows = slice(c * C, (c + 1) * C)
        for h in range(GLA_HEADS):
            kc = slice(h * GLA_DK, (h + 1) * GLA_DK)
            vc = slice(h * GLA_DV, (h + 1) * GLA_DV)
            vh = v_ref[0, rows, vc]
            st = st_ref[h]
            o = _dot(sc_ref[rows, h * C:(h + 1) * C], vh) + lax.dot_general(
                qi_ref[rows, kc], st.astype(BF16), (((1,), (1,)), ((), ())), preferred_element_type=F32)
            st_ref[h] = st * dec_ref[c:c + 1, kc] + lax.dot_general(
                vh, ks_ref[rows, kc], (((0,), (0,)), ((), ())), preferred_element_type=F32)
            o = o * lax.rsqrt(jnp.mean(o * o, axis=-1, keepdims=True) + GLA_NORM_EPS) * ng_ref[...]
            og = og_ref[0, rows, vc].astype(F32)
            y_ref[0, rows, vc] = (o * (og * _sigmoid(og))).astype(BF16)


def _gla(q, k, v, og, la, norm_g, tc):
    B, S, qk = q.shape
    vw = v.shape[-1]
    tok = lambda width: pl.BlockSpec((1, tc, width), lambda b, i: (b, i, 0))
    return pl.pallas_call(
        _gla_kernel,
        grid=(B, S // tc),
        in_specs=[tok(qk), tok(qk), tok(vw), tok(vw), tok(qk),
                  pl.BlockSpec(norm_g.shape, lambda b, i: (0, 0))],
        out_specs=tok(vw),
        out_shape=jax.ShapeDtypeStruct((B, S, vw), BF16),
        scratch_shapes=[pltpu.VMEM((GLA_HEADS, GLA_DV, GLA_DK), F32)] + [pltpu.VMEM((tc, qk), BF16)] * 4
        + [pltpu.VMEM((8, qk), F32), pltpu.VMEM((tc, GLA_HEADS * GLA_CHUNK), BF16)],
        compiler_params=_cparams("parallel", "arbitrary"),
        name="gla_chunk",
    )(q, k, v, og, la, norm_g)


def _rwkv_proj_kernel(x_ref, g_ref, w_ref, mu_ref, w0_ref, w2_ref, a0_ref, a2_ref, g2_ref,
                      rv_ref, w_out_ref, ka_ref, gate_ref, carry_ref):
    W = rv_ref.shape[-1]
    tm = x_ref.shape[1]

    @pl.when(pl.program_id(1) == 0)
    def _():
        carry_ref[...] = jnp.zeros_like(carry_ref)

    h = _rms(x_ref[0], g_ref[...]).astype(BF16)
    z = _dot(h, w_ref[...])
    rolled = pltpu.roll(z, 1, axis=0)
    first = lax.broadcasted_iota(jnp.int32, z.shape, 0) == 0
    prev = jnp.where(first, carry_ref[0:1, :], rolled)
    carry_ref[0:1, :] = z[tm - 1:tm, :]
    u = z + mu_ref[...] * (prev - z)

    r = u[:, :W]
    k = u[:, W:2 * W]
    v = u[:, 2 * W:3 * W]
    lora = u[:, 3 * W:3 * W + LANES]
    xg = u[:, 3 * W + LANES:]
    w_log = -_softplus(-(w0_ref[...] + _dot(jnp.tanh(lora).astype(BF16), w2_ref[...]))) - 0.5
    decay = jnp.exp(-jnp.exp(w_log))
    iclr = _sigmoid(a0_ref[...] + _dot(lora.astype(BF16), a2_ref[...]))
    rv_ref[0] = _pack_pair(r, v)
    ka_ref[0] = _pack_pair(k, iclr)
    w_out_ref[0] = decay
    gate_ref[0] = _dot(_sigmoid(xg).astype(BF16), g2_ref[...]).astype(BF16)


def _rwkv_proj(x, gain, w, mu, w0, w2p, a0, a2p, g2, tm):
    B, S, D = x.shape
    W = w0.shape[-1]
    tok = lambda width: pl.BlockSpec((1, tm, width), lambda b, i: (b, i, 0))
    full = lambda a: pl.BlockSpec(a.shape, lambda b, i: (0,) * a.ndim)
    consts = (gain, w, mu, w0, w2p, a0, a2p, g2)
    return pl.pallas_call(
        _rwkv_proj_kernel,
        grid=(B, S // tm),
        in_specs=[tok(D)] + [full(c) for c in consts],
        out_specs=[tok(W)] * 4,
        out_shape=[jax.ShapeDtypeStruct((B, S, W), U32), jax.ShapeDtypeStruct((B, S, W), F32),
                   jax.ShapeDtypeStruct((B, S, W), U32), jax.ShapeDtypeStruct((B, S, W), BF16)],
        scratch_shapes=[pltpu.VMEM((8, w.shape[1]), F32)],
        compiler_params=_cparams("parallel", "arbitrary"),
        name="rwkv_proj",
    )(x, *consts)


def _rwkv_scan_kernel(rv_ref, w_ref, kia_ref, kian_ref, kk_ref, ka_ref, rk_ref, lnw_ref, lnb_ref,
                      y_ref, st_ref, sa_ref, av_ref, bv_ref, km_ref, r_ref, v_ref, yr_ref):
    tt, G, N, _ = rv_ref.shape
    half = N // 2

    def step_vectors(packed):
        k_raw, iclr = _unpack_rows(packed)
        kk = k_raw * kk_ref[...]
        kk = kk / jnp.maximum(jnp.sqrt(jnp.sum(kk * kk, axis=0, keepdims=True)), 1e-12)
        return -kk, kk * iclr, k_raw * (1.0 + (iclr - 1.0) * ka_ref[...])

    @pl.when(pl.program_id(0) == 0)
    def _():
        st_ref[...] = jnp.zeros_like(st_ref)
        sa_ref[...] = jnp.zeros_like(sa_ref)
        for g in range(G):
            _, bv_ref[0, g], km_ref[0, g] = step_vectors(kia_ref[0, g])

    t0 = pl.program_id(0) * tt

    def group_step(tg, carry):
        t = tg // G
        g = tg % G
        cur = (t0 + t) % 2
        last = t + 1 >= tt
        t_nx = jnp.minimum(t + 1, tt - 1)
        av_ref[...], bv_ref[1 - cur, g], km_ref[1 - cur, g] = step_vectors(
            jnp.where(last, kian_ref[0, g], kia_ref[t_nx, g]))
        r_ref[...], v_ref[...] = _unpack_rows(rv_ref[t, g])

        def half_step(ih, c):
            rows = pl.ds(pl.multiple_of(ih * half, half), half)
            sa_h = sa_ref[g, rows, :]
            vt = v_ref[rows, :]
            y = jnp.zeros((half, LANES), F32)
            sa_nx = jnp.zeros((half, LANES), F32)
            for j in range(N):
                row = slice(j, j + 1)
                s_new = (st_ref[g, j, rows, :] * w_ref[t, g, row, :] + sa_h * bv_ref[cur, g, row, :]
                         + vt * km_ref[cur, g, row, :])
                st_ref[g, j, rows, :] = s_new
                y = y + s_new * r_ref[row, :]
                sa_nx = sa_nx + s_new * av_ref[row, :]
            yr_ref[rows, :] = y
            sa_ref[g, rows, :] = sa_nx
            return c

        lax.fori_loop(0, 2, half_step, 0)

        y = yr_ref[...]
        mean = jnp.mean(y, axis=0, keepdims=True)
        d = y - mean
        var = jnp.mean(d * d, axis=0, keepdims=True)
        bonus = jnp.sum(r_ref[...] * km_ref[cur, g] * rk_ref[...], axis=0, keepdims=True)
        out = d * lax.rsqrt(var + RWKV_GN_EPS) * lnw_ref[...] + lnb_ref[...] + bonus * v_ref[...]
        y_ref[t, g] = _pack_pair(out[:half], out[half:])
        return carry

    lax.fori_loop(0, tt * G, group_step, 0)


def _rwkv_scan(rv, w, kia, k_k, k_a, r_k, ln_w, ln_b, tt):
    S, G, N, _ = rv.shape
    blk = pl.BlockSpec((tt, G, N, LANES), lambda i: (i, 0, 0, 0))
    nxt = pl.BlockSpec((1, G, N, LANES), lambda i: (jnp.minimum((i + 1) * tt, S - 1), 0, 0, 0))
    par = pl.BlockSpec((N, LANES), lambda i: (0, 0))
    tile = pltpu.VMEM((N, LANES), F32)
    per_g = pltpu.VMEM((G, N, LANES), F32)
    two_g = pltpu.VMEM((2, G, N, LANES), F32)
    return pl.pallas_call(
        _rwkv_scan_kernel,
        grid=(S // tt,),
        in_specs=[blk] * 3 + [nxt] + [par] * 5,
        out_specs=pl.BlockSpec((tt, G, N // 2, LANES), lambda i: (i, 0, 0, 0)),
        out_shape=jax.ShapeDtypeStruct((S, G, N // 2, LANES), U32),
        scratch_shapes=[pltpu.VMEM((G, N, N, LANES), F32), per_g, tile, two_g, two_g, tile, tile, tile],
        compiler_params=_cparams("arbitrary"),
        name="rwkv_scan",
    )(rv, w, kia, kia, k_k, k_a, r_k, ln_w, ln_b)


def _merge_kernel(x_ref, ya_ref, ys_ref, gate_ref, g_mix_ref, wg_ref, wa_ref, wb_ref, wo_ref, g_ffn_ref,
                  rw_ref, rb_ref,
                  x1_ref, h2_ref, e_ref, rank_ref, p_ref, cnt_ref, carry_ref, *, n_experts):
    tm, D = x_ref.shape

    @pl.when(pl.program_id(0) == 0)
    def _():
        carry_ref[...] = jnp.zeros_like(carry_ref)

    hm = tm // MERGE_PARTS
    lane = lax.broadcasted_iota(jnp.int32, (hm, LANES), 1)

    def mix(rows):
        x = x_ref[rows, :]
        h = _rms(x, g_mix_ref[...]).astype(BF16)
        gates = _sigmoid(_dot(h, wg_ref[...]))
        ys = jnp.concatenate(_unpack_rows(ys_ref[rows, :]), axis=1)
        yb = (ys * gate_ref[rows, :].astype(F32)).astype(BF16)
        merged = gates[:, :D] * _dot(ya_ref[rows, :], wa_ref[...]) + gates[:, D:] * _dot(yb, wb_ref[...])
        x1 = x + _dot(merged.astype(BF16), wo_ref[...])
        x1_ref[rows, :] = x1
        h2 = _rms(x1, g_ffn_ref[...])
        _store_packed(h2_ref.at[rows], h2)
        return jnp.where(lane < n_experts, _dot_hi(h2, rw_ref[...]) + rb_ref[...], -jnp.inf)

    def route(rows, logits, seen):
        vals, idxs, hots = [], [], []
        for _ in range(TOP_K):
            m = jnp.max(logits, axis=-1, keepdims=True)
            idx = jnp.min(jnp.where(logits == m, lane, LANES), axis=-1, keepdims=True)
            hot = lane == idx
            vals.append(m)
            idxs.append(idx)
            hots.append(hot)
            logits = jnp.where(hot, -jnp.inf, logits)
        exps = [jnp.exp(val - vals[0]) for val in vals]
        denom = functools.reduce(lambda s, e: s + e, exps)
        chosen = functools.reduce(lambda s, e: s + e, [hot.astype(F32) for hot in hots])
        r_i = lax.broadcasted_iota(jnp.int32, (hm, hm), 0)
        c_i = lax.broadcasted_iota(jnp.int32, (hm, hm), 1)
        before = _dot((r_i > c_i).astype(BF16), chosen.astype(BF16)) + seen
        e_out = jnp.zeros((hm, LANES), jnp.int32)
        rank_out = jnp.zeros((hm, LANES), jnp.int32)
        p_out = jnp.zeros((hm, LANES), F32)
        for s in range(TOP_K):
            rank = jnp.sum(jnp.where(hots[s], before, 0.0), axis=-1, keepdims=True)
            e_out = jnp.where(lane == s, idxs[s], e_out)
            rank_out = jnp.where(lane == s, rank.astype(jnp.int32), rank_out)
            p_out = jnp.where(lane == s, exps[s] / denom, p_out)
        e_ref[rows, :] = e_out
        rank_ref[rows, :] = rank_out
        p_ref[rows, :] = p_out
        return seen + jnp.sum(chosen, axis=0, keepdims=True)

    parts = [slice(i * hm, (i + 1) * hm) for i in range(MERGE_PARTS)]
    logits = [mix(rows) for rows in parts]
    seen = carry_ref[0:1, :]
    for rows, lg in zip(parts, logits):
        seen = route(rows, lg, seen)
    carry_ref[0:1, :] = seen
    cnt_ref[...] = jnp.broadcast_to(seen, cnt_ref.shape).astype(jnp.int32)


def _merge(x, ya, ys, gate, consts, n_experts, tm):
    T, D = x.shape
    W = gate.shape[-1]
    tok = lambda width: pl.BlockSpec((tm, width), lambda i: (i, 0))
    full = lambda a: pl.BlockSpec(a.shape, lambda i: (0,) * a.ndim)
    return pl.pallas_call(
        functools.partial(_merge_kernel, n_experts=n_experts),
        grid=(T // tm,),
        in_specs=[tok(D), tok(ya.shape[-1]), tok(ys.shape[-1]), tok(W)] + [full(c) for c in consts],
        out_specs=[tok(D), pl.BlockSpec((tm, D // 2 // LANES, LANES), lambda i: (i, 0, 0)),
                   tok(LANES), tok(LANES), tok(LANES), pl.BlockSpec((8, LANES), lambda i: (0, 0))],
        out_shape=[jax.ShapeDtypeStruct((T, D), F32), jax.ShapeDtypeStruct((T, D // 2 // LANES, LANES), U32),
                   jax.ShapeDtypeStruct((T, LANES), jnp.int32), jax.ShapeDtypeStruct((T, LANES), jnp.int32),
                   jax.ShapeDtypeStruct((T, LANES), F32), jax.ShapeDtypeStruct((8, LANES), jnp.int32)],
        scratch_shapes=[pltpu.VMEM((8, LANES), F32)],
        compiler_params=_cparams("arbitrary"),
        name="merge_router",
    )(x, ya, ys, gate, *consts)


def _idx_copy(dest_ref, idx_ref, sem, tile, n_idx):
    slot = tile % 2
    return pltpu.make_async_copy(dest_ref.at[pl.ds(pl.multiple_of(tile * n_idx, n_idx), n_idx)],
                                 idx_ref.at[pl.ds(pl.multiple_of(slot * n_idx, n_idx), n_idx)], sem.at[slot])


def _dispatch_kernel(dest_ref, rows_ref, init_ref, xs_ref, idx_ref, idx_sem, row_sem, *, tm):
    del init_ref
    i = pl.program_id(0)
    n = pl.num_programs(0)
    n_idx = tm * TOP_K
    idx_copy = functools.partial(_idx_copy, dest_ref, idx_ref, idx_sem, n_idx=n_idx)

    @pl.when(i == 0)
    def _():
        idx_copy(i).start()

    idx_copy(i).wait()

    @pl.when(i + 1 < n)
    def _():
        idx_copy(i + 1).start()

    base = (i % 2) * n_idx

    def issue(g, c):
        r0 = g * ISSUE_GROUP
        dst = [idx_ref[base + r0 * TOP_K + k] for k in range(ISSUE_GROUP * TOP_K)]
        for k, d in enumerate(dst):
            pltpu.make_async_copy(rows_ref.at[r0 + k // TOP_K], xs_ref.at[d], row_sem).start(priority=k % 2)
        return c

    lax.fori_loop(0, tm // ISSUE_GROUP, issue, 0)

    for _ in range(TOP_K):
        pltpu.make_async_copy(rows_ref, xs_ref.at[pl.ds(0, tm)], row_sem).wait()


def _dispatch(dest, rows, n_rows, tm):
    T = rows.shape[0]
    packed = (n_rows,) + rows.shape[1:]
    any_spec = pl.BlockSpec(memory_space=pl.ANY)
    return pl.pallas_call(
        functools.partial(_dispatch_kernel, tm=tm),
        grid=(T // tm,),
        in_specs=[any_spec, pl.BlockSpec((tm,) + rows.shape[1:], lambda i: (i, 0, 0)), any_spec],
        out_specs=any_spec,
        out_shape=jax.ShapeDtypeStruct(packed, U32),
        input_output_aliases={2: 0},
        scratch_shapes=[pltpu.SMEM((2 * tm * TOP_K,), jnp.int32), pltpu.SemaphoreType.DMA((2,)),
                        pltpu.SemaphoreType.DMA],
        compiler_params=_cparams("arbitrary"),
        name="moe_dispatch",
    )(dest, rows, jnp.zeros(packed, U32))


def _split_w1_kernel(w_ref, glu_ref, lin_ref):
    cols = w_ref.shape[-1]
    src = lax.broadcasted_iota(jnp.int32, (cols, cols // 2), 0)
    dst = lax.broadcasted_iota(jnp.int32, (cols, cols // 2), 1)
    w = w_ref[0].astype(BF16)
    glu_ref[0] = _dot(w, (src == 2 * dst).astype(BF16)).astype(BF16)
    lin_ref[0] = _dot(w, (src == 2 * dst + 1).astype(BF16)).astype(BF16)


def _split_w1(w1, cols=512):
    E, D, FF2 = w1.shape
    out = jax.ShapeDtypeStruct((E, D, FF2 // 2), BF16)
    return pl.pallas_call(
        _split_w1_kernel,
        grid=(E, FF2 // cols),
        in_specs=[pl.BlockSpec((1, D, cols), lambda e, c: (e, 0, c))],
        out_specs=[pl.BlockSpec((1, D, cols // 2), lambda e, c: (e, 0, c))] * 2,
        out_shape=[out, out],
        compiler_params=_cparams("parallel", "parallel"),
        name="split_w1",
    )(w1)


def _expert_kernel(be_ref, nu_ref, x_ref, w1g_ref, w1l_ref, b1g_ref, b1l_ref, w2_ref, b2_ref, o_ref):
    del be_ref
    used = pl.program_id(0) < nu_ref[0]

    @pl.when(used)
    def _():
        lo, hi = _load_packed(x_ref)
        x = jnp.concatenate([lo.astype(BF16), hi.astype(BF16)], axis=1)
        glu = jnp.minimum(_dot(x, w1g_ref[0]) + b1g_ref[0], SWIGLU_LIMIT)
        lin = jnp.clip(_dot(x, w1l_ref[0]) + b1l_ref[0], -SWIGLU_LIMIT, SWIGLU_LIMIT)
        act = glu * _sigmoid(SWIGLU_ALPHA * glu) * (lin + 1.0)
        _store_packed(o_ref, _dot(act.astype(BF16), w2_ref[0]) + b2_ref[0])

    @pl.when(jnp.logical_not(used))
    def _():
        o_ref[...] = jnp.zeros_like(o_ref)


def _experts(block_expert, n_used, xs, w1g, w1l, b1g, b1l, w2, b2):
    R = xs.shape[0]
    D, FF = w1g.shape[1:]
    nb = R // MOE_ROWS
    per_e = lambda shape: pl.BlockSpec((1,) + shape, lambda i, be, nu: (be[i], 0, 0))
    rows = pl.BlockSpec((MOE_ROWS,) + xs.shape[1:], lambda i, be, nu: (i, 0, 0))
    return pl.pallas_call(
        _expert_kernel,
        grid_spec=pltpu.PrefetchScalarGridSpec(
            num_scalar_prefetch=2,
            grid=(nb,),
            in_specs=[rows, per_e((D, FF)), per_e((D, FF)), per_e((1, FF)), per_e((1, FF)),
                      per_e((FF, D)), per_e((1, D))],
            out_specs=rows,
        ),
        out_shape=jax.ShapeDtypeStruct(xs.shape, U32),
        compiler_params=_cparams("arbitrary"),
        name="expert_ffn",
    )(block_expert, n_used, xs, w1g, w1l, b1g, b1l, w2, b2)


def _combine_kernel(dest_ref, rows_ref, x1_ref, p_ref, g_ref, o_ref, buf_ref, idx_ref, idx_sem, row_sem):
    tm, D = x1_ref.shape
    i = pl.program_id(0)
    n = pl.num_programs(0)
    n_idx = tm * TOP_K
    idx_copy = functools.partial(_idx_copy, dest_ref, idx_ref, idx_sem, n_idx=n_idx)

    def gather(tile):
        slot = tile % 2
        base = slot * n_idx

        def issue(g, c):
            r0 = g * ISSUE_GROUP
            src = [idx_ref[base + r0 * TOP_K + k] for k in range(ISSUE_GROUP * TOP_K)]
            for k, d in enumerate(src):
                pltpu.make_async_copy(rows_ref.at[d], buf_ref.at[slot, k % TOP_K, r0 + k // TOP_K],
                                      row_sem.at[slot]).start(priority=k % 2)
            return c

        lax.fori_loop(0, tm // ISSUE_GROUP, issue, 0)

    @pl.when(i == 0)
    def _():
        idx_copy(i).start()
        idx_copy(i).wait()
        gather(i)

        @pl.when(n > 1)
        def _():
            idx_copy(i + 1).start()

    @pl.when(i + 1 < n)
    def _():
        idx_copy(i + 1).wait()
        gather(i + 1)

    @pl.when(i + 2 < n)
    def _():
        idx_copy(i + 2).start()

    slot = i % 2
    for s in range(TOP_K):
        pltpu.make_async_copy(rows_ref.at[pl.ds(0, tm)], buf_ref.at[slot, s], row_sem.at[slot]).wait()

    half = D // 2
    x1 = x1_ref[...]
    acc_lo, acc_hi = x1[:, :half], x1[:, half:]
    p = p_ref[...]
    for s in range(TOP_K):
        lo, hi = _load_packed(buf_ref.at[slot, s])
        acc_lo = acc_lo + p[:, s:s + 1] * lo
        acc_hi = acc_hi + p[:, s:s + 1] * hi
    ms = (jnp.sum(acc_lo * acc_lo, axis=-1, keepdims=True) + jnp.sum(acc_hi * acc_hi, axis=-1, keepdims=True)) / D
    scale = lax.rsqrt(ms + RMS_EPS)
    o_ref[:, :half] = acc_lo * scale * g_ref[:, :half]
    o_ref[:, half:] = acc_hi * scale * g_ref[:, half:]


def _combine(dest, rows, x1, p, gain, tm):
    T, D = x1.shape
    any_spec = pl.BlockSpec(memory_space=pl.ANY)
    return pl.pallas_call(
        _combine_kernel,
        grid=(T // tm,),
        in_specs=[any_spec, any_spec, pl.BlockSpec((tm, D), lambda i: (i, 0)),
                  pl.BlockSpec((tm, LANES), lambda i: (i, 0)), pl.BlockSpec(gain.shape, lambda i: (0, 0))],
        out_specs=pl.BlockSpec((tm, D), lambda i: (i, 0)),
        out_shape=jax.ShapeDtypeStruct((T, D), F32),
        scratch_shapes=[pltpu.VMEM((2, TOP_K, tm) + rows.shape[1:], U32), pltpu.SMEM((2 * tm * TOP_K,), jnp.int32),
                        pltpu.SemaphoreType.DMA((2,)), pltpu.SemaphoreType.DMA((2,))],
        compiler_params=_cparams("arbitrary"),
        name="moe_combine_norm",
    )(dest, rows, x1, p, gain)


def _pick_tile(n, pref):
    t = min(n, pref)
    assert n % t == 0, (n, t)
    return t


def kernel(x, norm_mix_g, w_in, gla_gate_up, gla_gate_bias, gla_norm_g, rwkv_mu, rwkv_w0, rwkv_w2, rwkv_a0, rwkv_a2, rwkv_g2, rwkv_k_k, rwkv_k_a, rwkv_r_k, rwkv_ln_w, rwkv_ln_b, w_branch_a, w_branch_b, w_out, norm_ffn_g, router_w, router_b, expert_w1, expert_b1, expert_w2, expert_b2, norm_final_g):
    assert norm_mix_g.shape[0] == 1, "single-layer block"
    B, S, D = x.shape
    T = B * S
    qk = gla_gate_bias.shape[-1]
    vw = GLA_HEADS * GLA_DV
    W = rwkv_w0.shape[-1]
    H = W // RWKV_HEAD_DIM
    E = router_w.shape[-1]
    assert qk == GLA_HEADS * GLA_DK and B * H % LANES == 0 and E <= LANES
    row = lambda a: a.reshape(1, -1)

    wi = w_in[0]
    gla_cols = 2 * qk + vw + GLA_GATE_RANK + vw
    rwkv_cols = rwkv_mu.shape[-1]
    o_q, o_k, o_v, o_lr, o_og = 0, qk, 2 * qk, 2 * qk + vw, 2 * qk + vw + GLA_GATE_RANK
    w_gla = jnp.concatenate(
        [wi[:, o_q:o_lr], wi[:, o_og:gla_cols], wi[:, o_lr:o_og],
         jnp.zeros((D, LANES - GLA_GATE_RANK), F32)], axis=1).astype(BF16)
    up = jnp.concatenate([gla_gate_up[0], jnp.zeros((LANES - GLA_GATE_RANK, qk), F32)], axis=0).astype(BF16)
    w_rwkv = wi[:, gla_cols:gla_cols + rwkv_cols].astype(BF16)
    w_gate = wi[:, gla_cols + rwkv_cols:].astype(BF16)
    n_w, n_a = rwkv_w2.shape[1], rwkv_a2.shape[1]
    assert n_w + n_a == LANES
    w2p = jnp.concatenate([rwkv_w2[0], jnp.zeros((n_a, W), F32)], axis=0).astype(BF16)
    a2p = jnp.concatenate([jnp.zeros((n_w, W), F32), rwkv_a2[0]], axis=0).astype(BF16)

    tm_gla = _pick_tile(S, 512)
    q, k, v, og, la = _gla_proj(x, row(norm_mix_g), w_gla, up, row(gla_gate_bias), tm_gla)
    ya = _gla(q, k, v, og, la, row(gla_norm_g), _pick_tile(S, 256))

    tm_r = _pick_tile(S, 256)
    hn = RWKV_HEAD_DIM // 2
    col = jnp.arange(W)
    scan_order = (col % (W // 2)) // hn * RWKV_HEAD_DIM + col // (W // 2) * hn + col % hn
    rv_, w_, kia_, gate_ = _rwkv_proj(x, row(norm_mix_g), w_rwkv, row(rwkv_mu), row(rwkv_w0), w2p,
                                      row(rwkv_a0), a2p, rwkv_g2[0][:, scan_order].astype(BF16), tm_r)
    G = B * H // LANES
    bg = B // G
    to_scan = lambda t: t.reshape(G, bg, S, H, RWKV_HEAD_DIM).transpose(2, 0, 4, 1, 3).reshape(
        S, G, RWKV_HEAD_DIM, LANES)
    par = lambda p: jnp.tile(p.reshape(H, RWKV_HEAD_DIM).T, (1, bg))
    ys = _rwkv_scan(to_scan(rv_), to_scan(w_), to_scan(kia_), par(rwkv_k_k), par(rwkv_k_a), par(rwkv_r_k),
                    par(rwkv_ln_w), par(rwkv_ln_b), _pick_tile(S, 16))
    ys = ys.reshape(S, G, hn, bg, H).transpose(1, 3, 0, 4, 2).reshape(T, W // 2)

    rw = jnp.concatenate([router_w[0], jnp.zeros((D, LANES - E), F32)], axis=1)
    rb = jnp.concatenate([router_b[0], jnp.zeros((LANES - E,), F32)]).reshape(1, LANES)
    consts = (row(norm_mix_g), w_gate, w_branch_a[0].astype(BF16), w_branch_b[0][scan_order, :].astype(BF16),
              w_out[0].astype(BF16), row(norm_ffn_g), rw, rb)
    tm_m = _pick_tile(T, 512)
    x1, h2, e_sel, rank, p_sel, counts = _merge(
        x.reshape(T, D), ya.reshape(T, vw), ys, gate_.reshape(T, W), consts, E, tm_m)

    counts = counts[0, :E]
    padded = (counts + MOE_ROWS - 1) // MOE_ROWS * MOE_ROWS
    pad_ends = jnp.cumsum(padded)
    pad_starts = pad_ends - padded
    nb = -(-T * TOP_K // MOE_ROWS) + E
    dest = pad_starts[e_sel[:, :TOP_K]] + rank[:, :TOP_K]
    block_start = jnp.arange(nb, dtype=jnp.int32) * MOE_ROWS
    block_expert = jnp.minimum(jnp.sum(pad_ends[None, :] <= block_start[:, None], axis=1), E - 1).astype(jnp.int32)
    n_used = (pad_ends[-1:] // MOE_ROWS).astype(jnp.int32)
    dest = dest.reshape(-1).astype(jnp.int32)
    tm_d = _pick_tile(T, 256)
    xs = _dispatch(dest, h2, nb * MOE_ROWS, _pick_tile(T, 512))

    FF = expert_w2.shape[2]
    w1g, w1l = _split_w1(expert_w1[0])
    b1 = expert_b1[0].reshape(E, 1, FF, 2)
    outs_e = _experts(block_expert, n_used, xs, w1g, w1l, b1[..., 0], b1[..., 1],
                      expert_w2[0].astype(BF16), expert_b2[0].reshape(E, 1, D))

    out = _combine(dest, outs_e, x1, p_sel, row(norm_final_g), tm_d)
    return out.reshape(B, S, D)
```

```python
import functools

import jax
import jax.numpy as jnp
from jax import lax
from jax.experimental import pallas as pl
from jax.experimental.pallas import tpu as pltpu

F32 = jnp.float32
BF16 = jnp.bfloat16
U32 = jnp.uint32
HIGHEST = lax.Precision.HIGHEST

LANES = 128
VMEM_LIMIT = 56 * 1024 * 1024

RMS_EPS = 1e-6
GLA_HEADS = 4
GLA_DK = 128
GLA_DV = 256
GLA_GATE_RANK = 16
GLA_TAU = 16.0
GLA_CHUNK = 64
GLA_NORM_EPS = 1e-5
RWKV_HEAD_DIM = 64
RWKV_GN_EPS = 64e-5
TOP_K = 4
SWIGLU_ALPHA = 1.702
SWIGLU_LIMIT = 7.0
MOE_ROWS = 512
ISSUE_GROUP = 4
MERGE_PARTS = 2
PROJ_PARTS = 2

def _cparams(*sem):
    return pltpu.CompilerParams(dimension_semantics=sem, vmem_limit_bytes=VMEM_LIMIT)


def _rms(xf, gain):
    return xf * lax.rsqrt(jnp.mean(xf * xf, axis=-1, keepdims=True) + RMS_EPS) * gain


def _softplus(y):
    return jnp.maximum(y, 0.0) + jnp.log1p(jnp.exp(-jnp.abs(y)))


def _sigmoid(y):
    return 1.0 / (1.0 + jnp.exp(-y))


def _dot(a, b):
    return jnp.dot(a, b, preferred_element_type=F32)


def _dot_hi(a, b):
    return jnp.dot(a, b, preferred_element_type=F32, precision=HIGHEST)


def _pack_pair(lo, hi):
    bits = lambda t: lax.bitcast_convert_type(t.astype(BF16).astype(F32), U32)
    return (bits(lo) >> 16) | bits(hi)


def _pack_rows(x):
    n = x.shape[-1] // 2
    return _pack_pair(x[:, :n], x[:, n:])


def _unpack_rows(w):
    lo = lax.bitcast_convert_type(w << 16, F32)
    hi = lax.bitcast_convert_type(w & jnp.uint32(0xFFFF0000), F32)
    return lo, hi


def _store_packed(ref, x):
    packed = _pack_rows(x)
    for c in range(ref.shape[1]):
        ref[:, c, :] = packed[:, c * LANES:(c + 1) * LANES]


def _load_packed(ref):
    return _unpack_rows(jnp.concatenate([ref[:, c, :] for c in range(ref.shape[1])], axis=1))


def _gla_proj_kernel(x_ref, g_ref, w_ref, up_ref, bias_ref, q_ref, k_ref, v_ref, og_ref, la_ref):
    qk = q_ref.shape[-1]
    vw = v_ref.shape[-1]
    h = _rms(x_ref[0], g_ref[...]).astype(BF16)
    z = _dot(h, w_ref[...])
    q_ref[0] = z[:, :qk].astype(BF16)
    k_ref[0] = z[:, qk:2 * qk].astype(BF16)
    v_ref[0] = z[:, 2 * qk:2 * qk + vw].astype(BF16)
    og_ref[0] = z[:, 2 * qk + vw:2 * qk + 2 * vw].astype(BF16)
    lr = z[:, 2 * qk + 2 * vw:].astype(BF16)
    pre = _dot(lr, up_ref[...]) + bias_ref[...]
    la_ref[0] = -_softplus(-pre) * (1.0 / GLA_TAU)


def _gla_proj(x, gain, w, up, bias, tm):
    B, S, D = x.shape
    qk = bias.shape[-1]
    vw = (w.shape[1] - 2 * qk - LANES) // 2
    tok = lambda width: pl.BlockSpec((1, tm, width), lambda b, i: (b, i, 0))
    full = lambda a: pl.BlockSpec(a.shape, lambda b, i: (0,) * a.ndim)
    return pl.pallas_call(
        _gla_proj_kernel,
        grid=(B, S // tm),
        in_specs=[tok(D), full(gain), full(w), full(up), full(bias)],
        out_specs=[tok(qk), tok(qk), tok(vw), tok(vw), tok(qk)],
        out_shape=[jax.ShapeDtypeStruct((B, S, qk), BF16), jax.ShapeDtypeStruct((B, S, qk), BF16),
                   jax.ShapeDtypeStruct((B, S, vw), BF16), jax.ShapeDtypeStruct((B, S, vw), BF16),
                   jax.ShapeDtypeStruct((B, S, qk), F32)],
        compiler_params=_cparams("parallel", "parallel"),
        name="gla_proj",
    )(x, gain, w, up, bias)


def _gla_kernel(q_ref, k_ref, v_ref, og_ref, la_ref, ng_ref, y_ref, st_ref, qa_ref, ka_ref, qi_ref, ks_ref, dec_ref,
                sc_ref):
    C = GLA_CHUNK
    tc = q_ref.shape[1]

    @pl.when(pl.program_id(1) == 0)
    def _():
        st_ref[...] = jnp.zeros_like(st_ref)

    row = lax.broadcasted_iota(jnp.int32, (tc, tc), 0)
    col = lax.broadcasted_iota(jnp.int32, (tc, tc), 1)
    same_chunk_before = jnp.logical_and(row >= col, row // C == col // C)
    b_all = _dot_hi(same_chunk_before.astype(F32), la_ref[0])
    scale = GLA_DK ** -0.5
    for c in range(tc // C):
        rows = slice(c * C, (c + 1) * C)
        b = b_all[rows]
        b_mid = b[C // 2:C // 2 + 1]
        b_end = b[C - 1:C]
        q = q_ref[0, rows, :].astype(F32) * scale
        k = k_ref[0, rows, :].astype(F32)
        qa_ref[rows, :] = (q * jnp.exp(b - b_mid)).astype(BF16)
        ka_ref[rows, :] = (k * jnp.exp(b_mid - b)).astype(BF16)
        qi_ref[rows, :] = (q * jnp.exp(b)).astype(BF16)
        ks_ref[rows, :] = (k * jnp.exp(b_end - b)).astype(BF16)
        dec_ref[c:c + 1, :] = jnp.exp(b_end)

    r_c = lax.broadcasted_iota(jnp.int32, (C, C), 0)
    c_c = lax.broadcasted_iota(jnp.int32, (C, C), 1)
    causal = r_c >= c_c
    for c in range(tc // C):
        rows = slice(c * C, (c + 1) * C)
        for h in range(GLA_HEADS):
            kc = slice(h * GLA_DK, (h + 1) * GLA_DK)
            scores = lax.dot_general(qa_ref[rows, kc], ka_ref[rows, kc], (((1,), (1,)), ((), ())),
                                     preferred_element_type=F32)
            sc_ref[rows, h * C:(h + 1) * C] = jnp.where(causal, scores, 0.0).astype(BF16)

    for c in range(tc // C):
        rows = slice(c * C, (c + 1) * C)
        for h in range(GLA_HEADS):
            kc = slice(h * GLA_DK, (h + 1) * GLA_DK)
            vc = slice(h * GLA_DV, (h + 1) * GLA_DV)
            vh = v_ref[0, rows, vc]
            st = st_ref[h]
            o = _dot(sc_ref[rows, h * C:(h + 1) * C], vh) + lax.dot_general(
                qi_ref[rows, kc], st.astype(BF16), (((1,), (1,)), ((), ())), preferred_element_type=F32)
            st_ref[h] = st * dec_ref[c:c + 1, kc] + lax.dot_general(
                vh, ks_ref[rows, kc], (((0,), (0,)), ((), ())), preferred_element_type=F32)
            o = o * lax.rsqrt(jnp.mean(o * o, axis=-1, keepdims=True) + GLA_NORM_EPS) * ng_ref[...]
            og = og_ref[0, rows, vc].astype(F32)
            y_ref[0, rows, vc] = (o * (og * _sigmoid(og))).astype(BF16)


def _gla(q, k, v, og, la, norm_g, tc):
    B, S, qk = q.shape
    vw = v.shape[-1]
    tok = lambda width: pl.BlockSpec((1, tc, width), lambda b, i: (b, i, 0))
    return pl.pallas_call(
        _gla_kernel,
        grid=(B, S // tc),
        in_specs=[tok(qk), tok(qk), tok(vw), tok(vw), tok(qk),
                  pl.BlockSpec(norm_g.shape, lambda b, i: (0, 0))],
        out_specs=tok(vw),
        out_shape=jax.ShapeDtypeStruct((B, S, vw), BF16),
        scratch_shapes=[pltpu.VMEM((GLA_HEADS, GLA_DV, GLA_DK), F32)] + [pltpu.VMEM((tc, qk), BF16)] * 4
        + [pltpu.VMEM((8, qk), F32), pltpu.VMEM((tc, GLA_HEADS * GLA_CHUNK), BF16)],
        compiler_params=_cparams("parallel", "arbitrary"),
        name="gla_chunk",
    )(q, k, v, og, la, norm_g)


def _rwkv_proj_kernel(x_ref, g_ref, w_ref, mu_ref, w0_ref, w2_ref, a0_ref, a2_ref, g2_ref,
                      rv_ref, w_out_ref, ka_ref, gate_ref, carry_ref):
    W = rv_ref.shape[-1]
    tm = x_ref.shape[1]

    @pl.when(pl.program_id(1) == 0)
    def _():
        carry_ref[...] = jnp.zeros_like(carry_ref)

    pm = tm // PROJ_PARTS
    parts = [slice(i * pm, (i + 1) * pm) for i in range(PROJ_PARTS)]
    zs = [_dot(_rms(x_ref[0, rows, :], g_ref[...]).astype(BF16), w_ref[...]) for rows in parts]
    first = lax.broadcasted_iota(jnp.int32, zs[0].shape, 0) == 0
    last_row = carry_ref[0:1, :]
    for rows, z in zip(parts, zs):
        prev = jnp.where(first, last_row, pltpu.roll(z, 1, axis=0))
        last_row = z[pm - 1:pm, :]
        u = z + mu_ref[...] * (prev - z)

        r = u[:, :W]
        k = u[:, W:2 * W]
        v = u[:, 2 * W:3 * W]
        lora = u[:, 3 * W:3 * W + LANES]
        xg = u[:, 3 * W + LANES:]
        w_log = -_softplus(-(w0_ref[...] + _dot(jnp.tanh(lora).astype(BF16), w2_ref[...]))) - 0.5
        iclr = _sigmoid(a0_ref[...] + _dot(lora.astype(BF16), a2_ref[...]))
        rv_ref[0, rows, :] = _pack_pair(r, v)
        ka_ref[0, rows, :] = _pack_pair(k, iclr)
        w_out_ref[0, rows, :] = _pack_rows(-jnp.exp(w_log))
        gate_ref[0, rows, :] = _dot(_sigmoid(xg).astype(BF16), g2_ref[...]).astype(BF16)
    carry_ref[0:1, :] = last_row


def _rwkv_proj(x, gain, w, mu, w0, w2p, a0, a2p, g2, tm):
    B, S, D = x.shape
    W = w0.shape[-1]
    tok = lambda width: pl.BlockSpec((1, tm, width), lambda b, i: (b, i, 0))
    full = lambda a: pl.BlockSpec(a.shape, lambda b, i: (0,) * a.ndim)
    consts = (gain, w, mu, w0, w2p, a0, a2p, g2)
    return pl.pallas_call(
        _rwkv_proj_kernel,
        grid=(B, S // tm),
        in_specs=[tok(D)] + [full(c) for c in consts],
        out_specs=[tok(W), tok(W // 2), tok(W), tok(W)],
        out_shape=[jax.ShapeDtypeStruct((B, S, W), U32), jax.ShapeDtypeStruct((B, S, W // 2), U32),
                   jax.ShapeDtypeStruct((B, S, W), U32), jax.ShapeDtypeStruct((B, S, W), BF16)],
        scratch_shapes=[pltpu.VMEM((8, w.shape[1]), F32)],
        compiler_params=_cparams("parallel", "arbitrary"),
        name="rwkv_proj",
    )(x, *consts)


def _rwkv_scan_kernel(rv_ref, ld_ref, kia_ref, kian_ref, kk_ref, ka_ref, rk_ref, lnw_ref, lnb_ref,
                      y_ref, st_ref, sa_ref, av_ref, bv_ref, km_ref, r_ref, v_ref, yr_ref, w_ref):
    tt, G, N, _ = rv_ref.shape
    half = N // 2

    def step_vectors(packed):
        k_raw, iclr = _unpack_rows(packed)
        kk = k_raw * kk_ref[...]
        kk = kk / jnp.maximum(jnp.sqrt(jnp.sum(kk * kk, axis=0, keepdims=True)), 1e-12)
        return -kk, kk * iclr, k_raw * (1.0 + (iclr - 1.0) * ka_ref[...])

    @pl.when(pl.program_id(0) == 0)
    def _():
        st_ref[...] = jnp.zeros_like(st_ref)
        sa_ref[...] = jnp.zeros_like(sa_ref)
        for g in range(G):
            _, bv_ref[0, g], km_ref[0, g] = step_vectors(kia_ref[0, g])

    t0 = pl.program_id(0) * tt

    def group_step(tg, carry):
        t = tg // G
        g = tg % G
        cur = (t0 + t) % 2
        last = t + 1 >= tt
        t_nx = jnp.minimum(t + 1, tt - 1)
        av_ref[...], bv_ref[1 - cur, g], km_ref[1 - cur, g] = step_vectors(
            jnp.where(last, kian_ref[0, g], kia_ref[t_nx, g]))
        r_ref[...], v_ref[...] = _unpack_rows(rv_ref[t, g])
        w_ref[...] = jnp.exp(jnp.concatenate(_unpack_rows(ld_ref[t, g]), axis=0))

        def half_step(ih, c):
            rows = slice(ih * half, (ih + 1) * half)
            sa_h = sa_ref[g, rows, :]
            vt = v_ref[rows, :]
            y = jnp.zeros((half, LANES), F32)
            sa_nx = jnp.zeros((half, LANES), F32)
            for j in range(N):
                row = slice(j, j + 1)
                s_new = (st_ref[g, j, rows, :] * w_ref[row, :] + sa_h * bv_ref[cur, g, row, :]
                         + vt * km_ref[cur, g, row, :])
                st_ref[g, j, rows, :] = s_new
                y = y + s_new * r_ref[row, :]
                sa_nx = sa_nx + s_new * av_ref[row, :]
            yr_ref[rows, :] = y
            sa_ref[g, rows, :] = sa_nx
            return c

        for ih in range(2):
            half_step(ih, 0)

        y = yr_ref[...]
        mean = jnp.mean(y, axis=0, keepdims=True)
        d = y - mean
        var = jnp.mean(d * d, axis=0, keepdims=True)
        bonus = jnp.sum(r_ref[...] * km_ref[cur, g] * rk_ref[...], axis=0, keepdims=True)
        out = d * lax.rsqrt(var + RWKV_GN_EPS) * lnw_ref[...] + lnb_ref[...] + bonus * v_ref[...]
        y_ref[t, g] = _pack_pair(out[:half], out[half:])
        return carry

    lax.fori_loop(0, tt * G, group_step, 0)


def _rwkv_scan(rv, log_decay, kia, k_k, k_a, r_k, ln_w, ln_b, tt):
    S, G, N, _ = rv.shape
    blk = pl.BlockSpec((tt, G, N, LANES), lambda i: (i, 0, 0, 0))
    pair = pl.BlockSpec((tt, G, N // 2, LANES), lambda i: (i, 0, 0, 0))
    nxt = pl.BlockSpec((1, G, N, LANES), lambda i: (jnp.minimum((i + 1) * tt, S - 1), 0, 0, 0))
    par = pl.BlockSpec((N, LANES), lambda i: (0, 0))
    tile = pltpu.VMEM((N, LANES), F32)
    per_g = pltpu.VMEM((G, N, LANES), F32)
    two_g = pltpu.VMEM((2, G, N, LANES), F32)
    return pl.pallas_call(
        _rwkv_scan_kernel,
        grid=(S // tt,),
        in_specs=[blk, pair, blk, nxt] + [par] * 5,
        out_specs=pair,
        out_shape=jax.ShapeDtypeStruct((S, G, N // 2, LANES), U32),
        scratch_shapes=[pltpu.VMEM((G, N, N, LANES), F32), per_g, tile, two_g, two_g, tile, tile, tile, tile],
        compiler_params=_cparams("arbitrary"),
        name="rwkv_scan",
    )(rv, log_decay, kia, kia, k_k, k_a, r_k, ln_w, ln_b)


def _merge_kernel(x_ref, ya_ref, ys_ref, gate_ref, g_mix_ref, wg_ref, wa_ref, wb_ref, wo_ref, g_ffn_ref,
                  rw_ref, rb_ref,
                  x1_ref, h2_ref, e_ref, rank_ref, p_ref, cnt_ref, carry_ref, *, n_experts):
    tm, D = x_ref.shape

    @pl.when(pl.program_id(0) == 0)
    def _():
        carry_ref[...] = jnp.zeros_like(carry_ref)

    hm = tm // MERGE_PARTS
    lane = lax.broadcasted_iota(jnp.int32, (hm, LANES), 1)

    def mix(rows):
        x = x_ref[rows, :]
        h = _rms(x, g_mix_ref[...]).astype(BF16)
        gates = _sigmoid(_dot(h, wg_ref[...]))
        ys = jnp.concatenate(_unpack_rows(ys_ref[rows, :]), axis=1)
        yb = (ys * gate_ref[rows, :].astype(F32)).astype(BF16)
        merged = gates[:, :D] * _dot(ya_ref[rows, :], wa_ref[...]) + gates[:, D:] * _dot(yb, wb_ref[...])
        x1 = x + _dot(merged.astype(BF16), wo_ref[...])
        x1_ref[rows, :] = x1
        h2 = _rms(x1, g_ffn_ref[...])
        _store_packed(h2_ref.at[rows], h2)
        return jnp.where(lane < n_experts, _dot_hi(h2, rw_ref[...]) + rb_ref[...], -jnp.inf)

    def route(rows, logits, seen):
        vals, idxs, hots = [], [], []
        for _ in range(TOP_K):
            m = jnp.max(logits, axis=-1, keepdims=True)
            idx = jnp.min(jnp.where(logits == m, lane, LANES), axis=-1, keepdims=True)
            hot = lane == idx
            vals.append(m)
            idxs.append(idx)
            hots.append(hot)
            logits = jnp.where(hot, -jnp.inf, logits)
        exps = [jnp.exp(val - vals[0]) for val in vals]
        denom = functools.reduce(lambda s, e: s + e, exps)
        chosen = functools.reduce(lambda s, e: s + e, [hot.astype(F32) for hot in hots])
        r_i = lax.broadcasted_iota(jnp.int32, (hm, hm), 0)
        c_i = lax.broadcasted_iota(jnp.int32, (hm, hm), 1)
        before = _dot((r_i > c_i).astype(BF16), chosen.astype(BF16)) + seen
        e_out = jnp.zeros((hm, LANES), jnp.int32)
        rank_out = jnp.zeros((hm, LANES), jnp.int32)
        p_out = jnp.zeros((hm, LANES), F32)
        for s in range(TOP_K):
            rank = jnp.sum(jnp.where(hots[s], before, 0.0), axis=-1, keepdims=True)
            e_out = jnp.where(lane == s, idxs[s], e_out)
            rank_out = jnp.where(lane == s, rank.astype(jnp.int32), rank_out)
            p_out = jnp.where(lane == s, exps[s] / denom, p_out)
        e_ref[rows, :] = e_out
        rank_ref[rows, :] = rank_out
        p_ref[rows, :] = p_out
        return seen + jnp.sum(chosen, axis=0, keepdims=True)

    parts = [slice(i * hm, (i + 1) * hm) for i in range(MERGE_PARTS)]
    logits = [mix(rows) for rows in parts]
    seen = carry_ref[0:1, :]
    for rows, lg in zip(parts, logits):
        seen = route(rows, lg, seen)
    carry_ref[0:1, :] = seen
    cnt_ref[...] = jnp.broadcast_to(seen, cnt_ref.shape).astype(jnp.int32)


def _merge(x, ya, ys, gate, consts, n_experts, tm):
    T, D = x.shape
    W = gate.shape[-1]
    tok = lambda width: pl.BlockSpec((tm, width), lambda i: (i, 0))
    full = lambda a: pl.BlockSpec(a.shape, lambda i: (0,) * a.ndim)
    return pl.pallas_call(
        functools.partial(_merge_kernel, n_experts=n_experts),
        grid=(T // tm,),
        in_specs=[tok(D), tok(ya.shape[-1]), tok(ys.shape[-1]), tok(W)] + [full(c) for c in consts],
        out_specs=[tok(D), pl.BlockSpec((tm, D // 2 // LANES, LANES), lambda i: (i, 0, 0)),
                   tok(LANES), tok(LANES), tok(LANES), pl.BlockSpec((8, LANES), lambda i: (0, 0))],
        out_shape=[jax.ShapeDtypeStruct((T, D), F32), jax.ShapeDtypeStruct((T, D // 2 // LANES, LANES), U32),
                   jax.ShapeDtypeStruct((T, LANES), jnp.int32), jax.ShapeDtypeStruct((T, LANES), jnp.int32),
                   jax.ShapeDtypeStruct((T, LANES), F32), jax.ShapeDtypeStruct((8, LANES), jnp.int32)],
        scratch_shapes=[pltpu.VMEM((8, LANES), F32)],
        compiler_params=_cparams("arbitrary"),
        name="merge_router",
    )(x, ya, ys, gate, *consts)


def _idx_copy(dest_ref, idx_ref, sem, tile, n_idx):
    slot = tile % 2
    return pltpu.make_async_copy(dest_ref.at[pl.ds(pl.multiple_of(tile * n_idx, n_idx), n_idx)],
                                 idx_ref.at[pl.ds(pl.multiple_of(slot * n_idx, n_idx), n_idx)], sem.at[slot])


def _dispatch_kernel(dest_ref, rows_ref, init_ref, xs_ref, idx_ref, idx_sem, row_sem, *, tm):
    del init_ref
    i = pl.program_id(0)
    n = pl.num_programs(0)
    n_idx = tm * TOP_K
    idx_copy = functools.partial(_idx_copy, dest_ref, idx_ref, idx_sem, n_idx=n_idx)

    @pl.when(i == 0)
    def _():
        idx_copy(i).start()

    idx_copy(i).wait()

    @pl.when(i + 1 < n)
    def _():
        idx_copy(i + 1).start()

    base = (i % 2) * n_idx

    def issue(g, c):
        r0 = g * ISSUE_GROUP
        dst = [idx_ref[base + r0 * TOP_K + k] for k in range(ISSUE_GROUP * TOP_K)]
        for k, d in enumerate(dst):
            pltpu.make_async_copy(rows_ref.at[r0 + k // TOP_K], xs_ref.at[d], row_sem).start(priority=k % 2)
        return c

    lax.fori_loop(0, tm // ISSUE_GROUP, issue, 0)

    for _ in range(TOP_K):
        pltpu.make_async_copy(rows_ref, xs_ref.at[pl.ds(0, tm)], row_sem).wait()


def _dispatch(dest, rows, n_rows, tm):
    T = rows.shape[0]
    packed = (n_rows,) + rows.shape[1:]
    any_spec = pl.BlockSpec(memory_space=pl.ANY)
    return pl.pallas_call(
        functools.partial(_dispatch_kernel, tm=tm),
        grid=(T // tm,),
        in_specs=[any_spec, pl.BlockSpec((tm,) + rows.shape[1:], lambda i: (i, 0, 0)), any_spec],
        out_specs=any_spec,
        out_shape=jax.ShapeDtypeStruct(packed, U32),
        input_output_aliases={2: 0},
        scratch_shapes=[pltpu.SMEM((2 * tm * TOP_K,), jnp.int32), pltpu.SemaphoreType.DMA((2,)),
                        pltpu.SemaphoreType.DMA],
        compiler_params=_cparams("arbitrary"),
        name="moe_dispatch",
    )(dest, rows, jnp.zeros(packed, U32))


def _split_w1_kernel(w_ref, glu_ref, lin_ref):
    cols = w_ref.shape[-1]
    src = lax.broadcasted_iota(jnp.int32, (cols, cols // 2), 0)
    dst = lax.broadcasted_iota(jnp.int32, (cols, cols // 2), 1)
    w = w_ref[0].astype(BF16)
    glu_ref[0] = _dot(w, (src == 2 * dst).astype(BF16)).astype(BF16)
    lin_ref[0] = _dot(w, (src == 2 * dst + 1).astype(BF16)).astype(BF16)


def _split_w1(w1, cols=512):
    E, D, FF2 = w1.shape
    out = jax.ShapeDtypeStruct((E, D, FF2 // 2), BF16)
    return pl.pallas_call(
        _split_w1_kernel,
        grid=(E, FF2 // cols),
        in_specs=[pl.BlockSpec((1, D, cols), lambda e, c: (e, 0, c))],
        out_specs=[pl.BlockSpec((1, D, cols // 2), lambda e, c: (e, 0, c))] * 2,
        out_shape=[out, out],
        compiler_params=_cparams("parallel", "parallel"),
        name="split_w1",
    )(w1)


def _expert_kernel(be_ref, nu_ref, x_ref, w1g_ref, w1l_ref, b1g_ref, b1l_ref, w2_ref, b2_ref, o_ref):
    del be_ref
    used = pl.program_id(0) < nu_ref[0]

    @pl.when(used)
    def _():
        lo, hi = _load_packed(x_ref)
        x = jnp.concatenate([lo.astype(BF16), hi.astype(BF16)], axis=1)
        glu = jnp.minimum(_dot(x, w1g_ref[0]) + b1g_ref[0], SWIGLU_LIMIT)
        lin = jnp.clip(_dot(x, w1l_ref[0]) + b1l_ref[0], -SWIGLU_LIMIT, SWIGLU_LIMIT)
        act = glu * _sigmoid(SWIGLU_ALPHA * glu) * (lin + 1.0)
        _store_packed(o_ref, _dot(act.astype(BF16), w2_ref[0]) + b2_ref[0])

    @pl.when(jnp.logical_not(used))
    def _():
        o_ref[...] = jnp.zeros_like(o_ref)


def _experts(block_expert, n_used, xs, w1g, w1l, b1g, b1l, w2, b2):
    R = xs.shape[0]
    D, FF = w1g.shape[1:]
    nb = R // MOE_ROWS
    per_e = lambda shape: pl.BlockSpec((1,) + shape, lambda i, be, nu: (be[i], 0, 0))
    rows = pl.BlockSpec((MOE_ROWS,) + xs.shape[1:], lambda i, be, nu: (i, 0, 0))
    return pl.pallas_call(
        _expert_kernel,
        grid_spec=pltpu.PrefetchScalarGridSpec(
            num_scalar_prefetch=2,
            grid=(nb,),
            in_specs=[rows, per_e((D, FF)), per_e((D, FF)), per_e((1, FF)), per_e((1, FF)),
                      per_e((FF, D)), per_e((1, D))],
            out_specs=rows,
        ),
        out_shape=jax.ShapeDtypeStruct(xs.shape, U32),
        compiler_params=_cparams("arbitrary"),
        name="expert_ffn",
    )(block_expert, n_used, xs, w1g, w1l, b1g, b1l, w2, b2)


def _combine_kernel(dest_ref, rows_ref, x1_ref, p_ref, g_ref, o_ref, buf_ref, idx_ref, idx_sem, row_sem):
    tm, D = x1_ref.shape
    i = pl.program_id(0)
    n = pl.num_programs(0)
    n_idx = tm * TOP_K
    idx_copy = functools.partial(_idx_copy, dest_ref, idx_ref, idx_sem, n_idx=n_idx)

    def gather(tile):
        slot = tile % 2
        base = slot * n_idx

        def issue(g, c):
            r0 = g * ISSUE_GROUP
            src = [idx_ref[base + r0 * TOP_K + k] for k in range(ISSUE_GROUP * TOP_K)]
            for k, d in enumerate(src):
                pltpu.make_async_copy(rows_ref.at[d], buf_ref.at[slot, k % TOP_K, r0 + k // TOP_K],
                                      row_sem.at[slot]).start(priority=k % 2)
            return c

        lax.fori_loop(0, tm // ISSUE_GROUP, issue, 0)

    @pl.when(i == 0)
    def _():
        idx_copy(i).start()
        idx_copy(i).wait()
        gather(i)

        @pl.when(n > 1)
        def _():
            idx_copy(i + 1).start()

    @pl.when(i + 1 < n)
    def _():
        idx_copy(i + 1).wait()
        gather(i + 1)

    @pl.when(i + 2 < n)
    def _():
        idx_copy(i + 2).start()

    slot = i % 2
    for s in range(TOP_K):
        pltpu.make_async_copy(rows_ref.at[pl.ds(0, tm)], buf_ref.at[slot, s], row_sem.at[slot]).wait()

    half = D // 2
    x1 = x1_ref[...]
    acc_lo, acc_hi = x1[:, :half], x1[:, half:]
    p = p_ref[...]
    for s in range(TOP_K):
        lo, hi = _load_packed(buf_ref.at[slot, s])
        acc_lo = acc_lo + p[:, s:s + 1] * lo
        acc_hi = acc_hi + p[:, s:s + 1] * hi
    ms = (jnp.sum(acc_lo * acc_lo, axis=-1, keepdims=True) + jnp.sum(acc_hi * acc_hi, axis=-1, keepdims=True)) / D
    scale = lax.rsqrt(ms + RMS_EPS)
    o_ref[:, :half] = acc_lo * scale * g_ref[:, :half]
    o_ref[:, half:] = acc_hi * scale * g_ref[:, half:]


def _combine(dest, rows, x1, p, gain, tm):
    T, D = x1.shape
    any_spec = pl.BlockSpec(memory_space=pl.ANY)
    return pl.pallas_call(
        _combine_kernel,
        grid=(T // tm,),
        in_specs=[any_spec, any_spec, pl.BlockSpec((tm, D), lambda i: (i, 0)),
                  pl.BlockSpec((tm, LANES), lambda i: (i, 0)), pl.BlockSpec(gain.shape, lambda i: (0, 0))],
        out_specs=pl.BlockSpec((tm, D), lambda i: (i, 0)),
        out_shape=jax.ShapeDtypeStruct((T, D), F32),
        scratch_shapes=[pltpu.VMEM((2, TOP_K, tm) + rows.shape[1:], U32), pltpu.SMEM((2 * tm * TOP_K,), jnp.int32),
                        pltpu.SemaphoreType.DMA((2,)), pltpu.SemaphoreType.DMA((2,))],
        compiler_params=_cparams("arbitrary"),
        name="moe_combine_norm",
    )(dest, rows, x1, p, gain)


def _pick_tile(n, pref):
    t = min(n, pref)
    assert n % t == 0, (n, t)
    return t


def kernel(x, norm_mix_g, w_in, gla_gate_up, gla_gate_bias, gla_norm_g, rwkv_mu, rwkv_w0, rwkv_w2, rwkv_a0, rwkv_a2, rwkv_g2, rwkv_k_k, rwkv_k_a, rwkv_r_k, rwkv_ln_w, rwkv_ln_b, w_branch_a, w_branch_b, w_out, norm_ffn_g, router_w, router_b, expert_w1, expert_b1, expert_w2, expert_b2, norm_final_g):
    assert norm_mix_g.shape[0] == 1, "single-layer block"
    B, S, D = x.shape
    T = B * S
    qk = gla_gate_bias.shape[-1]
    vw = GLA_HEADS * GLA_DV
    W = rwkv_w0.shape[-1]
    H = W // RWKV_HEAD_DIM
    E = router_w.shape[-1]
    assert qk == GLA_HEADS * GLA_DK and B * H % LANES == 0 and E <= LANES
    row = lambda a: a.reshape(1, -1)

    wi = w_in[0]
    gla_cols = 2 * qk + vw + GLA_GATE_RANK + vw
    rwkv_cols = rwkv_mu.shape[-1]
    o_q, o_k, o_v, o_lr, o_og = 0, qk, 2 * qk, 2 * qk + vw, 2 * qk + vw + GLA_GATE_RANK
    w_gla = jnp.concatenate(
        [wi[:, o_q:o_lr], wi[:, o_og:gla_cols], wi[:, o_lr:o_og],
         jnp.zeros((D, LANES - GLA_GATE_RANK), F32)], axis=1).astype(BF16)
    up = jnp.concatenate([gla_gate_up[0], jnp.zeros((LANES - GLA_GATE_RANK, qk), F32)], axis=0).astype(BF16)
    w_rwkv = wi[:, gla_cols:gla_cols + rwkv_cols].astype(BF16)
    w_gate = wi[:, gla_cols + rwkv_cols:].astype(BF16)
    n_w, n_a = rwkv_w2.shape[1], rwkv_a2.shape[1]
    assert n_w + n_a == LANES
    w2p = jnp.concatenate([rwkv_w2[0], jnp.zeros((n_a, W), F32)], axis=0).astype(BF16)
    a2p = jnp.concatenate([jnp.zeros((n_w, W), F32), rwkv_a2[0]], axis=0).astype(BF16)

    tm_gla = _pick_tile(S, 512)
    q, k, v, og, la = _gla_proj(x, row(norm_mix_g), w_gla, up, row(gla_gate_bias), tm_gla)
    ya = _gla(q, k, v, og, la, row(gla_norm_g), _pick_tile(S, 256))

    tm_r = _pick_tile(S, 512)
    hn = RWKV_HEAD_DIM // 2
    col = jnp.arange(W)
    scan_order = (col % (W // 2)) // hn * RWKV_HEAD_DIM + col // (W // 2) * hn + col % hn
    rv_, ld_, kia_, gate_ = _rwkv_proj(x, row(norm_mix_g), w_rwkv, row(rwkv_mu), row(rwkv_w0)[:, scan_order],
                                       w2p[:, scan_order], row(rwkv_a0), a2p,
                                       rwkv_g2[0][:, scan_order].astype(BF16), tm_r)
    G = B * H // LANES
    bg = B // G
    to_scan = lambda t, n: t.reshape(G, bg, S, H, n).transpose(2, 0, 4, 1, 3).reshape(S, G, n, LANES)
    par = lambda p: jnp.tile(p.reshape(H, RWKV_HEAD_DIM).T, (1, bg))
    ys = _rwkv_scan(to_scan(rv_, RWKV_HEAD_DIM), to_scan(ld_, hn), to_scan(kia_, RWKV_HEAD_DIM),
                    par(rwkv_k_k), par(rwkv_k_a), par(rwkv_r_k), par(rwkv_ln_w), par(rwkv_ln_b), _pick_tile(S, 16))
    ys = ys.reshape(S, G, hn, bg, H).transpose(1, 3, 0, 4, 2).reshape(T, W // 2)

    rw = jnp.concatenate([router_w[0], jnp.zeros((D, LANES - E), F32)], axis=1)
    rb = jnp.concatenate([router_b[0], jnp.zeros((LANES - E,), F32)]).reshape(1, LANES)
    consts = (row(norm_mix_g), w_gate, w_branch_a[0].astype(BF16), w_branch_b[0][scan_order, :].astype(BF16),
              w_out[0].astype(BF16), row(norm_ffn_g), rw, rb)
    tm_m = _pick_tile(T, 512)
    x1, h2, e_sel, rank, p_sel, counts = _merge(
        x.reshape(T, D), ya.reshape(T, vw), ys, gate_.reshape(T, W), consts, E, tm_m)

    counts = counts[0, :E]
    padded = (counts + MOE_ROWS - 1) // MOE_ROWS * MOE_ROWS
    pad_ends = jnp.cumsum(padded)
    pad_starts = pad_ends - padded
    nb = -(-T * TOP_K // MOE_ROWS) + E
    dest = pad_starts[e_sel[:, :TOP_K]] + rank[:, :TOP_K]
    block_start = jnp.arange(nb, dtype=jnp.int32) * MOE_ROWS
    block_expert = jnp.minimum(jnp.sum(pad_ends[None, :] <= block_start[:, None], axis=1), E - 1).astype(jnp.int32)
    n_used = (pad_ends[-1:] // MOE_ROWS).astype(jnp.int32)
    dest = dest.reshape(-1).astype(jnp.int32)
    tm_d = _pick_tile(T, 256)
    xs = _dispatch(dest, h2, nb * MOE_ROWS, _pick_tile(T, 512))

    FF = expert_w2.shape[2]
    w1g, w1l = _split_w1(expert_w1[0])
    b1 = expert_b1[0].reshape(E, 1, FF, 2)
    outs_e = _experts(block_expert, n_used, xs, w1g, w1l, b1[..., 0], b1[..., 1],
                      expert_w2[0].astype(BF16), expert_b2[0].reshape(E, 1, D))

    out = _combine(dest, outs_e, x1, p_sel, row(norm_final_g), tm_d)
    return out.reshape(B, S, D)
```

```python
import functools

import jax
import jax.numpy as jnp
from jax import lax
from jax.experimental import pallas as pl
from jax.experimental.pallas import tpu as pltpu

F32 = jnp.float32
BF16 = jnp.bfloat16
U32 = jnp.uint32
HIGHEST = lax.Precision.HIGHEST

LANES = 128
VMEM_LIMIT = 56 * 1024 * 1024

RMS_EPS = 1e-6
GLA_HEADS = 4
GLA_DK = 128
GLA_DV = 256
GLA_GATE_RANK = 16
GLA_TAU = 16.0
GLA_CHUNK = 64
GLA_NORM_EPS = 1e-5
RWKV_HEAD_DIM = 64
RWKV_GN_EPS = 64e-5
TOP_K = 4
SWIGLU_ALPHA = 1.702
SWIGLU_LIMIT = 7.0
MOE_ROWS = 512
ISSUE_GROUP = 4
MERGE_PARTS = 2
PROJ_PARTS = 2

def _cparams(*sem):
    return pltpu.CompilerParams(dimension_semantics=sem, vmem_limit_bytes=VMEM_LIMIT)


def _rms(xf, gain):
    return xf * lax.rsqrt(jnp.mean(xf * xf, axis=-1, keepdims=True) + RMS_EPS) * gain


def _softplus(y):
    return jnp.maximum(y, 0.0) + jnp.log1p(jnp.exp(-jnp.abs(y)))


def _sigmoid(y):
    return 1.0 / (1.0 + jnp.exp(-y))


def _dot(a, b):
    return jnp.dot(a, b, preferred_element_type=F32)


def _dot_hi(a, b):
    return jnp.dot(a, b, preferred_element_type=F32, precision=HIGHEST)


def _pack_pair(lo, hi):
    bits = lambda t: lax.bitcast_convert_type(t.astype(BF16).astype(F32), U32)
    return (bits(lo) >> 16) | bits(hi)


def _pack_rows(x):
    n = x.shape[-1] // 2
    return _pack_pair(x[:, :n], x[:, n:])


def _unpack_rows(w):
    lo = lax.bitcast_convert_type(w << 16, F32)
    hi = lax.bitcast_convert_type(w & jnp.uint32(0xFFFF0000), F32)
    return lo, hi


def _store_packed(ref, x):
    packed = _pack_rows(x)
    for c in range(ref.shape[1]):
        ref[:, c, :] = packed[:, c * LANES:(c + 1) * LANES]


def _load_packed(ref):
    return _unpack_rows(jnp.concatenate([ref[:, c, :] for c in range(ref.shape[1])], axis=1))


def _gla_proj_kernel(x_ref, g_ref, w_ref, up_ref, bias_ref, q_ref, k_ref, v_ref, og_ref, la_ref):
    qk = q_ref.shape[-1]
    vw = v_ref.shape[-1]
    pm = x_ref.shape[1] // PROJ_PARTS
    parts = [slice(i * pm, (i + 1) * pm) for i in range(PROJ_PARTS)]
    zs = [_dot(_rms(x_ref[0, rows, :], g_ref[...]).astype(BF16), w_ref[...]) for rows in parts]
    for rows, z in zip(parts, zs):
        q_ref[0, rows, :] = z[:, :qk].astype(BF16)
        k_ref[0, rows, :] = z[:, qk:2 * qk].astype(BF16)
        v_ref[0, rows, :] = z[:, 2 * qk:2 * qk + vw].astype(BF16)
        og_ref[0, rows, :] = z[:, 2 * qk + vw:2 * qk + 2 * vw].astype(BF16)
        lr = z[:, 2 * qk + 2 * vw:].astype(BF16)
        pre = _dot(lr, up_ref[...]) + bias_ref[...]
        la_ref[0, rows, :] = -_softplus(-pre) * (1.0 / GLA_TAU)


def _gla_proj(x, gain, w, up, bias, tm):
    B, S, D = x.shape
    qk = bias.shape[-1]
    vw = (w.shape[1] - 2 * qk - LANES) // 2
    tok = lambda width: pl.BlockSpec((1, tm, width), lambda b, i: (b, i, 0))
    full = lambda a: pl.BlockSpec(a.shape, lambda b, i: (0,) * a.ndim)
    return pl.pallas_call(
        _gla_proj_kernel,
        grid=(B, S // tm),
        in_specs=[tok(D), full(gain), full(w), full(up), full(bias)],
        out_specs=[tok(qk), tok(qk), tok(vw), tok(vw), tok(qk)],
        out_shape=[jax.ShapeDtypeStruct((B, S, qk), BF16), jax.ShapeDtypeStruct((B, S, qk), BF16),
                   jax.ShapeDtypeStruct((B, S, vw), BF16), jax.ShapeDtypeStruct((B, S, vw), BF16),
                   jax.ShapeDtypeStruct((B, S, qk), F32)],
        compiler_params=_cparams("parallel", "parallel"),
        name="gla_proj",
    )(x, gain, w, up, bias)


def _gla_kernel(q_ref, k_ref, v_ref, og_ref, la_ref, ng_ref, y_ref, st_ref, qa_ref, ka_ref, qi_ref, ks_ref, dec_ref,
                sc_ref):
    C = GLA_CHUNK
    tc = q_ref.shape[1]

    @pl.when(pl.program_id(1) == 0)
    def _():
        st_ref[...] = jnp.zeros_like(st_ref)

    row = lax.broadcasted_iota(jnp.int32, (tc, tc), 0)
    col = lax.broadcasted_iota(jnp.int32, (tc, tc), 1)
    same_chunk_before = jnp.logical_and(row >= col, row // C == col // C)
    b_all = _dot_hi(same_chunk_before.astype(F32), la_ref[0])
    scale = GLA_DK ** -0.5
    for c in range(tc // C):
        rows = slice(c * C, (c + 1) * C)
        b = b_all[rows]
        b_mid = b[C // 2:C // 2 + 1]
        b_end = b[C - 1:C]
        q = q_ref[0, rows, :].astype(F32) * scale
        k = k_ref[0, rows, :].astype(F32)
        qa_ref[rows, :] = (q * jnp.exp(b - b_mid)).astype(BF16)
        ka_ref[rows, :] = (k * jnp.exp(b_mid - b)).astype(BF16)
        qi_ref[rows, :] = (q * jnp.exp(b)).astype(BF16)
        ks_ref[rows, :] = (k * jnp.exp(b_end - b)).astype(BF16)
        dec_ref[c:c + 1, :] = jnp.exp(b_end)

    r_c = lax.broadcasted_iota(jnp.int32, (C, C), 0)
    c_c = lax.broadcasted_iota(jnp.int32, (C, C), 1)
    causal = r_c >= c_c
    for c in range(tc // C):
        rows = slice(c * C, (c + 1) * C)
        for h in range(GLA_HEADS):
            kc = slice(h * GLA_DK, (h + 1) * GLA_DK)
            scores = lax.dot_general(qa_ref[rows, kc], ka_ref[rows, kc], (((1,), (1,)), ((), ())),
                                     preferred_element_type=F32)
            sc_ref[rows, h * C:(h + 1) * C] = jnp.where(causal, scores, 0.0).astype(BF16)

    for c in range(tc // C):
        rows = slice(c * C, (c + 1) * C)
        for h in range(GLA_HEADS):
            kc = slice(h * GLA_DK, (h + 1) * GLA_DK)
            vc = slice(h * GLA_DV, (h + 1) * GLA_DV)
            vh = v_ref[0, rows, vc]
            st = st_ref[h]
            o = _dot(sc_ref[rows, h * C:(h + 1) * C], vh) + lax.dot_general(
                qi_ref[rows, kc], st.astype(BF16), (((1,), (1,)), ((), ())), preferred_element_type=F32)
            st_ref[h] = st * dec_ref[c:c + 1, kc] + lax.dot_general(
                vh, ks_ref[rows, kc], (((0,), (0,)), ((), ())), preferred_element_type=F32)
            o = o * lax.rsqrt(jnp.mean(o * o, axis=-1, keepdims=True) + GLA_NORM_EPS) * ng_ref[...]
            og = og_ref[0, rows, vc].astype(F32)
            y_ref[0, rows, vc] = (o * (og * _sigmoid(og))).astype(BF16)


def _gla(q, k, v, og, la, norm_g, tc):
    B, S, qk = q.shape
    vw = v.shape[-1]
    tok = lambda width: pl.BlockSpec((1, tc, width), lambda b, i: (b, i, 0))
    return pl.pallas_call(
        _gla_kernel,
        grid=(B, S // tc),
        in_specs=[tok(qk), tok(qk), tok(vw), tok(vw), tok(qk),
                  pl.BlockSpec(norm_g.shape, lambda b, i: (0, 0))],
        out_specs=tok(vw),
        out_shape=jax.ShapeDtypeStruct((B, S, vw), BF16),
        scratch_shapes=[pltpu.VMEM((GLA_HEADS, GLA_DV, GLA_DK), F32)] + [pltpu.VMEM((tc, qk), BF16)] * 4
        + [pltpu.VMEM((8, qk), F32), pltpu.VMEM((tc, GLA_HEADS * GLA_CHUNK), BF16)],
        compiler_params=_cparams("parallel", "arbitrary"),
        name="gla_chunk",
    )(q, k, v, og, la, norm_g)


def _rwkv_proj_kernel(x_ref, g_ref, w_ref, mu_ref, w0_ref, w2_ref, a0_ref, a2_ref, g2_ref,
                      rv_ref, w_out_ref, ka_ref, gate_ref, carry_ref):
    W = rv_ref.shape[-1]
    tm = x_ref.shape[1]

    @pl.when(pl.program_id(1) == 0)
    def _():
        carry_ref[...] = jnp.zeros_like(carry_ref)

    pm = tm // PROJ_PARTS
    parts = [slice(i * pm, (i + 1) * pm) for i in range(PROJ_PARTS)]
    zs = [_dot(_rms(x_ref[0, rows, :], g_ref[...]).astype(BF16), w_ref[...]) for rows in parts]
    first = lax.broadcasted_iota(jnp.int32, zs[0].shape, 0) == 0
    last_row = carry_ref[0:1, :]
    for rows, z in zip(parts, zs):
        prev = jnp.where(first, last_row, pltpu.roll(z, 1, axis=0))
        last_row = z[pm - 1:pm, :]
        u = z + mu_ref[...] * (prev - z)

        r = u[:, :W]
        k = u[:, W:2 * W]
        v = u[:, 2 * W:3 * W]
        lora = u[:, 3 * W:3 * W + LANES]
        xg = u[:, 3 * W + LANES:]
        w_log = -_softplus(-(w0_ref[...] + _dot(jnp.tanh(lora).astype(BF16), w2_ref[...]))) - 0.5
        iclr = _sigmoid(a0_ref[...] + _dot(lora.astype(BF16), a2_ref[...]))
        rv_ref[0, rows, :] = _pack_pair(r, v)
        ka_ref[0, rows, :] = _pack_pair(k, iclr)
        w_out_ref[0, rows, :] = _pack_rows(-jnp.exp(w_log))
        gate_ref[0, rows, :] = _dot(_sigmoid(xg).astype(BF16), g2_ref[...]).astype(BF16)
    carry_ref[0:1, :] = last_row


def _rwkv_proj(x, gain, w, mu, w0, w2p, a0, a2p, g2, tm):
    B, S, D = x.shape
    W = w0.shape[-1]
    tok = lambda width: pl.BlockSpec((1, tm, width), lambda b, i: (b, i, 0))
    full = lambda a: pl.BlockSpec(a.shape, lambda b, i: (0,) * a.ndim)
    consts = (gain, w, mu, w0, w2p, a0, a2p, g2)
    return pl.pallas_call(
        _rwkv_proj_kernel,
        grid=(B, S // tm),
        in_specs=[tok(D)] + [full(c) for c in consts],
        out_specs=[tok(W), tok(W // 2), tok(W), tok(W)],
        out_shape=[jax.ShapeDtypeStruct((B, S, W), U32), jax.ShapeDtypeStruct((B, S, W // 2), U32),
                   jax.ShapeDtypeStruct((B, S, W), U32), jax.ShapeDtypeStruct((B, S, W), BF16)],
        scratch_shapes=[pltpu.VMEM((8, w.shape[1]), F32)],
        compiler_params=_cparams("parallel", "arbitrary"),
        name="rwkv_proj",
    )(x, *consts)


def _rwkv_scan_kernel(rv_ref, ld_ref, kia_ref, kian_ref, kk_ref, ka_ref, rk_ref, lnw_ref, lnb_ref,
                      y_ref, st_ref, sa_ref, av_ref, bv_ref, km_ref, r_ref, v_ref, yr_ref, w_ref):
    tt, G, N, _ = rv_ref.shape
    half = N // 2

    def step_vectors(packed):
        k_raw, iclr = _unpack_rows(packed)
        kk = k_raw * kk_ref[...]
        kk = kk / jnp.maximum(jnp.sqrt(jnp.sum(kk * kk, axis=0, keepdims=True)), 1e-12)
        return -kk, kk * iclr, k_raw * (1.0 + (iclr - 1.0) * ka_ref[...])

    @pl.when(pl.program_id(0) == 0)
    def _():
        st_ref[...] = jnp.zeros_like(st_ref)
        sa_ref[...] = jnp.zeros_like(sa_ref)
        for g in range(G):
            _, bv_ref[0, g], km_ref[0, g] = step_vectors(kia_ref[0, g])

    t0 = pl.program_id(0) * tt

    def group_step(tg, carry):
        t = tg // G
        g = tg % G
        cur = (t0 + t) % 2
        last = t + 1 >= tt
        t_nx = jnp.minimum(t + 1, tt - 1)
        av_ref[...], bv_ref[1 - cur, g], km_ref[1 - cur, g] = step_vectors(
            jnp.where(last, kian_ref[0, g], kia_ref[t_nx, g]))
        r_ref[...], v_ref[...] = _unpack_rows(rv_ref[t, g])
        w_ref[...] = jnp.exp(jnp.concatenate(_unpack_rows(ld_ref[t, g]), axis=0))

        def half_step(ih, c):
            rows = slice(ih * half, (ih + 1) * half)
            sa_h = sa_ref[g, rows, :]
            vt = v_ref[rows, :]
            y = jnp.zeros((half, LANES), F32)
            sa_nx = jnp.zeros((half, LANES), F32)
            for j in range(N):
                row = slice(j, j + 1)
                s_new = (st_ref[g, j, rows, :] * w_ref[row, :] + sa_h * bv_ref[cur, g, row, :]
                         + vt * km_ref[cur, g, row, :])
                st_ref[g, j, rows, :] = s_new
                y = y + s_new * r_ref[row, :]
                sa_nx = sa_nx + s_new * av_ref[row, :]
            yr_ref[rows, :] = y
            sa_ref[g, rows, :] = sa_nx
            return c

        for ih in range(2):
            half_step(ih, 0)

        y = yr_ref[...]
        mean = jnp.mean(y, axis=0, keepdims=True)
        d = y - mean
        var = jnp.mean(d * d, axis=0, keepdims=True)
        bonus = jnp.sum(r_ref[...] * km_ref[cur, g] * rk_ref[...], axis=0, keepdims=True)
        out = d * lax.rsqrt(var + RWKV_GN_EPS) * lnw_ref[...] + lnb_ref[...] + bonus * v_ref[...]
        y_ref[t, g] = _pack_pair(out[:half], out[half:])
        return carry

    lax.fori_loop(0, tt * G, group_step, 0)


def _rwkv_scan(rv, log_decay, kia, k_k, k_a, r_k, ln_w, ln_b, tt):
    S, G, N, _ = rv.shape
    blk = pl.BlockSpec((tt, G, N, LANES), lambda i: (i, 0, 0, 0))
    pair = pl.BlockSpec((tt, G, N // 2, LANES), lambda i: (i, 0, 0, 0))
    nxt = pl.BlockSpec((1, G, N, LANES), lambda i: (jnp.minimum((i + 1) * tt, S - 1), 0, 0, 0))
    par = pl.BlockSpec((N, LANES), lambda i: (0, 0))
    tile = pltpu.VMEM((N, LANES), F32)
    per_g = pltpu.VMEM((G, N, LANES), F32)
    two_g = pltpu.VMEM((2, G, N, LANES), F32)
    return pl.pallas_call(
        _rwkv_scan_kernel,
        grid=(S // tt,),
        in_specs=[blk, pair, blk, nxt] + [par] * 5,
        out_specs=pair,
        out_shape=jax.ShapeDtypeStruct((S, G, N // 2, LANES), U32),
        scratch_shapes=[pltpu.VMEM((G, N, N, LANES), F32), per_g, tile, two_g, two_g, tile, tile, tile, tile],
        compiler_params=_cparams("arbitrary"),
        name="rwkv_scan",
    )(rv, log_decay, kia, kia, k_k, k_a, r_k, ln_w, ln_b)


def _merge_kernel(x_ref, ya_ref, ys_ref, gate_ref, g_mix_ref, wg_ref, wa_ref, wb_ref, wo_ref, g_ffn_ref,
                  rw_ref, rb_ref,
                  x1_ref, h2_ref, e_ref, rank_ref, p_ref, cnt_ref, carry_ref, *, n_experts):
    tm, D = x_ref.shape

    @pl.when(pl.program_id(0) == 0)
    def _():
        carry_ref[...] = jnp.zeros_like(carry_ref)

    hm = tm // MERGE_PARTS
    lane = lax.broadcasted_iota(jnp.int32, (hm, LANES), 1)

    def mix(rows):
        x = x_ref[rows, :]
        h = _rms(x, g_mix_ref[...]).astype(BF16)
        gates = _sigmoid(_dot(h, wg_ref[...]))
        ys = jnp.concatenate(_unpack_rows(ys_ref[rows, :]), axis=1)
        yb = (ys * gate_ref[rows, :].astype(F32)).astype(BF16)
        merged = gates[:, :D] * _dot(ya_ref[rows, :], wa_ref[...]) + gates[:, D:] * _dot(yb, wb_ref[...])
        x1 = x + _dot(merged.astype(BF16), wo_ref[...])
        x1_ref[rows, :] = x1
        h2 = _rms(x1, g_ffn_ref[...])
        _store_packed(h2_ref.at[rows], h2)
        return jnp.where(lane < n_experts, _dot_hi(h2, rw_ref[...]) + rb_ref[...], -jnp.inf)

    def route(rows, logits, seen):
        vals, idxs, hots = [], [], []
        for _ in range(TOP_K):
            m = jnp.max(logits, axis=-1, keepdims=True)
            idx = jnp.min(jnp.where(logits == m, lane, LANES), axis=-1, keepdims=True)
            hot = lane == idx
            vals.append(m)
            idxs.append(idx)
            hots.append(hot)
            logits = jnp.where(hot, -jnp.inf, logits)
        exps = [jnp.exp(val - vals[0]) for val in vals]
        denom = functools.reduce(lambda s, e: s + e, exps)
        chosen = functools.reduce(lambda s, e: s + e, [hot.astype(F32) for hot in hots])
        r_i = lax.broadcasted_iota(jnp.int32, (hm, hm), 0)
        c_i = lax.broadcasted_iota(jnp.int32, (hm, hm), 1)
        before = _dot((r_i > c_i).astype(BF16), chosen.astype(BF16)) + seen
        e_out = jnp.zeros((hm, LANES), jnp.int32)
        rank_out = jnp.zeros((hm, LANES), jnp.int32)
        p_out = jnp.zeros((hm, LANES), F32)
        for s in range(TOP_K):
            rank = jnp.sum(jnp.where(hots[s], before, 0.0), axis=-1, keepdims=True)
            e_out = jnp.where(lane == s, idxs[s], e_out)
            rank_out = jnp.where(lane == s, rank.astype(jnp.int32), rank_out)
            p_out = jnp.where(lane == s, exps[s] / denom, p_out)
        e_ref[rows, :] = e_out
        rank_ref[rows, :] = rank_out
        p_ref[rows, :] = p_out
        return seen + jnp.sum(chosen, axis=0, keepdims=True)

    parts = [slice(i * hm, (i + 1) * hm) for i in range(MERGE_PARTS)]
    logits = [mix(rows) for rows in parts]
    seen = carry_ref[0:1, :]
    for rows, lg in zip(parts, logits):
        seen = route(rows, lg, seen)
    carry_ref[0:1, :] = seen
    cnt_ref[...] = jnp.broadcast_to(seen, cnt_ref.shape).astype(jnp.int32)


def _merge(x, ya, ys, gate, consts, n_experts, tm):
    T, D = x.shape
    W = gate.shape[-1]
    tok = lambda width: pl.BlockSpec((tm, width), lambda i: (i, 0))
    full = lambda a: pl.BlockSpec(a.shape, lambda i: (0,) * a.ndim)
    return pl.pallas_call(
        functools.partial(_merge_kernel, n_experts=n_experts),
        grid=(T // tm,),
        in_specs=[tok(D), tok(ya.shape[-1]), tok(ys.shape[-1]), tok(W)] + [full(c) for c in consts],
        out_specs=[tok(D), pl.BlockSpec((tm, D // 2 // LANES, LANES), lambda i: (i, 0, 0)),
                   tok(LANES), tok(LANES), tok(LANES), pl.BlockSpec((8, LANES), lambda i: (0, 0))],
        out_shape=[jax.ShapeDtypeStruct((T, D), F32), jax.ShapeDtypeStruct((T, D // 2 // LANES, LANES), U32),
                   jax.ShapeDtypeStruct((T, LANES), jnp.int32), jax.ShapeDtypeStruct((T, LANES), jnp.int32),
                   jax.ShapeDtypeStruct((T, LANES), F32), jax.ShapeDtypeStruct((8, LANES), jnp.int32)],
        scratch_shapes=[pltpu.VMEM((8, LANES), F32)],
        compiler_params=_cparams("arbitrary"),
        name="merge_router",
    )(x, ya, ys, gate, *consts)


def _idx_copy(dest_ref, idx_ref, sem, tile, n_idx):
    slot = tile % 2
    return pltpu.make_async_copy(dest_ref.at[pl.ds(pl.multiple_of(tile * n_idx, n_idx), n_idx)],
                                 idx_ref.at[pl.ds(pl.multiple_of(slot * n_idx, n_idx), n_idx)], sem.at[slot])


def _dispatch_kernel(dest_ref, rows_ref, init_ref, xs_ref, idx_ref, idx_sem, row_sem, *, tm):
    del init_ref
    i = pl.program_id(0)
    n = pl.num_programs(0)
    n_idx = tm * TOP_K
    idx_copy = functools.partial(_idx_copy, dest_ref, idx_ref, idx_sem, n_idx=n_idx)

    @pl.when(i == 0)
    def _():
        idx_copy(i).start()

    idx_copy(i).wait()

    @pl.when(i + 1 < n)
    def _():
        idx_copy(i + 1).start()

    base = (i % 2) * n_idx

    def issue(g, c):
        r0 = g * ISSUE_GROUP
        dst = [idx_ref[base + r0 * TOP_K + k] for k in range(ISSUE_GROUP * TOP_K)]
        for k, d in enumerate(dst):
            pltpu.make_async_copy(rows_ref.at[r0 + k // TOP_K], xs_ref.at[d], row_sem).start(priority=k % 2)
        return c

    lax.fori_loop(0, tm // ISSUE_GROUP, issue, 0)

    for _ in range(TOP_K):
        pltpu.make_async_copy(rows_ref, xs_ref.at[pl.ds(0, tm)], row_sem).wait()


def _dispatch(dest, rows, n_rows, tm):
    T = rows.shape[0]
    packed = (n_rows,) + rows.shape[1:]
    any_spec = pl.BlockSpec(memory_space=pl.ANY)
    return pl.pallas_call(
        functools.partial(_dispatch_kernel, tm=tm),
        grid=(T // tm,),
        in_specs=[any_spec, pl.BlockSpec((tm,) + rows.shape[1:], lambda i: (i, 0, 0)), any_spec],
        out_specs=any_spec,
        out_shape=jax.ShapeDtypeStruct(packed, U32),
        input_output_aliases={2: 0},
        scratch_shapes=[pltpu.SMEM((2 * tm * TOP_K,), jnp.int32), pltpu.SemaphoreType.DMA((2,)),
                        pltpu.SemaphoreType.DMA],
        compiler_params=_cparams("arbitrary"),
        name="moe_dispatch",
    )(dest, rows, jnp.zeros(packed, U32))


def _split_w1_kernel(w_ref, glu_ref, lin_ref):
    cols = w_ref.shape[-1]
    src = lax.broadcasted_iota(jnp.int32, (cols, cols // 2), 0)
    dst = lax.broadcasted_iota(jnp.int32, (cols, cols // 2), 1)
    w = w_ref[0].astype(BF16)
    glu_ref[0] = _dot(w, (src == 2 * dst).astype(BF16)).astype(BF16)
    lin_ref[0] = _dot(w, (src == 2 * dst + 1).astype(BF16)).astype(BF16)


def _split_w1(w1, cols=512):
    E, D, FF2 = w1.shape
    out = jax.ShapeDtypeStruct((E, D, FF2 // 2), BF16)
    return pl.pallas_call(
        _split_w1_kernel,
        grid=(E, FF2 // cols),
        in_specs=[pl.BlockSpec((1, D, cols), lambda e, c: (e, 0, c))],
        out_specs=[pl.BlockSpec((1, D, cols // 2), lambda e, c: (e, 0, c))] * 2,
        out_shape=[out, out],
        compiler_params=_cparams("parallel", "parallel"),
        name="split_w1",
    )(w1)


def _expert_kernel(be_ref, nu_ref, x_ref, w1g_ref, w1l_ref, b1g_ref, b1l_ref, w2_ref, b2_ref, o_ref):
    del be_ref
    used = pl.program_id(0) < nu_ref[0]

    @pl.when(used)
    def _():
        lo, hi = _load_packed(x_ref)
        x = jnp.concatenate([lo.astype(BF16), hi.astype(BF16)], axis=1)
        glu = jnp.minimum(_dot(x, w1g_ref[0]) + b1g_ref[0], SWIGLU_LIMIT)
        lin = jnp.clip(_dot(x, w1l_ref[0]) + b1l_ref[0], -SWIGLU_LIMIT, SWIGLU_LIMIT)
        act = glu * _sigmoid(SWIGLU_ALPHA * glu) * (lin + 1.0)
        _store_packed(o_ref, _dot(act.astype(BF16), w2_ref[0]) + b2_ref[0])

    @pl.when(jnp.logical_not(used))
    def _():
        o_ref[...] = jnp.zeros_like(o_ref)


def _experts(block_expert, n_used, xs, w1g, w1l, b1g, b1l, w2, b2):
    R = xs.shape[0]
    D, FF = w1g.shape[1:]
    nb = R // MOE_ROWS
    per_e = lambda shape: pl.BlockSpec((1,) + shape, lambda i, be, nu: (be[i], 0, 0))
    rows = pl.BlockSpec((MOE_ROWS,) + xs.shape[1:], lambda i, be, nu: (i, 0, 0))
    return pl.pallas_call(
        _expert_kernel,
        grid_spec=pltpu.PrefetchScalarGridSpec(
            num_scalar_prefetch=2,
            grid=(nb,),
            in_specs=[rows, per_e((D, FF)), per_e((D, FF)), per_e((1, FF)), per_e((1, FF)),
                      per_e((FF, D)), per_e((1, D))],
            out_specs=rows,
        ),
        out_shape=jax.ShapeDtypeStruct(xs.shape, U32),
        compiler_params=_cparams("arbitrary"),
        name="expert_ffn",
    )(block_expert, n_used, xs, w1g, w1l, b1g, b1l, w2, b2)


def _combine_kernel(dest_ref, rows_ref, x1_ref, p_ref, g_ref, o_ref, buf_ref, idx_ref, idx_sem, row_sem):
    tm, D = x1_ref.shape
    i = pl.program_id(0)
    n = pl.num_programs(0)
    n_idx = tm * TOP_K
    idx_copy = functools.partial(_idx_copy, dest_ref, idx_ref, idx_sem, n_idx=n_idx)

    def gather(tile):
        slot = tile % 2
        base = slot * n_idx

        def issue(g, c):
            r0 = g * ISSUE_GROUP
            src = [idx_ref[base + r0 * TOP_K + k] for k in range(ISSUE_GROUP * TOP_K)]
            for k, d in enumerate(src):
                pltpu.make_async_copy(rows_ref.at[d], buf_ref.at[slot, k % TOP_K, r0 + k // TOP_K],
                                      row_sem.at[slot]).start(priority=k % 2)
            return c

        lax.fori_loop(0, tm // ISSUE_GROUP, issue, 0)

    @pl.when(i == 0)
    def _():
        idx_copy(i).start()
        idx_copy(i).wait()
        gather(i)

        @pl.when(n > 1)
        def _():
            idx_copy(i + 1).start()

    @pl.when(i + 1 < n)
    def _():
        idx_copy(i + 1).wait()
        gather(i + 1)

    @pl.when(i + 2 < n)
    def _():
        idx_copy(i + 2).start()

    slot = i % 2
    for s in range(TOP_K):
        pltpu.make_async_copy(rows_ref.at[pl.ds(0, tm)], buf_ref.at[slot, s], row_sem.at[slot]).wait()

    half = D // 2
    x1 = x1_ref[...]
    acc_lo, acc_hi = x1[:, :half], x1[:, half:]
    p = p_ref[...]
    for s in range(TOP_K):
        lo, hi = _load_packed(buf_ref.at[slot, s])
        acc_lo = acc_lo + p[:, s:s + 1] * lo
        acc_hi = acc_hi + p[:, s:s + 1] * hi
    ms = (jnp.sum(acc_lo * acc_lo, axis=-1, keepdims=True) + jnp.sum(acc_hi * acc_hi, axis=-1, keepdims=True)) / D
    scale = lax.rsqrt(ms + RMS_EPS)
    o_ref[:, :half] = acc_lo * scale * g_ref[:, :half]
    o_ref[:, half:] = acc_hi * scale * g_ref[:, half:]


def _combine(dest, rows, x1, p, gain, tm):
    T, D = x1.shape
    any_spec = pl.BlockSpec(memory_space=pl.ANY)
    return pl.pallas_call(
        _combine_kernel,
        grid=(T // tm,),
        in_specs=[any_spec, any_spec, pl.BlockSpec((tm, D), lambda i: (i, 0)),
                  pl.BlockSpec((tm, LANES), lambda i: (i, 0)), pl.BlockSpec(gain.shape, lambda i: (0, 0))],
        out_specs=pl.BlockSpec((tm, D), lambda i: (i, 0)),
        out_shape=jax.ShapeDtypeStruct((T, D), F32),
        scratch_shapes=[pltpu.VMEM((2, TOP_K, tm) + rows.shape[1:], U32), pltpu.SMEM((2 * tm * TOP_K,), jnp.int32),
                        pltpu.SemaphoreType.DMA((2,)), pltpu.SemaphoreType.DMA((2,))],
        compiler_params=_cparams("arbitrary"),
        name="moe_combine_norm",
    )(dest, rows, x1, p, gain)


def _pick_tile(n, pref):
    t = min(n, pref)
    assert n % t == 0, (n, t)
    return t


def kernel(x, norm_mix_g, w_in, gla_gate_up, gla_gate_bias, gla_norm_g, rwkv_mu, rwkv_w0, rwkv_w2, rwkv_a0, rwkv_a2, rwkv_g2, rwkv_k_k, rwkv_k_a, rwkv_r_k, rwkv_ln_w, rwkv_ln_b, w_branch_a, w_branch_b, w_out, norm_ffn_g, router_w, router_b, expert_w1, expert_b1, expert_w2, expert_b2, norm_final_g):
    assert norm_mix_g.shape[0] == 1, "single-layer block"
    B, S, D = x.shape
    T = B * S
    qk = gla_gate_bias.shape[-1]
    vw = GLA_HEADS * GLA_DV
    W = rwkv_w0.shape[-1]
    H = W // RWKV_HEAD_DIM
    E = router_w.shape[-1]
    assert qk == GLA_HEADS * GLA_DK and B * H % LANES == 0 and E <= LANES
    row = lambda a: a.reshape(1, -1)

    wi = w_in[0]
    gla_cols = 2 * qk + vw + GLA_GATE_RANK + vw
    rwkv_cols = rwkv_mu.shape[-1]
    o_q, o_k, o_v, o_lr, o_og = 0, qk, 2 * qk, 2 * qk + vw, 2 * qk + vw + GLA_GATE_RANK
    w_gla = jnp.concatenate(
        [wi[:, o_q:o_lr], wi[:, o_og:gla_cols], wi[:, o_lr:o_og],
         jnp.zeros((D, LANES - GLA_GATE_RANK), F32)], axis=1).astype(BF16)
    up = jnp.concatenate([gla_gate_up[0], jnp.zeros((LANES - GLA_GATE_RANK, qk), F32)], axis=0).astype(BF16)
    w_rwkv = wi[:, gla_cols:gla_cols + rwkv_cols].astype(BF16)
    w_gate = wi[:, gla_cols + rwkv_cols:].astype(BF16)
    n_w, n_a = rwkv_w2.shape[1], rwkv_a2.shape[1]
    assert n_w + n_a == LANES
    w2p = jnp.concatenate([rwkv_w2[0], jnp.zeros((n_a, W), F32)], axis=0).astype(BF16)
    a2p = jnp.concatenate([jnp.zeros((n_w, W), F32), rwkv_a2[0]], axis=0).astype(BF16)

    tm_gla = _pick_tile(S, 512)
    q, k, v, og, la = _gla_proj(x, row(norm_mix_g), w_gla, up, row(gla_gate_bias), tm_gla)
    ya = _gla(q, k, v, og, la, row(gla_norm_g), _pick_tile(S, 256))

    tm_r = _pick_tile(S, 512)
    hn = RWKV_HEAD_DIM // 2
    col = jnp.arange(W)
    scan_order = (col % (W // 2)) // hn * RWKV_HEAD_DIM + col // (W // 2) * hn + col % hn
    rv_, ld_, kia_, gate_ = _rwkv_proj(x, row(norm_mix_g), w_rwkv, row(rwkv_mu), row(rwkv_w0)[:, scan_order],
                                       w2p[:, scan_order], row(rwkv_a0), a2p,
                                       rwkv_g2[0][:, scan_order].astype(BF16), tm_r)
    G = B * H // LANES
    bg = B // G
    to_scan = lambda t, n: t.reshape(G, bg, S, H, n).transpose(2, 0, 4, 1, 3).reshape(S, G, n, LANES)
    par = lambda p: jnp.tile(p.reshape(H, RWKV_HEAD_DIM).T, (1, bg))
    ys = _rwkv_scan(to_scan(rv_, RWKV_HEAD_DIM), to_scan(ld_, hn), to_scan(kia_, RWKV_HEAD_DIM),
                    par(rwkv_k_k), par(rwkv_k_a), par(rwkv_r_k), par(rwkv_ln_w), par(rwkv_ln_b), _pick_tile(S, 32))
    ys = ys.reshape(S, G, hn, bg, H).transpose(1, 3, 0, 4, 2).reshape(T, W // 2)

    rw = jnp.concatenate([router_w[0], jnp.zeros((D, LANES - E), F32)], axis=1)
    rb = jnp.concatenate([router_b[0], jnp.zeros((LANES - E,), F32)]).reshape(1, LANES)
    consts = (row(norm_mix_g), w_gate, w_branch_a[0].astype(BF16), w_branch_b[0][scan_order, :].astype(BF16),
              w_out[0].astype(BF16), row(norm_ffn_g), rw, rb)
    tm_m = _pick_tile(T, 512)
    x1, h2, e_sel, rank, p_sel, counts = _merge(
        x.reshape(T, D), ya.reshape(T, vw), ys, gate_.reshape(T, W), consts, E, tm_m)

    counts = counts[0, :E]
    padded = (counts + MOE_ROWS - 1) // MOE_ROWS * MOE_ROWS
    pad_ends = jnp.cumsum(padded)
    pad_starts = pad_ends - padded
    nb = -(-T * TOP_K // MOE_ROWS) + E
    dest = pad_starts[e_sel[:, :TOP_K]] + rank[:, :TOP_K]
    block_start = jnp.arange(nb, dtype=jnp.int32) * MOE_ROWS
    block_expert = jnp.minimum(jnp.sum(pad_ends[None, :] <= block_start[:, None], axis=1), E - 1).astype(jnp.int32)
    n_used = (pad_ends[-1:] // MOE_ROWS).astype(jnp.int32)
    dest = dest.reshape(-1).astype(jnp.int32)
    tm_d = _pick_tile(T, 256)
    xs = _dispatch(dest, h2, nb * MOE_ROWS, _pick_tile(T, 1024))

    FF = expert_w2.shape[2]
    w1g, w1l = _split_w1(expert_w1[0])
    b1 = expert_b1[0].reshape(E, 1, FF, 2)
    outs_e = _experts(block_expert, n_used, xs, w1g, w1l, b1[..., 0], b1[..., 1],
                      expert_w2[0].astype(BF16), expert_b2[0].reshape(E, 1, D))

    out = _combine(dest, outs_e, x1, p_sel, row(norm_final_g), tm_d)
    return out.reshape(B, S, D)
```

```python
import functools

import jax
import jax.numpy as jnp
from jax import lax
from jax.experimental import pallas as pl
from jax.experimental.pallas import tpu as pltpu

F32 = jnp.float32
BF16 = jnp.bfloat16
U32 = jnp.uint32
HIGHEST = lax.Precision.HIGHEST

LANES = 128
VMEM_LIMIT = 56 * 1024 * 1024

RMS_EPS = 1e-6
GLA_HEADS = 4
GLA_DK = 128
GLA_DV = 256
GLA_GATE_RANK = 16
GLA_TAU = 16.0
GLA_CHUNK = 64
GLA_NORM_EPS = 1e-5
RWKV_HEAD_DIM = 64
RWKV_GN_EPS = 64e-5
TOP_K = 4
SWIGLU_ALPHA = 1.702
SWIGLU_LIMIT = 7.0
MOE_ROWS = 512
ISSUE_GROUP = 8
MERGE_PARTS = 2
PROJ_PARTS = 2

def _cparams(*sem):
    return pltpu.CompilerParams(dimension_semantics=sem, vmem_limit_bytes=VMEM_LIMIT)


def _rms(xf, gain):
    return xf * lax.rsqrt(jnp.mean(xf * xf, axis=-1, keepdims=True) + RMS_EPS) * gain


def _softplus(y):
    return jnp.maximum(y, 0.0) + jnp.log1p(jnp.exp(-jnp.abs(y)))


def _sigmoid(y):
    return 1.0 / (1.0 + jnp.exp(-y))


def _dot(a, b):
    return jnp.dot(a, b, preferred_element_type=F32)


def _dot_hi(a, b):
    return jnp.dot(a, b, preferred_element_type=F32, precision=HIGHEST)


def _pack_pair(lo, hi):
    bits = lambda t: lax.bitcast_convert_type(t.astype(BF16).astype(F32), U32)
    return (bits(lo) >> 16) | bits(hi)


def _pack_rows(x):
    n = x.shape[-1] // 2
    return _pack_pair(x[:, :n], x[:, n:])


def _unpack_rows(w):
    lo = lax.bitcast_convert_type(w << 16, F32)
    hi = lax.bitcast_convert_type(w & jnp.uint32(0xFFFF0000), F32)
    return lo, hi


def _store_packed(ref, x):
    packed = _pack_rows(x)
    for c in range(ref.shape[1]):
        ref[:, c, :] = packed[:, c * LANES:(c + 1) * LANES]


def _load_packed(ref):
    return _unpack_rows(jnp.concatenate([ref[:, c, :] for c in range(ref.shape[1])], axis=1))


def _gla_proj_kernel(x_ref, g_ref, w_ref, up_ref, bias_ref, q_ref, k_ref, v_ref, og_ref, la_ref):
    qk = q_ref.shape[-1]
    vw = v_ref.shape[-1]
    pm = x_ref.shape[1] // PROJ_PARTS
    parts = [slice(i * pm, (i + 1) * pm) for i in range(PROJ_PARTS)]
    zs = [_dot(_rms(x_ref[0, rows, :], g_ref[...]).astype(BF16), w_ref[...]) for rows in parts]
    for rows, z in zip(parts, zs):
        q_ref[0, rows, :] = z[:, :qk].astype(BF16)
        k_ref[0, rows, :] = z[:, qk:2 * qk].astype(BF16)
        v_ref[0, rows, :] = z[:, 2 * qk:2 * qk + vw].astype(BF16)
        og_ref[0, rows, :] = z[:, 2 * qk + vw:2 * qk + 2 * vw].astype(BF16)
        lr = z[:, 2 * qk + 2 * vw:].astype(BF16)
        pre = _dot(lr, up_ref[...]) + bias_ref[...]
        la_ref[0, rows, :] = -_softplus(-pre) * (1.0 / GLA_TAU)


def _gla_proj(x, gain, w, up, bias, tm):
    B, S, D = x.shape
    qk = bias.shape[-1]
    vw = (w.shape[1] - 2 * qk - LANES) // 2
    tok = lambda width: pl.BlockSpec((1, tm, width), lambda b, i: (b, i, 0))
    full = lambda a: pl.BlockSpec(a.shape, lambda b, i: (0,) * a.ndim)
    return pl.pallas_call(
        _gla_proj_kernel,
        grid=(B, S // tm),
        in_specs=[tok(D), full(gain), full(w), full(up), full(bias)],
        out_specs=[tok(qk), tok(qk), tok(vw), tok(vw), tok(qk)],
        out_shape=[jax.ShapeDtypeStruct((B, S, qk), BF16), jax.ShapeDtypeStruct((B, S, qk), BF16),
                   jax.ShapeDtypeStruct((B, S, vw), BF16), jax.ShapeDtypeStruct((B, S, vw), BF16),
                   jax.ShapeDtypeStruct((B, S, qk), F32)],
        compiler_params=_cparams("parallel", "parallel"),
        name="gla_proj",
    )(x, gain, w, up, bias)


def _gla_kernel(q_ref, k_ref, v_ref, og_ref, la_ref, ng_ref, y_ref, st_ref, qa_ref, ka_ref, qi_ref, ks_ref, dec_ref,
                sc_ref):
    C = GLA_CHUNK
    tc = q_ref.shape[1]

    @pl.when(pl.program_id(1) == 0)
    def _():
        st_ref[...] = jnp.zeros_like(st_ref)

    row = lax.broadcasted_iota(jnp.int32, (tc, tc), 0)
    col = lax.broadcasted_iota(jnp.int32, (tc, tc), 1)
    same_chunk_before = jnp.logical_and(row >= col, row // C == col // C)
    b_all = _dot_hi(same_chunk_before.astype(F32), la_ref[0])
    scale = GLA_DK ** -0.5
    for c in range(tc // C):
        rows = slice(c * C, (c + 1) * C)
        b = b_all[rows]
        b_mid = b[C // 2:C // 2 + 1]
        b_end = b[C - 1:C]
        q = q_ref[0, rows, :].astype(F32) * scale
        k = k_ref[0, rows, :].astype(F32)
        qa_ref[rows, :] = (q * jnp.exp(b - b_mid)).astype(BF16)
        ka_ref[rows, :] = (k * jnp.exp(b_mid - b)).astype(BF16)
        qi_ref[rows, :] = (q * jnp.exp(b)).astype(BF16)
        ks_ref[rows, :] = (k * jnp.exp(b_end - b)).astype(BF16)
        dec_ref[c:c + 1, :] = jnp.exp(b_end)

    r_c = lax.broadcasted_iota(jnp.int32, (C, C), 0)
    c_c = lax.broadcasted_iota(jnp.int32, (C, C), 1)
    causal = r_c >= c_c
    for c in range(tc // C):
        rows = slice(c * C, (c + 1) * C)
        for h in range(GLA_HEADS):
            kc = slice(h * GLA_DK, (h + 1) * GLA_DK)
            scores = lax.dot_general(qa_ref[rows, kc], ka_ref[rows, kc], (((1,), (1,)), ((), ())),
                                     preferred_element_type=F32)
            sc_ref[rows, h * C:(h + 1) * C] = jnp.where(causal, scores, 0.0).astype(BF16)

    for c in range(tc // C):
        rows = slice(c * C, (c + 1) * C)
        for h in range(GLA_HEADS):
            kc = slice(h * GLA_DK, (h + 1) * GLA_DK)
            vc = slice(h * GLA_DV, (h + 1) * GLA_DV)
            vh = v_ref[0, rows, vc]
            st = st_ref[h]
            o = _dot(sc_ref[rows, h * C:(h + 1) * C], vh) + lax.dot_general(
                qi_ref[rows, kc], st.astype(BF16), (((1,), (1,)), ((), ())), preferred_element_type=F32)
            st_ref[h] = st * dec_ref[c:c + 1, kc] + lax.dot_general(
                vh, ks_ref[rows, kc], (((0,), (0,)), ((), ())), preferred_element_type=F32)
            o = o * lax.rsqrt(jnp.mean(o * o, axis=-1, keepdims=True) + GLA_NORM_EPS) * ng_ref[...]
            og = og_ref[0, rows, vc].astype(F32)
            y_ref[0, rows, vc] = (o * (og * _sigmoid(og))).astype(BF16)


def _gla(q, k, v, og, la, norm_g, tc):
    B, S, qk = q.shape
    vw = v.shape[-1]
    tok = lambda width: pl.BlockSpec((1, tc, width), lambda b, i: (b, i, 0))
    return pl.pallas_call(
        _gla_kernel,
        grid=(B, S // tc),
        in_specs=[tok(qk), tok(qk), tok(vw), tok(vw), tok(qk),
                  pl.BlockSpec(norm_g.shape, lambda b, i: (0, 0))],
        out_specs=tok(vw),
        out_shape=jax.ShapeDtypeStruct((B, S, vw), BF16),
        scratch_shapes=[pltpu.VMEM((GLA_HEADS, GLA_DV, GLA_DK), F32)] + [pltpu.VMEM((tc, qk), BF16)] * 4
        + [pltpu.VMEM((8, qk), F32), pltpu.VMEM((tc, GLA_HEADS * GLA_CHUNK), BF16)],
        compiler_params=_cparams("parallel", "arbitrary"),
        name="gla_chunk",
    )(q, k, v, og, la, norm_g)


def _rwkv_proj_kernel(x_ref, g_ref, w_ref, mu_ref, w0_ref, w2_ref, a0_ref, a2_ref, g2_ref,
                      rv_ref, w_out_ref, ka_ref, gate_ref, carry_ref):
    W = rv_ref.shape[-1]
    tm = x_ref.shape[1]

    @pl.when(pl.program_id(1) == 0)
    def _():
        carry_ref[...] = jnp.zeros_like(carry_ref)

    pm = tm // PROJ_PARTS
    parts = [slice(i * pm, (i + 1) * pm) for i in range(PROJ_PARTS)]
    zs = [_dot(_rms(x_ref[0, rows, :], g_ref[...]).astype(BF16), w_ref[...]) for rows in parts]
    first = lax.broadcasted_iota(jnp.int32, zs[0].shape, 0) == 0
    last_row = carry_ref[0:1, :]
    for rows, z in zip(parts, zs):
        prev = jnp.where(first, last_row, pltpu.roll(z, 1, axis=0))
        last_row = z[pm - 1:pm, :]
        u = z + mu_ref[...] * (prev - z)

        r = u[:, :W]
        k = u[:, W:2 * W]
        v = u[:, 2 * W:3 * W]
        lora = u[:, 3 * W:3 * W + LANES]
        xg = u[:, 3 * W + LANES:]
        w_log = -_softplus(-(w0_ref[...] + _dot(jnp.tanh(lora).astype(BF16), w2_ref[...]))) - 0.5
        iclr = _sigmoid(a0_ref[...] + _dot(lora.astype(BF16), a2_ref[...]))
        rv_ref[0, rows, :] = _pack_pair(r, v)
        ka_ref[0, rows, :] = _pack_pair(k, iclr)
        w_out_ref[0, rows, :] = _pack_rows(-jnp.exp(w_log))
        gate_ref[0, rows, :] = _dot(_sigmoid(xg).astype(BF16), g2_ref[...]).astype(BF16)
    carry_ref[0:1, :] = last_row


def _rwkv_proj(x, gain, w, mu, w0, w2p, a0, a2p, g2, tm):
    B, S, D = x.shape
    W = w0.shape[-1]
    tok = lambda width: pl.BlockSpec((1, tm, width), lambda b, i: (b, i, 0))
    full = lambda a: pl.BlockSpec(a.shape, lambda b, i: (0,) * a.ndim)
    consts = (gain, w, mu, w0, w2p, a0, a2p, g2)
    return pl.pallas_call(
        _rwkv_proj_kernel,
        grid=(B, S // tm),
        in_specs=[tok(D)] + [full(c) for c in consts],
        out_specs=[tok(W), tok(W // 2), tok(W), tok(W)],
        out_shape=[jax.ShapeDtypeStruct((B, S, W), U32), jax.ShapeDtypeStruct((B, S, W // 2), U32),
                   jax.ShapeDtypeStruct((B, S, W), U32), jax.ShapeDtypeStruct((B, S, W), BF16)],
        scratch_shapes=[pltpu.VMEM((8, w.shape[1]), F32)],
        compiler_params=_cparams("parallel", "arbitrary"),
        name="rwkv_proj",
    )(x, *consts)


def _rwkv_scan_kernel(rv_ref, ld_ref, kia_ref, kian_ref, kk_ref, ka_ref, rk_ref, lnw_ref, lnb_ref,
                      y_ref, st_ref, sa_ref, av_ref, bv_ref, km_ref, r_ref, v_ref, yr_ref, w_ref):
    tt, G, N, _ = rv_ref.shape
    half = N // 2

    def step_vectors(packed):
        k_raw, iclr = _unpack_rows(packed)
        kk = k_raw * kk_ref[...]
        kk = kk / jnp.maximum(jnp.sqrt(jnp.sum(kk * kk, axis=0, keepdims=True)), 1e-12)
        return -kk, kk * iclr, k_raw * (1.0 + (iclr - 1.0) * ka_ref[...])

    @pl.when(pl.program_id(0) == 0)
    def _():
        st_ref[...] = jnp.zeros_like(st_ref)
        sa_ref[...] = jnp.zeros_like(sa_ref)
        for g in range(G):
            _, bv_ref[0, g], km_ref[0, g] = step_vectors(kia_ref[0, g])

    t0 = pl.program_id(0) * tt

    def group_step(tg, carry):
        t = tg // G
        g = tg % G
        cur = (t0 + t) % 2
        last = t + 1 >= tt
        t_nx = jnp.minimum(t + 1, tt - 1)
        av_ref[...], bv_ref[1 - cur, g], km_ref[1 - cur, g] = step_vectors(
            jnp.where(last, kian_ref[0, g], kia_ref[t_nx, g]))
        r_ref[...], v_ref[...] = _unpack_rows(rv_ref[t, g])
        w_ref[...] = jnp.exp(jnp.concatenate(_unpack_rows(ld_ref[t, g]), axis=0))

        def half_step(ih, c):
            rows = slice(ih * half, (ih + 1) * half)
            sa_h = sa_ref[g, rows, :]
            vt = v_ref[rows, :]
            y = jnp.zeros((half, LANES), F32)
            sa_nx = jnp.zeros((half, LANES), F32)
            for j in range(N):
                row = slice(j, j + 1)
                s_new = (st_ref[g, j, rows, :] * w_ref[row, :] + sa_h * bv_ref[cur, g, row, :]
                         + vt * km_ref[cur, g, row, :])
                st_ref[g, j, rows, :] = s_new
                y = y + s_new * r_ref[row, :]
                sa_nx = sa_nx + s_new * av_ref[row, :]
            yr_ref[rows, :] = y
            sa_ref[g, rows, :] = sa_nx
            return c

        for ih in range(2):
            half_step(ih, 0)

        y = yr_ref[...]
        mean = jnp.mean(y, axis=0, keepdims=True)
        d = y - mean
        var = jnp.mean(d * d, axis=0, keepdims=True)
        bonus = jnp.sum(r_ref[...] * km_ref[cur, g] * rk_ref[...], axis=0, keepdims=True)
        out = d * lax.rsqrt(var + RWKV_GN_EPS) * lnw_ref[...] + lnb_ref[...] + bonus * v_ref[...]
        y_ref[t, g] = _pack_pair(out[:half], out[half:])
        return carry

    lax.fori_loop(0, tt * G, group_step, 0)


def _rwkv_scan(rv, log_decay, kia, k_k, k_a, r_k, ln_w, ln_b, tt):
    S, G, N, _ = rv.shape
    blk = pl.BlockSpec((tt, G, N, LANES), lambda i: (i, 0, 0, 0))
    pair = pl.BlockSpec((tt, G, N // 2, LANES), lambda i: (i, 0, 0, 0))
    nxt = pl.BlockSpec((1, G, N, LANES), lambda i: (jnp.minimum((i + 1) * tt, S - 1), 0, 0, 0))
    par = pl.BlockSpec((N, LANES), lambda i: (0, 0))
    tile = pltpu.VMEM((N, LANES), F32)
    per_g = pltpu.VMEM((G, N, LANES), F32)
    two_g = pltpu.VMEM((2, G, N, LANES), F32)
    return pl.pallas_call(
        _rwkv_scan_kernel,
        grid=(S // tt,),
        in_specs=[blk, pair, blk, nxt] + [par] * 5,
        out_specs=pair,
        out_shape=jax.ShapeDtypeStruct((S, G, N // 2, LANES), U32),
        scratch_shapes=[pltpu.VMEM((G, N, N, LANES), F32), per_g, tile, two_g, two_g, tile, tile, tile, tile],
        compiler_params=_cparams("arbitrary"),
        name="rwkv_scan",
    )(rv, log_decay, kia, kia, k_k, k_a, r_k, ln_w, ln_b)


def _merge_kernel(x_ref, ya_ref, ys_ref, gate_ref, g_mix_ref, wg_ref, wa_ref, wb_ref, wo_ref, g_ffn_ref,
                  rw_ref, rb_ref,
                  x1_ref, h2_ref, e_ref, rank_ref, p_ref, cnt_ref, carry_ref, *, n_experts):
    tm, D = x_ref.shape

    @pl.when(pl.program_id(0) == 0)
    def _():
        carry_ref[...] = jnp.zeros_like(carry_ref)

    hm = tm // MERGE_PARTS
    lane = lax.broadcasted_iota(jnp.int32, (hm, LANES), 1)

    def mix(rows):
        x = x_ref[rows, :]
        h = _rms(x, g_mix_ref[...]).astype(BF16)
        gates = _sigmoid(_dot(h, wg_ref[...]))
        ys = jnp.concatenate(_unpack_rows(ys_ref[rows, :]), axis=1)
        yb = (ys * gate_ref[rows, :].astype(F32)).astype(BF16)
        merged = gates[:, :D] * _dot(ya_ref[rows, :], wa_ref[...]) + gates[:, D:] * _dot(yb, wb_ref[...])
        x1 = x + _dot(merged.astype(BF16), wo_ref[...])
        x1_ref[rows, :] = x1
        h2 = _rms(x1, g_ffn_ref[...])
        _store_packed(h2_ref.at[rows], h2)
        return jnp.where(lane < n_experts, _dot_hi(h2, rw_ref[...]) + rb_ref[...], -jnp.inf)

    def route(rows, logits, seen):
        vals, idxs, hots = [], [], []
        for _ in range(TOP_K):
            m = jnp.max(logits, axis=-1, keepdims=True)
            idx = jnp.min(jnp.where(logits == m, lane, LANES), axis=-1, keepdims=True)
            hot = lane == idx
            vals.append(m)
            idxs.append(idx)
            hots.append(hot)
            logits = jnp.where(hot, -jnp.inf, logits)
        exps = [jnp.exp(val - vals[0]) for val in vals]
        denom = functools.reduce(lambda s, e: s + e, exps)
        chosen = functools.reduce(lambda s, e: s + e, [hot.astype(F32) for hot in hots])
        r_i = lax.broadcasted_iota(jnp.int32, (hm, hm), 0)
        c_i = lax.broadcasted_iota(jnp.int32, (hm, hm), 1)
        before = _dot((r_i > c_i).astype(BF16), chosen.astype(BF16)) + seen
        e_out = jnp.zeros((hm, LANES), jnp.int32)
        rank_out = jnp.zeros((hm, LANES), jnp.int32)
        p_out = jnp.zeros((hm, LANES), F32)
        for s in range(TOP_K):
            rank = jnp.sum(jnp.where(hots[s], before, 0.0), axis=-1, keepdims=True)
            e_out = jnp.where(lane == s, idxs[s], e_out)
            rank_out = jnp.where(lane == s, rank.astype(jnp.int32), rank_out)
            p_out = jnp.where(lane == s, exps[s] / denom, p_out)
        e_ref[rows, :] = e_out
        rank_ref[rows, :] = rank_out
        p_ref[rows, :] = p_out
        return seen + jnp.sum(chosen, axis=0, keepdims=True)

    parts = [slice(i * hm, (i + 1) * hm) for i in range(MERGE_PARTS)]
    logits = [mix(rows) for rows in parts]
    seen = carry_ref[0:1, :]
    for rows, lg in zip(parts, logits):
        seen = route(rows, lg, seen)
    carry_ref[0:1, :] = seen
    cnt_ref[...] = jnp.broadcast_to(seen, cnt_ref.shape).astype(jnp.int32)


def _merge(x, ya, ys, gate, consts, n_experts, tm):
    T, D = x.shape
    W = gate.shape[-1]
    tok = lambda width: pl.BlockSpec((tm, width), lambda i: (i, 0))
    full = lambda a: pl.BlockSpec(a.shape, lambda i: (0,) * a.ndim)
    return pl.pallas_call(
        functools.partial(_merge_kernel, n_experts=n_experts),
        grid=(T // tm,),
        in_specs=[tok(D), tok(ya.shape[-1]), tok(ys.shape[-1]), tok(W)] + [full(c) for c in consts],
        out_specs=[tok(D), pl.BlockSpec((tm, D // 2 // LANES, LANES), lambda i: (i, 0, 0)),
                   tok(LANES), tok(LANES), tok(LANES), pl.BlockSpec((8, LANES), lambda i: (0, 0))],
        out_shape=[jax.ShapeDtypeStruct((T, D), F32), jax.ShapeDtypeStruct((T, D // 2 // LANES, LANES), U32),
                   jax.ShapeDtypeStruct((T, LANES), jnp.int32), jax.ShapeDtypeStruct((T, LANES), jnp.int32),
                   jax.ShapeDtypeStruct((T, LANES), F32), jax.ShapeDtypeStruct((8, LANES), jnp.int32)],
        scratch_shapes=[pltpu.VMEM((8, LANES), F32)],
        compiler_params=_cparams("arbitrary"),
        name="merge_router",
    )(x, ya, ys, gate, *consts)


def _idx_copy(dest_ref, idx_ref, sem, tile, n_idx):
    slot = tile % 2
    return pltpu.make_async_copy(dest_ref.at[pl.ds(pl.multiple_of(tile * n_idx, n_idx), n_idx)],
                                 idx_ref.at[pl.ds(pl.multiple_of(slot * n_idx, n_idx), n_idx)], sem.at[slot])


def _dispatch_kernel(dest_ref, rows_ref, init_ref, xs_ref, idx_ref, idx_sem, row_sem, *, tm):
    del init_ref
    i = pl.program_id(0)
    n = pl.num_programs(0)
    n_idx = tm * TOP_K
    idx_copy = functools.partial(_idx_copy, dest_ref, idx_ref, idx_sem, n_idx=n_idx)

    @pl.when(i == 0)
    def _():
        idx_copy(i).start()

    idx_copy(i).wait()

    @pl.when(i + 1 < n)
    def _():
        idx_copy(i + 1).start()

    base = (i % 2) * n_idx

    def issue(g, c):
        r0 = g * ISSUE_GROUP
        dst = [idx_ref[base + r0 * TOP_K + k] for k in range(ISSUE_GROUP * TOP_K)]
        for k, d in enumerate(dst):
            pltpu.make_async_copy(rows_ref.at[r0 + k // TOP_K], xs_ref.at[d], row_sem).start(priority=k % 2)
        return c

    lax.fori_loop(0, tm // ISSUE_GROUP, issue, 0)

    for _ in range(TOP_K):
        pltpu.make_async_copy(rows_ref, xs_ref.at[pl.ds(0, tm)], row_sem).wait()


def _dispatch(dest, rows, n_rows, tm):
    T = rows.shape[0]
    packed = (n_rows,) + rows.shape[1:]
    any_spec = pl.BlockSpec(memory_space=pl.ANY)
    return pl.pallas_call(
        functools.partial(_dispatch_kernel, tm=tm),
        grid=(T // tm,),
        in_specs=[any_spec, pl.BlockSpec((tm,) + rows.shape[1:], lambda i: (i, 0, 0)), any_spec],
        out_specs=any_spec,
        out_shape=jax.ShapeDtypeStruct(packed, U32),
        input_output_aliases={2: 0},
        scratch_shapes=[pltpu.SMEM((2 * tm * TOP_K,), jnp.int32), pltpu.SemaphoreType.DMA((2,)),
                        pltpu.SemaphoreType.DMA],
        compiler_params=_cparams("arbitrary"),
        name="moe_dispatch",
    )(dest, rows, jnp.zeros(packed, U32))


def _split_w1_kernel(w_ref, glu_ref, lin_ref):
    cols = w_ref.shape[-1]
    src = lax.broadcasted_iota(jnp.int32, (cols, cols // 2), 0)
    dst = lax.broadcasted_iota(jnp.int32, (cols, cols // 2), 1)
    w = w_ref[0].astype(BF16)
    glu_ref[0] = _dot(w, (src == 2 * dst).astype(BF16)).astype(BF16)
    lin_ref[0] = _dot(w, (src == 2 * dst + 1).astype(BF16)).astype(BF16)


def _split_w1(w1, cols=512):
    E, D, FF2 = w1.shape
    out = jax.ShapeDtypeStruct((E, D, FF2 // 2), BF16)
    return pl.pallas_call(
        _split_w1_kernel,
        grid=(E, FF2 // cols),
        in_specs=[pl.BlockSpec((1, D, cols), lambda e, c: (e, 0, c))],
        out_specs=[pl.BlockSpec((1, D, cols // 2), lambda e, c: (e, 0, c))] * 2,
        out_shape=[out, out],
        compiler_params=_cparams("parallel", "parallel"),
        name="split_w1",
    )(w1)


def _expert_kernel(be_ref, nu_ref, x_ref, w1g_ref, w1l_ref, b1g_ref, b1l_ref, w2_ref, b2_ref, o_ref):
    del be_ref
    used = pl.program_id(0) < nu_ref[0]

    @pl.when(used)
    def _():
        lo, hi = _load_packed(x_ref)
        x = jnp.concatenate([lo.astype(BF16), hi.astype(BF16)], axis=1)
        glu = jnp.minimum(_dot(x, w1g_ref[0]) + b1g_ref[0], SWIGLU_LIMIT)
        lin = jnp.clip(_dot(x, w1l_ref[0]) + b1l_ref[0], -SWIGLU_LIMIT, SWIGLU_LIMIT)
        act = glu * _sigmoid(SWIGLU_ALPHA * glu) * (lin + 1.0)
        _store_packed(o_ref, _dot(act.astype(BF16), w2_ref[0]) + b2_ref[0])

    @pl.when(jnp.logical_not(used))
    def _():
        o_ref[...] = jnp.zeros_like(o_ref)


def _experts(block_expert, n_used, xs, w1g, w1l, b1g, b1l, w2, b2):
    R = xs.shape[0]
    D, FF = w1g.shape[1:]
    nb = R // MOE_ROWS
    per_e = lambda shape: pl.BlockSpec((1,) + shape, lambda i, be, nu: (be[i], 0, 0))
    rows = pl.BlockSpec((MOE_ROWS,) + xs.shape[1:], lambda i, be, nu: (i, 0, 0))
    return pl.pallas_call(
        _expert_kernel,
        grid_spec=pltpu.PrefetchScalarGridSpec(
            num_scalar_prefetch=2,
            grid=(nb,),
            in_specs=[rows, per_e((D, FF)), per_e((D, FF)), per_e((1, FF)), per_e((1, FF)),
                      per_e((FF, D)), per_e((1, D))],
            out_specs=rows,
        ),
        out_shape=jax.ShapeDtypeStruct(xs.shape, U32),
        compiler_params=_cparams("arbitrary"),
        name="expert_ffn",
    )(block_expert, n_used, xs, w1g, w1l, b1g, b1l, w2, b2)


def _combine_kernel(dest_ref, rows_ref, x1_ref, p_ref, g_ref, o_ref, buf_ref, idx_ref, idx_sem, row_sem):
    tm, D = x1_ref.shape
    i = pl.program_id(0)
    n = pl.num_programs(0)
    n_idx = tm * TOP_K
    idx_copy = functools.partial(_idx_copy, dest_ref, idx_ref, idx_sem, n_idx=n_idx)

    def gather(tile):
        slot = tile % 2
        base = slot * n_idx

        def issue(g, c):
            r0 = g * ISSUE_GROUP
            src = [idx_ref[base + r0 * TOP_K + k] for k in range(ISSUE_GROUP * TOP_K)]
            for k, d in enumerate(src):
                pltpu.make_async_copy(rows_ref.at[d], buf_ref.at[slot, k % TOP_K, r0 + k // TOP_K],
                                      row_sem.at[slot]).start(priority=k % 2)
            return c

        lax.fori_loop(0, tm // ISSUE_GROUP, issue, 0)

    @pl.when(i == 0)
    def _():
        idx_copy(i).start()
        idx_copy(i).wait()
        gather(i)

        @pl.when(n > 1)
        def _():
            idx_copy(i + 1).start()

    @pl.when(i + 1 < n)
    def _():
        idx_copy(i + 1).wait()
        gather(i + 1)

    @pl.when(i + 2 < n)
    def _():
        idx_copy(i + 2).start()

    slot = i % 2
    for s in range(TOP_K):
        pltpu.make_async_copy(rows_ref.at[pl.ds(0, tm)], buf_ref.at[slot, s], row_sem.at[slot]).wait()

    half = D // 2
    x1 = x1_ref[...]
    acc_lo, acc_hi = x1[:, :half], x1[:, half:]
    p = p_ref[...]
    for s in range(TOP_K):
        lo, hi = _load_packed(buf_ref.at[slot, s])
        acc_lo = acc_lo + p[:, s:s + 1] * lo
        acc_hi = acc_hi + p[:, s:s + 1] * hi
    ms = (jnp.sum(acc_lo * acc_lo, axis=-1, keepdims=True) + jnp.sum(acc_hi * acc_hi, axis=-1, keepdims=True)) / D
    scale = lax.rsqrt(ms + RMS_EPS)
    o_ref[:, :half] = acc_lo * scale * g_ref[:, :half]
    o_ref[:, half:] = acc_hi * scale * g_ref[:, half:]


def _combine(dest, rows, x1, p, gain, tm):
    T, D = x1.shape
    any_spec = pl.BlockSpec(memory_space=pl.ANY)
    return pl.pallas_call(
        _combine_kernel,
        grid=(T // tm,),
        in_specs=[any_spec, any_spec, pl.BlockSpec((tm, D), lambda i: (i, 0)),
                  pl.BlockSpec((tm, LANES), lambda i: (i, 0)), pl.BlockSpec(gain.shape, lambda i: (0, 0))],
        out_specs=pl.BlockSpec((tm, D), lambda i: (i, 0)),
        out_shape=jax.ShapeDtypeStruct((T, D), F32),
        scratch_shapes=[pltpu.VMEM((2, TOP_K, tm) + rows.shape[1:], U32), pltpu.SMEM((2 * tm * TOP_K,), jnp.int32),
                        pltpu.SemaphoreType.DMA((2,)), pltpu.SemaphoreType.DMA((2,))],
        compiler_params=_cparams("arbitrary"),
        name="moe_combine_norm",
    )(dest, rows, x1, p, gain)


def _pick_tile(n, pref):
    t = min(n, pref)
    assert n % t == 0, (n, t)
    return t


def kernel(x, norm_mix_g, w_in, gla_gate_up, gla_gate_bias, gla_norm_g, rwkv_mu, rwkv_w0, rwkv_w2, rwkv_a0, rwkv_a2, rwkv_g2, rwkv_k_k, rwkv_k_a, rwkv_r_k, rwkv_ln_w, rwkv_ln_b, w_branch_a, w_branch_b, w_out, norm_ffn_g, router_w, router_b, expert_w1, expert_b1, expert_w2, expert_b2, norm_final_g):
    assert norm_mix_g.shape[0] == 1, "single-layer block"
    B, S, D = x.shape
    T = B * S
    qk = gla_gate_bias.shape[-1]
    vw = GLA_HEADS * GLA_DV
    W = rwkv_w0.shape[-1]
    H = W // RWKV_HEAD_DIM
    E = router_w.shape[-1]
    assert qk == GLA_HEADS * GLA_DK and B * H % LANES == 0 and E <= LANES
    row = lambda a: a.reshape(1, -1)

    wi = w_in[0]
    gla_cols = 2 * qk + vw + GLA_GATE_RANK + vw
    rwkv_cols = rwkv_mu.shape[-1]
    o_q, o_k, o_v, o_lr, o_og = 0, qk, 2 * qk, 2 * qk + vw, 2 * qk + vw + GLA_GATE_RANK
    w_gla = jnp.concatenate(
        [wi[:, o_q:o_lr], wi[:, o_og:gla_cols], wi[:, o_lr:o_og],
         jnp.zeros((D, LANES - GLA_GATE_RANK), F32)], axis=1).astype(BF16)
    up = jnp.concatenate([gla_gate_up[0], jnp.zeros((LANES - GLA_GATE_RANK, qk), F32)], axis=0).astype(BF16)
    w_rwkv = wi[:, gla_cols:gla_cols + rwkv_cols].astype(BF16)
    w_gate = wi[:, gla_cols + rwkv_cols:].astype(BF16)
    n_w, n_a = rwkv_w2.shape[1], rwkv_a2.shape[1]
    assert n_w + n_a == LANES
    w2p = jnp.concatenate([rwkv_w2[0], jnp.zeros((n_a, W), F32)], axis=0).astype(BF16)
    a2p = jnp.concatenate([jnp.zeros((n_w, W), F32), rwkv_a2[0]], axis=0).astype(BF16)

    tm_gla = _pick_tile(S, 512)
    q, k, v, og, la = _gla_proj(x, row(norm_mix_g), w_gla, up, row(gla_gate_bias), tm_gla)
    ya = _gla(q, k, v, og, la, row(gla_norm_g), _pick_tile(S, 256))

    tm_r = _pick_tile(S, 512)
    hn = RWKV_HEAD_DIM // 2
    col = jnp.arange(W)
    scan_order = (col % (W // 2)) // hn * RWKV_HEAD_DIM + col // (W // 2) * hn + col % hn
    rv_, ld_, kia_, gate_ = _rwkv_proj(x, row(norm_mix_g), w_rwkv, row(rwkv_mu), row(rwkv_w0)[:, scan_order],
                                       w2p[:, scan_order], row(rwkv_a0), a2p,
                                       rwkv_g2[0][:, scan_order].astype(BF16), tm_r)
    G = B * H // LANES
    bg = B // G
    to_scan = lambda t, n: t.reshape(G, bg, S, H, n).transpose(2, 0, 4, 1, 3).reshape(S, G, n, LANES)
    par = lambda p: jnp.tile(p.reshape(H, RWKV_HEAD_DIM).T, (1, bg))
    ys = _rwkv_scan(to_scan(rv_, RWKV_HEAD_DIM), to_scan(ld_, hn), to_scan(kia_, RWKV_HEAD_DIM),
                    par(rwkv_k_k), par(rwkv_k_a), par(rwkv_r_k), par(rwkv_ln_w), par(rwkv_ln_b), _pick_tile(S, 32))
    ys = ys.reshape(S, G, hn, bg, H).transpose(1, 3, 0, 4, 2).reshape(T, W // 2)

    rw = jnp.concatenate([router_w[0], jnp.zeros((D, LANES - E), F32)], axis=1)
    rb = jnp.concatenate([router_b[0], jnp.zeros((LANES - E,), F32)]).reshape(1, LANES)
    consts = (row(norm_mix_g), w_gate, w_branch_a[0].astype(BF16), w_branch_b[0][scan_order, :].astype(BF16),
              w_out[0].astype(BF16), row(norm_ffn_g), rw, rb)
    tm_m = _pick_tile(T, 512)
    x1, h2, e_sel, rank, p_sel, counts = _merge(
        x.reshape(T, D), ya.reshape(T, vw), ys, gate_.reshape(T, W), consts, E, tm_m)

    counts = counts[0, :E]
    padded = (counts + MOE_ROWS - 1) // MOE_ROWS * MOE_ROWS
    pad_ends = jnp.cumsum(padded)
    pad_starts = pad_ends - padded
    nb = -(-T * TOP_K // MOE_ROWS) + E
    dest = pad_starts[e_sel[:, :TOP_K]] + rank[:, :TOP_K]
    block_start = jnp.arange(nb, dtype=jnp.int32) * MOE_ROWS
    block_expert = jnp.minimum(jnp.sum(pad_ends[None, :] <= block_start[:, None], axis=1), E - 1).astype(jnp.int32)
    n_used = (pad_ends[-1:] // MOE_ROWS).astype(jnp.int32)
    dest = dest.reshape(-1).astype(jnp.int32)
    tm_d = _pick_tile(T, 256)
    xs = _dispatch(dest, h2, nb * MOE_ROWS, _pick_tile(T, 1024))

    FF = expert_w2.shape[2]
    w1g, w1l = _split_w1(expert_w1[0])
    b1 = expert_b1[0].reshape(E, 1, FF, 2)
    outs_e = _experts(block_expert, n_used, xs, w1g, w1l, b1[..., 0], b1[..., 1],
                      expert_w2[0].astype(BF16), expert_b2[0].reshape(E, 1, D))

    out = _combine(dest, outs_e, x1, p_sel, row(norm_final_g), tm_d)
    return out.reshape(B, S, D)
```

```python
import functools

import jax
import jax.numpy as jnp
from jax import lax
from jax.experimental import pallas as pl
from jax.experimental.pallas import tpu as pltpu

F32 = jnp.float32
BF16 = jnp.bfloat16
U32 = jnp.uint32
HIGHEST = lax.Precision.HIGHEST

LANES = 128
VMEM_LIMIT = 56 * 1024 * 1024

RMS_EPS = 1e-6
GLA_HEADS = 4
GLA_DK = 128
GLA_DV = 256
GLA_GATE_RANK = 16
GLA_TAU = 16.0
GLA_CHUNK = 64
GLA_NORM_EPS = 1e-5
RWKV_HEAD_DIM = 64
RWKV_GN_EPS = 64e-5
TOP_K = 4
SWIGLU_ALPHA = 1.702
SWIGLU_LIMIT = 7.0
MOE_ROWS = 512
ISSUE_GROUP = 8
MERGE_PARTS = 2
PROJ_PARTS = 2

def _cparams(*sem):
    return pltpu.CompilerParams(dimension_semantics=sem, vmem_limit_bytes=VMEM_LIMIT)


def _rms(xf, gain):
    return xf * lax.rsqrt(jnp.mean(xf * xf, axis=-1, keepdims=True) + RMS_EPS) * gain


def _softplus(y):
    return jnp.maximum(y, 0.0) + jnp.log1p(jnp.exp(-jnp.abs(y)))


def _sigmoid(y):
    return 1.0 / (1.0 + jnp.exp(-y))


def _dot(a, b):
    return jnp.dot(a, b, preferred_element_type=F32)


def _dot_hi(a, b):
    return jnp.dot(a, b, preferred_element_type=F32, precision=HIGHEST)


def _pack_pair(lo, hi):
    bits = lambda t: lax.bitcast_convert_type(t.astype(BF16).astype(F32), U32)
    return (bits(lo) >> 16) | bits(hi)


def _pack_rows(x):
    n = x.shape[-1] // 2
    return _pack_pair(x[:, :n], x[:, n:])


def _unpack_rows(w):
    lo = lax.bitcast_convert_type(w << 16, F32)
    hi = lax.bitcast_convert_type(w & jnp.uint32(0xFFFF0000), F32)
    return lo, hi


def _store_packed(ref, x):
    packed = _pack_rows(x)
    for c in range(ref.shape[1]):
        ref[:, c, :] = packed[:, c * LANES:(c + 1) * LANES]


def _load_packed(ref):
    return _unpack_rows(jnp.concatenate([ref[:, c, :] for c in range(ref.shape[1])], axis=1))


def _gla_proj_kernel(x_ref, g_ref, w_ref, up_ref, bias_ref, q_ref, k_ref, v_ref, og_ref, la_ref):
    qk = q_ref.shape[-1]
    vw = v_ref.shape[-1]
    pm = x_ref.shape[1] // PROJ_PARTS
    parts = [slice(i * pm, (i + 1) * pm) for i in range(PROJ_PARTS)]
    zs = [_dot(_rms(x_ref[0, rows, :], g_ref[...]).astype(BF16), w_ref[...]) for rows in parts]
    for rows, z in zip(parts, zs):
        q_ref[0, rows, :] = z[:, :qk].astype(BF16)
        k_ref[0, rows, :] = z[:, qk:2 * qk].astype(BF16)
        v_ref[0, rows, :] = z[:, 2 * qk:2 * qk + vw].astype(BF16)
        og_ref[0, rows, :] = z[:, 2 * qk + vw:2 * qk + 2 * vw].astype(BF16)
        lr = z[:, 2 * qk + 2 * vw:].astype(BF16)
        pre = _dot(lr, up_ref[...]) + bias_ref[...]
        la_ref[0, rows, :] = -_softplus(-pre) * (1.0 / GLA_TAU)


def _gla_proj(x, gain, w, up, bias, tm):
    B, S, D = x.shape
    qk = bias.shape[-1]
    vw = (w.shape[1] - 2 * qk - LANES) // 2
    tok = lambda width: pl.BlockSpec((1, tm, width), lambda b, i: (b, i, 0))
    full = lambda a: pl.BlockSpec(a.shape, lambda b, i: (0,) * a.ndim)
    return pl.pallas_call(
        _gla_proj_kernel,
        grid=(B, S // tm),
        in_specs=[tok(D), full(gain), full(w), full(up), full(bias)],
        out_specs=[tok(qk), tok(qk), tok(vw), tok(vw), tok(qk)],
        out_shape=[jax.ShapeDtypeStruct((B, S, qk), BF16), jax.ShapeDtypeStruct((B, S, qk), BF16),
                   jax.ShapeDtypeStruct((B, S, vw), BF16), jax.ShapeDtypeStruct((B, S, vw), BF16),
                   jax.ShapeDtypeStruct((B, S, qk), F32)],
        compiler_params=_cparams("parallel", "parallel"),
        name="gla_proj",
    )(x, gain, w, up, bias)


def _gla_kernel(q_ref, k_ref, v_ref, og_ref, la_ref, ng_ref, y_ref, st_ref, qa_ref, ka_ref, qi_ref, ks_ref, dec_ref,
                sc_ref):
    C = GLA_CHUNK
    tc = q_ref.shape[1]

    @pl.when(pl.program_id(1) == 0)
    def _():
        st_ref[...] = jnp.zeros_like(st_ref)

    row = lax.broadcasted_iota(jnp.int32, (tc, tc), 0)
    col = lax.broadcasted_iota(jnp.int32, (tc, tc), 1)
    same_chunk_before = jnp.logical_and(row >= col, row // C == col // C)
    b_all = _dot_hi(same_chunk_before.astype(F32), la_ref[0])
    scale = GLA_DK ** -0.5
    for c in range(tc // C):
        rows = slice(c * C, (c + 1) * C)
        b = b_all[rows]
        b_mid = b[C // 2:C // 2 + 1]
        b_end = b[C - 1:C]
        q = q_ref[0, rows, :].astype(F32) * scale
        k = k_ref[0, rows, :].astype(F32)
        qa_ref[rows, :] = (q * jnp.exp(b - b_mid)).astype(BF16)
        ka_ref[rows, :] = (k * jnp.exp(b_mid - b)).astype(BF16)
        qi_ref[rows, :] = (q * jnp.exp(b)).astype(BF16)
        ks_ref[rows, :] = (k * jnp.exp(b_end - b)).astype(BF16)
        dec_ref[c:c + 1, :] = jnp.exp(b_end)

    r_c = lax.broadcasted_iota(jnp.int32, (C, C), 0)
    c_c = lax.broadcasted_iota(jnp.int32, (C, C), 1)
    causal = r_c >= c_c
    for c in range(tc // C):
        rows = slice(c * C, (c + 1) * C)
        for h in range(GLA_HEADS):
            kc = slice(h * GLA_DK, (h + 1) * GLA_DK)
            scores = lax.dot_general(qa_ref[rows, kc], ka_ref[rows, kc], (((1,), (1,)), ((), ())),
                                     preferred_element_type=F32)
            sc_ref[rows, h * C:(h + 1) * C] = jnp.where(causal, scores, 0.0).astype(BF16)

    for c in range(tc // C):
        rows = slice(c * C, (c + 1) * C)
        for h in range(GLA_HEADS):
            kc = slice(h * GLA_DK, (h + 1) * GLA_DK)
            vc = slice(h * GLA_DV, (h + 1) * GLA_DV)
            vh = v_ref[0, rows, vc]
            st = st_ref[h]
            o = _dot(sc_ref[rows, h * C:(h + 1) * C], vh) + lax.dot_general(
                qi_ref[rows, kc], st.astype(BF16), (((1,), (1,)), ((), ())), preferred_element_type=F32)
            st_ref[h] = st * dec_ref[c:c + 1, kc] + lax.dot_general(
                vh, ks_ref[rows, kc], (((0,), (0,)), ((), ())), preferred_element_type=F32)
            o = o * lax.rsqrt(jnp.mean(o * o, axis=-1, keepdims=True) + GLA_NORM_EPS) * ng_ref[...]
            og = og_ref[0, rows, vc].astype(F32)
            y_ref[0, rows, vc] = (o * (og * _sigmoid(og))).astype(BF16)


def _gla(q, k, v, og, la, norm_g, tc):
    B, S, qk = q.shape
    vw = v.shape[-1]
    tok = lambda width: pl.BlockSpec((1, tc, width), lambda b, i: (b, i, 0))
    return pl.pallas_call(
        _gla_kernel,
        grid=(B, S // tc),
        in_specs=[tok(qk), tok(qk), tok(vw), tok(vw), tok(qk),
                  pl.BlockSpec(norm_g.shape, lambda b, i: (0, 0))],
        out_specs=tok(vw),
        out_shape=jax.ShapeDtypeStruct((B, S, vw), BF16),
        scratch_shapes=[pltpu.VMEM((GLA_HEADS, GLA_DV, GLA_DK), F32)] + [pltpu.VMEM((tc, qk), BF16)] * 4
        + [pltpu.VMEM((8, qk), F32), pltpu.VMEM((tc, GLA_HEADS * GLA_CHUNK), BF16)],
        compiler_params=_cparams("parallel", "arbitrary"),
        name="gla_chunk",
    )(q, k, v, og, la, norm_g)


def _rwkv_proj_kernel(x_ref, g_ref, w_ref, mu_ref, w0_ref, w2_ref, a0_ref, a2_ref, g2_ref,
                      rv_ref, w_out_ref, ka_ref, gate_ref, carry_ref):
    W = rv_ref.shape[-1]
    tm = x_ref.shape[1]

    @pl.when(pl.program_id(1) == 0)
    def _():
        carry_ref[...] = jnp.zeros_like(carry_ref)

    pm = tm // PROJ_PARTS
    parts = [slice(i * pm, (i + 1) * pm) for i in range(PROJ_PARTS)]
    zs = [_dot(_rms(x_ref[0, rows, :], g_ref[...]).astype(BF16), w_ref[...]) for rows in parts]
    first = lax.broadcasted_iota(jnp.int32, zs[0].shape, 0) == 0
    last_row = carry_ref[0:1, :]
    for rows, z in zip(parts, zs):
        prev = jnp.where(first, last_row, pltpu.roll(z, 1, axis=0))
        last_row = z[pm - 1:pm, :]
        u = z + mu_ref[...] * (prev - z)

        r = u[:, :W]
        k = u[:, W:2 * W]
        v = u[:, 2 * W:3 * W]
        lora = u[:, 3 * W:3 * W + LANES]
        xg = u[:, 3 * W + LANES:]
        w_log = -_softplus(-(w0_ref[...] + _dot(jnp.tanh(lora).astype(BF16), w2_ref[...]))) - 0.5
        iclr = _sigmoid(a0_ref[...] + _dot(lora.astype(BF16), a2_ref[...]))
        rv_ref[0, rows, :] = _pack_pair(r, v)
        ka_ref[0, rows, :] = _pack_pair(k, iclr)
        w_out_ref[0, rows, :] = _pack_rows(-jnp.exp(w_log))
        gate_ref[0, rows, :] = _dot(_sigmoid(xg).astype(BF16), g2_ref[...]).astype(BF16)
    carry_ref[0:1, :] = last_row


def _rwkv_proj(x, gain, w, mu, w0, w2p, a0, a2p, g2, tm):
    B, S, D = x.shape
    W = w0.shape[-1]
    tok = lambda width: pl.BlockSpec((1, tm, width), lambda b, i: (b, i, 0))
    full = lambda a: pl.BlockSpec(a.shape, lambda b, i: (0,) * a.ndim)
    consts = (gain, w, mu, w0, w2p, a0, a2p, g2)
    return pl.pallas_call(
        _rwkv_proj_kernel,
        grid=(B, S // tm),
        in_specs=[tok(D)] + [full(c) for c in consts],
        out_specs=[tok(W), tok(W // 2), tok(W), tok(W)],
        out_shape=[jax.ShapeDtypeStruct((B, S, W), U32), jax.ShapeDtypeStruct((B, S, W // 2), U32),
                   jax.ShapeDtypeStruct((B, S, W), U32), jax.ShapeDtypeStruct((B, S, W), BF16)],
        scratch_shapes=[pltpu.VMEM((8, w.shape[1]), F32)],
        compiler_params=_cparams("parallel", "arbitrary"),
        name="rwkv_proj",
    )(x, *consts)


def _rwkv_scan_kernel(rv_ref, ld_ref, kia_ref, kian_ref, kk_ref, ka_ref, rk_ref, lnw_ref, lnb_ref,
                      y_ref, st_ref, sa_ref, av_ref, bv_ref, km_ref, r_ref, v_ref, yr_ref, w_ref):
    tt, G, N, _ = rv_ref.shape
    half = N // 2

    def step_vectors(packed):
        k_raw, iclr = _unpack_rows(packed)
        kk = k_raw * kk_ref[...]
        kk = kk / jnp.maximum(jnp.sqrt(jnp.sum(kk * kk, axis=0, keepdims=True)), 1e-12)
        return -kk, kk * iclr, k_raw * (1.0 + (iclr - 1.0) * ka_ref[...])

    @pl.when(pl.program_id(0) == 0)
    def _():
        st_ref[...] = jnp.zeros_like(st_ref)
        sa_ref[...] = jnp.zeros_like(sa_ref)
        for g in range(G):
            _, bv_ref[0, g], km_ref[0, g] = step_vectors(kia_ref[0, g])

    t0 = pl.program_id(0) * tt

    def group_step(tg, carry):
        t = tg // G
        g = tg % G
        cur = (t0 + t) % 2
        last = t + 1 >= tt
        t_nx = jnp.minimum(t + 1, tt - 1)
        av_ref[...], bv_ref[1 - cur, g], km_ref[1 - cur, g] = step_vectors(
            jnp.where(last, kian_ref[0, g], kia_ref[t_nx, g]))
        r_ref[...], v_ref[...] = _unpack_rows(rv_ref[t, g])
        w_ref[...] = jnp.exp(jnp.concatenate(_unpack_rows(ld_ref[t, g]), axis=0))

        def half_step(ih, c):
            rows = slice(ih * half, (ih + 1) * half)
            sa_h = sa_ref[g, rows, :]
            vt = v_ref[rows, :]
            y = jnp.zeros((half, LANES), F32)
            sa_nx = jnp.zeros((half, LANES), F32)
            for j in range(N):
                row = slice(j, j + 1)
                s_new = (st_ref[g, j, rows, :] * w_ref[row, :] + sa_h * bv_ref[cur, g, row, :]
                         + vt * km_ref[cur, g, row, :])
                st_ref[g, j, rows, :] = s_new
                y = y + s_new * r_ref[row, :]
                sa_nx = sa_nx + s_new * av_ref[row, :]
            yr_ref[rows, :] = y
            sa_ref[g, rows, :] = sa_nx
            return c

        for ih in range(2):
            half_step(ih, 0)

        y = yr_ref[...]
        mean = jnp.mean(y, axis=0, keepdims=True)
        d = y - mean
        var = jnp.mean(d * d, axis=0, keepdims=True)
        bonus = jnp.sum(r_ref[...] * km_ref[cur, g] * rk_ref[...], axis=0, keepdims=True)
        out = d * lax.rsqrt(var + RWKV_GN_EPS) * lnw_ref[...] + lnb_ref[...] + bonus * v_ref[...]
        y_ref[t, g] = _pack_pair(out[:half], out[half:])
        return carry

    lax.fori_loop(0, tt * G, group_step, 0)


def _rwkv_scan(rv, log_decay, kia, k_k, k_a, r_k, ln_w, ln_b, tt):
    S, G, N, _ = rv.shape
    blk = pl.BlockSpec((tt, G, N, LANES), lambda i: (i, 0, 0, 0))
    pair = pl.BlockSpec((tt, G, N // 2, LANES), lambda i: (i, 0, 0, 0))
    nxt = pl.BlockSpec((1, G, N, LANES), lambda i: (jnp.minimum((i + 1) * tt, S - 1), 0, 0, 0))
    par = pl.BlockSpec((N, LANES), lambda i: (0, 0))
    tile = pltpu.VMEM((N, LANES), F32)
    per_g = pltpu.VMEM((G, N, LANES), F32)
    two_g = pltpu.VMEM((2, G, N, LANES), F32)
    return pl.pallas_call(
        _rwkv_scan_kernel,
        grid=(S // tt,),
        in_specs=[blk, pair, blk, nxt] + [par] * 5,
        out_specs=pair,
        out_shape=jax.ShapeDtypeStruct((S, G, N // 2, LANES), U32),
        scratch_shapes=[pltpu.VMEM((G, N, N, LANES), F32), per_g, tile, two_g, two_g, tile, tile, tile, tile],
        compiler_params=_cparams("arbitrary"),
        name="rwkv_scan",
    )(rv, log_decay, kia, kia, k_k, k_a, r_k, ln_w, ln_b)


def _merge_kernel(x_ref, ya_ref, ys_ref, gate_ref, g_mix_ref, wg_ref, wa_ref, wb_ref, wo_ref, g_ffn_ref,
                  rwh_ref, rwl_ref, rb_ref,
                  x1_ref, h2_ref, e_ref, rank_ref, p_ref, cnt_ref, carry_ref, *, n_experts):
    tm, D = x_ref.shape

    @pl.when(pl.program_id(0) == 0)
    def _():
        carry_ref[...] = jnp.zeros_like(carry_ref)

    hm = tm // MERGE_PARTS
    lane = lax.broadcasted_iota(jnp.int32, (hm, LANES), 1)

    def mix(rows):
        x = x_ref[rows, :]
        h = _rms(x, g_mix_ref[...]).astype(BF16)
        gates = _sigmoid(_dot(h, wg_ref[...]))
        ys = jnp.concatenate(_unpack_rows(ys_ref[rows, :]), axis=1)
        yb = (ys * gate_ref[rows, :].astype(F32)).astype(BF16)
        merged = gates[:, :D] * _dot(ya_ref[rows, :], wa_ref[...]) + gates[:, D:] * _dot(yb, wb_ref[...])
        x1 = x + _dot(merged.astype(BF16), wo_ref[...])
        x1_ref[rows, :] = x1
        h2 = _rms(x1, g_ffn_ref[...])
        _store_packed(h2_ref.at[rows], h2)
        h_hi = h2.astype(BF16)
        h_lo = (h2 - h_hi.astype(F32)).astype(BF16)
        logits = _dot(h_hi, rwh_ref[...]) + _dot(h_lo, rwh_ref[...]) + _dot(h_hi, rwl_ref[...]) + rb_ref[...]
        return jnp.where(lane < n_experts, logits, -jnp.inf)

    def route(rows, logits, seen):
        vals, idxs, hots = [], [], []
        for _ in range(TOP_K):
            m = jnp.max(logits, axis=-1, keepdims=True)
            idx = jnp.min(jnp.where(logits == m, lane, LANES), axis=-1, keepdims=True)
            hot = lane == idx
            vals.append(m)
            idxs.append(idx)
            hots.append(hot)
            logits = jnp.where(hot, -jnp.inf, logits)
        exps = [jnp.exp(val - vals[0]) for val in vals]
        denom = functools.reduce(lambda s, e: s + e, exps)
        chosen = functools.reduce(lambda s, e: s + e, [hot.astype(F32) for hot in hots])
        r_i = lax.broadcasted_iota(jnp.int32, (hm, hm), 0)
        c_i = lax.broadcasted_iota(jnp.int32, (hm, hm), 1)
        before = _dot((r_i > c_i).astype(BF16), chosen.astype(BF16)) + seen
        e_out = jnp.zeros((hm, LANES), jnp.int32)
        rank_out = jnp.zeros((hm, LANES), jnp.int32)
        p_out = jnp.zeros((hm, LANES), F32)
        for s in range(TOP_K):
            rank = jnp.sum(jnp.where(hots[s], before, 0.0), axis=-1, keepdims=True)
            e_out = jnp.where(lane == s, idxs[s], e_out)
            rank_out = jnp.where(lane == s, rank.astype(jnp.int32), rank_out)
            p_out = jnp.where(lane == s, exps[s] / denom, p_out)
        e_ref[rows, :] = e_out
        rank_ref[rows, :] = rank_out
        p_ref[rows, :] = p_out
        return seen + jnp.sum(chosen, axis=0, keepdims=True)

    parts = [slice(i * hm, (i + 1) * hm) for i in range(MERGE_PARTS)]
    logits = [mix(rows) for rows in parts]
    seen = carry_ref[0:1, :]
    for rows, lg in zip(parts, logits):
        seen = route(rows, lg, seen)
    carry_ref[0:1, :] = seen
    cnt_ref[...] = jnp.broadcast_to(seen, cnt_ref.shape).astype(jnp.int32)


def _merge(x, ya, ys, gate, consts, n_experts, tm):
    T, D = x.shape
    W = gate.shape[-1]
    tok = lambda width: pl.BlockSpec((tm, width), lambda i: (i, 0))
    full = lambda a: pl.BlockSpec(a.shape, lambda i: (0,) * a.ndim)
    return pl.pallas_call(
        functools.partial(_merge_kernel, n_experts=n_experts),
        grid=(T // tm,),
        in_specs=[tok(D), tok(ya.shape[-1]), tok(ys.shape[-1]), tok(W)] + [full(c) for c in consts],
        out_specs=[tok(D), pl.BlockSpec((tm, D // 2 // LANES, LANES), lambda i: (i, 0, 0)),
                   tok(LANES), tok(LANES), tok(LANES), pl.BlockSpec((8, LANES), lambda i: (0, 0))],
        out_shape=[jax.ShapeDtypeStruct((T, D), F32), jax.ShapeDtypeStruct((T, D // 2 // LANES, LANES), U32),
                   jax.ShapeDtypeStruct((T, LANES), jnp.int32), jax.ShapeDtypeStruct((T, LANES), jnp.int32),
                   jax.ShapeDtypeStruct((T, LANES), F32), jax.ShapeDtypeStruct((8, LANES), jnp.int32)],
        scratch_shapes=[pltpu.VMEM((8, LANES), F32)],
        compiler_params=_cparams("arbitrary"),
        name="merge_router",
    )(x, ya, ys, gate, *consts)


def _idx_copy(dest_ref, idx_ref, sem, tile, n_idx):
    slot = tile % 2
    return pltpu.make_async_copy(dest_ref.at[pl.ds(pl.multiple_of(tile * n_idx, n_idx), n_idx)],
                                 idx_ref.at[pl.ds(pl.multiple_of(slot * n_idx, n_idx), n_idx)], sem.at[slot])


def _dispatch_kernel(dest_ref, rows_ref, init_ref, xs_ref, idx_ref, idx_sem, row_sem, *, tm):
    del init_ref
    i = pl.program_id(0)
    n = pl.num_programs(0)
    n_idx = tm * TOP_K
    idx_copy = functools.partial(_idx_copy, dest_ref, idx_ref, idx_sem, n_idx=n_idx)

    @pl.when(i == 0)
    def _():
        idx_copy(i).start()

    idx_copy(i).wait()

    @pl.when(i + 1 < n)
    def _():
        idx_copy(i + 1).start()

    base = (i % 2) * n_idx

    def issue(g, c):
        r0 = g * ISSUE_GROUP
        dst = [idx_ref[base + r0 * TOP_K + k] for k in range(ISSUE_GROUP * TOP_K)]
        for k, d in enumerate(dst):
            pltpu.make_async_copy(rows_ref.at[r0 + k // TOP_K], xs_ref.at[d], row_sem).start(priority=k % 2)
        return c

    lax.fori_loop(0, tm // ISSUE_GROUP, issue, 0)

    for _ in range(TOP_K):
        pltpu.make_async_copy(rows_ref, xs_ref.at[pl.ds(0, tm)], row_sem).wait()


def _dispatch(dest, rows, n_rows, tm):
    T = rows.shape[0]
    packed = (n_rows,) + rows.shape[1:]
    any_spec = pl.BlockSpec(memory_space=pl.ANY)
    return pl.pallas_call(
        functools.partial(_dispatch_kernel, tm=tm),
        grid=(T // tm,),
        in_specs=[any_spec, pl.BlockSpec((tm,) + rows.shape[1:], lambda i: (i, 0, 0)), any_spec],
        out_specs=any_spec,
        out_shape=jax.ShapeDtypeStruct(packed, U32),
        input_output_aliases={2: 0},
        scratch_shapes=[pltpu.SMEM((2 * tm * TOP_K,), jnp.int32), pltpu.SemaphoreType.DMA((2,)),
                        pltpu.SemaphoreType.DMA],
        compiler_params=_cparams("arbitrary"),
        name="moe_dispatch",
    )(dest, rows, jnp.zeros(packed, U32))


def _split_w1_kernel(w_ref, glu_ref, lin_ref):
    cols = w_ref.shape[-1]
    src = lax.broadcasted_iota(jnp.int32, (cols, cols // 2), 0)
    dst = lax.broadcasted_iota(jnp.int32, (cols, cols // 2), 1)
    w = w_ref[0].astype(BF16)
    glu_ref[0] = _dot(w, (src == 2 * dst).astype(BF16)).astype(BF16)
    lin_ref[0] = _dot(w, (src == 2 * dst + 1).astype(BF16)).astype(BF16)


def _split_w1(w1, cols=512):
    E, D, FF2 = w1.shape
    out = jax.ShapeDtypeStruct((E, D, FF2 // 2), BF16)
    return pl.pallas_call(
        _split_w1_kernel,
        grid=(E, FF2 // cols),
        in_specs=[pl.BlockSpec((1, D, cols), lambda e, c: (e, 0, c))],
        out_specs=[pl.BlockSpec((1, D, cols // 2), lambda e, c: (e, 0, c))] * 2,
        out_shape=[out, out],
        compiler_params=_cparams("parallel", "parallel"),
        name="split_w1",
    )(w1)


def _expert_kernel(be_ref, nu_ref, x_ref, w1g_ref, w1l_ref, b1g_ref, b1l_ref, w2_ref, b2_ref, o_ref):
    del be_ref
    used = pl.program_id(0) < nu_ref[0]

    @pl.when(used)
    def _():
        lo, hi = _load_packed(x_ref)
        x = jnp.concatenate([lo.astype(BF16), hi.astype(BF16)], axis=1)
        glu = jnp.minimum(_dot(x, w1g_ref[0]) + b1g_ref[0], SWIGLU_LIMIT)
        lin = jnp.clip(_dot(x, w1l_ref[0]) + b1l_ref[0], -SWIGLU_LIMIT, SWIGLU_LIMIT)
        act = glu * _sigmoid(SWIGLU_ALPHA * glu) * (lin + 1.0)
        _store_packed(o_ref, _dot(act.astype(BF16), w2_ref[0]) + b2_ref[0])

    @pl.when(jnp.logical_not(used))
    def _():
        o_ref[...] = jnp.zeros_like(o_ref)


def _experts(block_expert, n_used, xs, w1g, w1l, b1g, b1l, w2, b2):
    R = xs.shape[0]
    D, FF = w1g.shape[1:]
    nb = R // MOE_ROWS
    per_e = lambda shape: pl.BlockSpec((1,) + shape, lambda i, be, nu: (be[i], 0, 0))
    rows = pl.BlockSpec((MOE_ROWS,) + xs.shape[1:], lambda i, be, nu: (i, 0, 0))
    return pl.pallas_call(
        _expert_kernel,
        grid_spec=pltpu.PrefetchScalarGridSpec(
            num_scalar_prefetch=2,
            grid=(nb,),
            in_specs=[rows, per_e((D, FF)), per_e((D, FF)), per_e((1, FF)), per_e((1, FF)),
                      per_e((FF, D)), per_e((1, D))],
            out_specs=rows,
        ),
        out_shape=jax.ShapeDtypeStruct(xs.shape, U32),
        compiler_params=_cparams("arbitrary"),
        name="expert_ffn",
    )(block_expert, n_used, xs, w1g, w1l, b1g, b1l, w2, b2)


def _combine_kernel(dest_ref, rows_ref, x1_ref, p_ref, g_ref, o_ref, buf_ref, idx_ref, idx_sem, row_sem):
    tm, D = x1_ref.shape
    i = pl.program_id(0)
    n = pl.num_programs(0)
    n_idx = tm * TOP_K
    idx_copy = functools.partial(_idx_copy, dest_ref, idx_ref, idx_sem, n_idx=n_idx)

    def gather(tile):
        slot = tile % 2
        base = slot * n_idx

        def issue(g, c):
            r0 = g * ISSUE_GROUP
            src = [idx_ref[base + r0 * TOP_K + k] for k in range(ISSUE_GROUP * TOP_K)]
            for k, d in enumerate(src):
                pltpu.make_async_copy(rows_ref.at[d], buf_ref.at[slot, k % TOP_K, r0 + k // TOP_K],
                                      row_sem.at[slot]).start(priority=k % 2)
            return c

        lax.fori_loop(0, tm // ISSUE_GROUP, issue, 0)

    @pl.when(i == 0)
    def _():
        idx_copy(i).start()
        idx_copy(i).wait()
        gather(i)

        @pl.when(n > 1)
        def _():
            idx_copy(i + 1).start()

    @pl.when(i + 1 < n)
    def _():
        idx_copy(i + 1).wait()
        gather(i + 1)

    @pl.when(i + 2 < n)
    def _():
        idx_copy(i + 2).start()

    slot = i % 2
    for s in range(TOP_K):
        pltpu.make_async_copy(rows_ref.at[pl.ds(0, tm)], buf_ref.at[slot, s], row_sem.at[slot]).wait()

    half = D // 2
    x1 = x1_ref[...]
    acc_lo, acc_hi = x1[:, :half], x1[:, half:]
    p = p_ref[...]
    for s in range(TOP_K):
        lo, hi = _load_packed(buf_ref.at[slot, s])
        acc_lo = acc_lo + p[:, s:s + 1] * lo
        acc_hi = acc_hi + p[:, s:s + 1] * hi
    ms = (jnp.sum(acc_lo * acc_lo, axis=-1, keepdims=True) + jnp.sum(acc_hi * acc_hi, axis=-1, keepdims=True)) / D
    scale = lax.rsqrt(ms + RMS_EPS)
    o_ref[:, :half] = acc_lo * scale * g_ref[:, :half]
    o_ref[:, half:] = acc_hi * scale * g_ref[:, half:]


def _combine(dest, rows, x1, p, gain, tm):
    T, D = x1.shape
    any_spec = pl.BlockSpec(memory_space=pl.ANY)
    return pl.pallas_call(
        _combine_kernel,
        grid=(T // tm,),
        in_specs=[any_spec, any_spec, pl.BlockSpec((tm, D), lambda i: (i, 0)),
                  pl.BlockSpec((tm, LANES), lambda i: (i, 0)), pl.BlockSpec(gain.shape, lambda i: (0, 0))],
        out_specs=pl.BlockSpec((tm, D), lambda i: (i, 0)),
        out_shape=jax.ShapeDtypeStruct((T, D), F32),
        scratch_shapes=[pltpu.VMEM((2, TOP_K, tm) + rows.shape[1:], U32), pltpu.SMEM((2 * tm * TOP_K,), jnp.int32),
                        pltpu.SemaphoreType.DMA((2,)), pltpu.SemaphoreType.DMA((2,))],
        compiler_params=_cparams("arbitrary"),
        name="moe_combine_norm",
    )(dest, rows, x1, p, gain)


def _pick_tile(n, pref):
    t = min(n, pref)
    assert n % t == 0, (n, t)
    return t


def kernel(x, norm_mix_g, w_in, gla_gate_up, gla_gate_bias, gla_norm_g, rwkv_mu, rwkv_w0, rwkv_w2, rwkv_a0, rwkv_a2, rwkv_g2, rwkv_k_k, rwkv_k_a, rwkv_r_k, rwkv_ln_w, rwkv_ln_b, w_branch_a, w_branch_b, w_out, norm_ffn_g, router_w, router_b, expert_w1, expert_b1, expert_w2, expert_b2, norm_final_g):
    assert norm_mix_g.shape[0] == 1, "single-layer block"
    B, S, D = x.shape
    T = B * S
    qk = gla_gate_bias.shape[-1]
    vw = GLA_HEADS * GLA_DV
    W = rwkv_w0.shape[-1]
    H = W // RWKV_HEAD_DIM
    E = router_w.shape[-1]
    assert qk == GLA_HEADS * GLA_DK and B * H % LANES == 0 and E <= LANES
    row = lambda a: a.reshape(1, -1)

    wi = w_in[0]
    gla_cols = 2 * qk + vw + GLA_GATE_RANK + vw
    rwkv_cols = rwkv_mu.shape[-1]
    o_q, o_k, o_v, o_lr, o_og = 0, qk, 2 * qk, 2 * qk + vw, 2 * qk + vw + GLA_GATE_RANK
    w_gla = jnp.concatenate(
        [wi[:, o_q:o_lr], wi[:, o_og:gla_cols], wi[:, o_lr:o_og],
         jnp.zeros((D, LANES - GLA_GATE_RANK), F32)], axis=1).astype(BF16)
    up = jnp.concatenate([gla_gate_up[0], jnp.zeros((LANES - GLA_GATE_RANK, qk), F32)], axis=0).astype(BF16)
    w_rwkv = wi[:, gla_cols:gla_cols + rwkv_cols].astype(BF16)
    w_gate = wi[:, gla_cols + rwkv_cols:].astype(BF16)
    n_w, n_a = rwkv_w2.shape[1], rwkv_a2.shape[1]
    assert n_w + n_a == LANES
    w2p = jnp.concatenate([rwkv_w2[0], jnp.zeros((n_a, W), F32)], axis=0).astype(BF16)
    a2p = jnp.concatenate([jnp.zeros((n_w, W), F32), rwkv_a2[0]], axis=0).astype(BF16)

    tm_gla = _pick_tile(S, 512)
    q, k, v, og, la = _gla_proj(x, row(norm_mix_g), w_gla, up, row(gla_gate_bias), tm_gla)
    ya = _gla(q, k, v, og, la, row(gla_norm_g), _pick_tile(S, 256))

    tm_r = _pick_tile(S, 512)
    hn = RWKV_HEAD_DIM // 2
    col = jnp.arange(W)
    scan_order = (col % (W // 2)) // hn * RWKV_HEAD_DIM + col // (W // 2) * hn + col % hn
    rv_, ld_, kia_, gate_ = _rwkv_proj(x, row(norm_mix_g), w_rwkv, row(rwkv_mu), row(rwkv_w0)[:, scan_order],
                                       w2p[:, scan_order], row(rwkv_a0), a2p,
                                       rwkv_g2[0][:, scan_order].astype(BF16), tm_r)
    G = B * H // LANES
    bg = B // G
    to_scan = lambda t, n: t.reshape(G, bg, S, H, n).transpose(2, 0, 4, 1, 3).reshape(S, G, n, LANES)
    par = lambda p: jnp.tile(p.reshape(H, RWKV_HEAD_DIM).T, (1, bg))
    ys = _rwkv_scan(to_scan(rv_, RWKV_HEAD_DIM), to_scan(ld_, hn), to_scan(kia_, RWKV_HEAD_DIM),
                    par(rwkv_k_k), par(rwkv_k_a), par(rwkv_r_k), par(rwkv_ln_w), par(rwkv_ln_b), _pick_tile(S, 32))
    ys = ys.reshape(S, G, hn, bg, H).transpose(1, 3, 0, 4, 2).reshape(T, W // 2)

    rw = jnp.concatenate([router_w[0], jnp.zeros((D, LANES - E), F32)], axis=1)
    rb = jnp.concatenate([router_b[0], jnp.zeros((LANES - E,), F32)]).reshape(1, LANES)
    rw_hi = rw.astype(BF16)
    rw_lo = (rw - rw_hi.astype(F32)).astype(BF16)
    consts = (row(norm_mix_g), w_gate, w_branch_a[0].astype(BF16), w_branch_b[0][scan_order, :].astype(BF16),
              w_out[0].astype(BF16), row(norm_ffn_g), rw_hi, rw_lo, rb)
    tm_m = _pick_tile(T, 512)
    x1, h2, e_sel, rank, p_sel, counts = _merge(
        x.reshape(T, D), ya.reshape(T, vw), ys, gate_.reshape(T, W), consts, E, tm_m)

    counts = counts[0, :E]
    padded = (counts + MOE_ROWS - 1) // MOE_ROWS * MOE_ROWS
    pad_ends = jnp.cumsum(padded)
    pad_starts = pad_ends - padded
    nb = -(-T * TOP_K // MOE_ROWS) + E
    dest = pad_starts[e_sel[:, :TOP_K]] + rank[:, :TOP_K]
    block_start = jnp.arange(nb, dtype=jnp.int32) * MOE_ROWS
    block_expert = jnp.minimum(jnp.sum(pad_ends[None, :] <= block_start[:, None], axis=1), E - 1).astype(jnp.int32)
    n_used = (pad_ends[-1:] // MOE_ROWS).astype(jnp.int32)
    dest = dest.reshape(-1).astype(jnp.int32)
    tm_d = _pick_tile(T, 256)
    xs = _dispatch(dest, h2, nb * MOE_ROWS, _pick_tile(T, 1024))

    FF = expert_w2.shape[2]
    w1g, w1l = _split_w1(expert_w1[0])
    b1 = expert_b1[0].reshape(E, 1, FF, 2)
    outs_e = _experts(block_expert, n_used, xs, w1g, w1l, b1[..., 0], b1[..., 1],
                      expert_w2[0].astype(BF16), expert_b2[0].reshape(E, 1, D))

    out = _combine(dest, outs_e, x1, p_sel, row(norm_final_g), tm_d)
    return out.reshape(B, S, D)
```

```python
import functools

import jax
import jax.numpy as jnp
from jax import lax
from jax.experimental import pallas as pl
from jax.experimental.pallas import tpu as pltpu

F32 = jnp.float32
BF16 = jnp.bfloat16
U32 = jnp.uint32

LANES = 128
VMEM_LIMIT = 56 * 1024 * 1024

RMS_EPS = 1e-6
GLA_HEADS = 4
GLA_DK = 128
GLA_DV = 256
GLA_GATE_RANK = 16
GLA_TAU = 16.0
GLA_CHUNK = 64
GLA_NORM_EPS = 1e-5
RWKV_HEAD_DIM = 64
RWKV_GN_EPS = 64e-5
TOP_K = 4
SWIGLU_ALPHA = 1.702
SWIGLU_LIMIT = 7.0
MOE_ROWS = 512
ISSUE_GROUP = 8
MERGE_PARTS = 2
PROJ_PARTS = 2

def _cparams(*sem):
    return pltpu.CompilerParams(dimension_semantics=sem, vmem_limit_bytes=VMEM_LIMIT)


def _rms(xf, gain):
    return xf * lax.rsqrt(jnp.mean(xf * xf, axis=-1, keepdims=True) + RMS_EPS) * gain


def _softplus(y):
    return jnp.maximum(y, 0.0) + jnp.log1p(jnp.exp(-jnp.abs(y)))


def _sigmoid(y):
    return 1.0 / (1.0 + jnp.exp(-y))


def _dot(a, b):
    return jnp.dot(a, b, preferred_element_type=F32)


def _pack_pair(lo, hi):
    bits = lambda t: lax.bitcast_convert_type(t.astype(BF16).astype(F32), U32)
    return (bits(lo) >> 16) | bits(hi)


def _pack_rows(x):
    n = x.shape[-1] // 2
    return _pack_pair(x[:, :n], x[:, n:])


def _unpack_rows(w):
    lo = lax.bitcast_convert_type(w << 16, F32)
    hi = lax.bitcast_convert_type(w & jnp.uint32(0xFFFF0000), F32)
    return lo, hi


def _store_packed(ref, x):
    packed = _pack_rows(x)
    for c in range(ref.shape[1]):
        ref[:, c, :] = packed[:, c * LANES:(c + 1) * LANES]


def _load_packed(ref):
    return _unpack_rows(jnp.concatenate([ref[:, c, :] for c in range(ref.shape[1])], axis=1))


def _gla_proj_kernel(x_ref, g_ref, w_ref, up_ref, bias_ref, q_ref, k_ref, v_ref, og_ref, la_ref):
    qk = q_ref.shape[-1]
    vw = v_ref.shape[-1]
    pm = x_ref.shape[1] // PROJ_PARTS
    parts = [slice(i * pm, (i + 1) * pm) for i in range(PROJ_PARTS)]
    zs = [_dot(_rms(x_ref[0, rows, :], g_ref[...]).astype(BF16), w_ref[...]) for rows in parts]
    for rows, z in zip(parts, zs):
        q_ref[0, rows, :] = z[:, :qk].astype(BF16)
        k_ref[0, rows, :] = z[:, qk:2 * qk].astype(BF16)
        v_ref[0, rows, :] = z[:, 2 * qk:2 * qk + vw].astype(BF16)
        og_ref[0, rows, :] = z[:, 2 * qk + vw:2 * qk + 2 * vw].astype(BF16)
        lr = z[:, 2 * qk + 2 * vw:].astype(BF16)
        pre = _dot(lr, up_ref[...]) + bias_ref[...]
        la_ref[0, rows, :] = -_softplus(-pre) * (1.0 / GLA_TAU)


def _gla_proj(x, gain, w, up, bias, tm):
    B, S, D = x.shape
    qk = bias.shape[-1]
    vw = (w.shape[1] - 2 * qk - LANES) // 2
    tok = lambda width: pl.BlockSpec((1, tm, width), lambda b, i: (b, i, 0))
    full = lambda a: pl.BlockSpec(a.shape, lambda b, i: (0,) * a.ndim)
    return pl.pallas_call(
        _gla_proj_kernel,
        grid=(B, S // tm),
        in_specs=[tok(D), full(gain), full(w), full(up), full(bias)],
        out_specs=[tok(qk), tok(qk), tok(vw), tok(vw), tok(qk)],
        out_shape=[jax.ShapeDtypeStruct((B, S, qk), BF16), jax.ShapeDtypeStruct((B, S, qk), BF16),
                   jax.ShapeDtypeStruct((B, S, vw), BF16), jax.ShapeDtypeStruct((B, S, vw), BF16),
                   jax.ShapeDtypeStruct((B, S, qk), F32)],
        compiler_params=_cparams("parallel", "parallel"),
        name="gla_proj",
    )(x, gain, w, up, bias)


def _gla_kernel(q_ref, k_ref, v_ref, og_ref, la_ref, ng_ref, y_ref, st_ref, qa_ref, ka_ref, qi_ref, ks_ref, dec_ref,
                sc_ref):
    C = GLA_CHUNK
    tc = q_ref.shape[1]

    @pl.when(pl.program_id(1) == 0)
    def _():
        st_ref[...] = jnp.zeros_like(st_ref)

    row = lax.broadcasted_iota(jnp.int32, (tc, tc), 0)
    col = lax.broadcasted_iota(jnp.int32, (tc, tc), 1)
    same_chunk_before = jnp.logical_and(row >= col, row // C == col // C)
    tri = same_chunk_before.astype(BF16)
    la = la_ref[0]
    b_all = jnp.zeros_like(la)
    for _ in range(3):
        term = la.astype(BF16)
        b_all = b_all + _dot(tri, term)
        la = la - term.astype(F32)
    scale = GLA_DK ** -0.5
    for c in range(tc // C):
        rows = slice(c * C, (c + 1) * C)
        b = b_all[rows]
        b_mid = b[C // 2:C // 2 + 1]
        b_end = b[C - 1:C]
        q = q_ref[0, rows, :].astype(F32) * scale
        k = k_ref[0, rows, :].astype(F32)
        qa_ref[rows, :] = (q * jnp.exp(b - b_mid)).astype(BF16)
        ka_ref[rows, :] = (k * jnp.exp(b_mid - b)).astype(BF16)
        qi_ref[rows, :] = (q * jnp.exp(b)).astype(BF16)
        ks_ref[rows, :] = (k * jnp.exp(b_end - b)).astype(BF16)
        dec_ref[c:c + 1, :] = jnp.exp(b_end)

    r_c = lax.broadcasted_iota(jnp.int32, (C, C), 0)
    c_c = lax.broadcasted_iota(jnp.int32, (C, C), 1)
    causal = r_c >= c_c
    for c in range(tc // C):
        rows = slice(c * C, (c + 1) * C)
        for h in range(GLA_HEADS):
            kc = slice(h * GLA_DK, (h + 1) * GLA_DK)
            scores = lax.dot_general(qa_ref[rows, kc], ka_ref[rows, kc], (((1,), (1,)), ((), ())),
                                     preferred_element_type=F32)
            sc_ref[rows, h * C:(h + 1) * C] = jnp.where(causal, scores, 0.0).astype(BF16)

    for c in range(tc // C):
        rows = slice(c * C, (c + 1) * C)
        for h in range(GLA_HEADS):
            kc = slice(h * GLA_DK, (h + 1) * GLA_DK)
            vc = slice(h * GLA_DV, (h + 1) * GLA_DV)
            vh = v_ref[0, rows, vc]
            st = st_ref[h]
            o = _dot(sc_ref[rows, h * C:(h + 1) * C], vh) + lax.dot_general(
                qi_ref[rows, kc], st.astype(BF16), (((1,), (1,)), ((), ())), preferred_element_type=F32)
            st_ref[h] = st * dec_ref[c:c + 1, kc] + lax.dot_general(
                vh, ks_ref[rows, kc], (((0,), (0,)), ((), ())), preferred_element_type=F32)
            o = o * lax.rsqrt(jnp.mean(o * o, axis=-1, keepdims=True) + GLA_NORM_EPS) * ng_ref[...]
            og = og_ref[0, rows, vc].astype(F32)
            y_ref[0, rows, vc] = (o * (og * _sigmoid(og))).astype(BF16)


def _gla(q, k, v, og, la, norm_g, tc):
    B, S, qk = q.shape
    vw = v.shape[-1]
    tok = lambda width: pl.BlockSpec((1, tc, width), lambda b, i: (b, i, 0))
    return pl.pallas_call(
        _gla_kernel,
        grid=(B, S // tc),
        in_specs=[tok(qk), tok(qk), tok(vw), tok(vw), tok(qk),
                  pl.BlockSpec(norm_g.shape, lambda b, i: (0, 0))],
        out_specs=tok(vw),
        out_shape=jax.ShapeDtypeStruct((B, S, vw), BF16),
        scratch_shapes=[pltpu.VMEM((GLA_HEADS, GLA_DV, GLA_DK), F32)] + [pltpu.VMEM((tc, qk), BF16)] * 4
        + [pltpu.VMEM((8, qk), F32), pltpu.VMEM((tc, GLA_HEADS * GLA_CHUNK), BF16)],
        compiler_params=_cparams("parallel", "arbitrary"),
        name="gla_chunk",
    )(q, k, v, og, la, norm_g)


def _rwkv_proj_kernel(x_ref, g_ref, w_ref, mu_ref, w0_ref, w2_ref, a0_ref, a2_ref, g2_ref,
                      rv_ref, w_out_ref, ka_ref, gate_ref, carry_ref):
    W = rv_ref.shape[-1]
    tm = x_ref.shape[1]

    @pl.when(pl.program_id(1) == 0)
    def _():
        carry_ref[...] = jnp.zeros_like(carry_ref)

    pm = tm // PROJ_PARTS
    parts = [slice(i * pm, (i + 1) * pm) for i in range(PROJ_PARTS)]
    zs = [_dot(_rms(x_ref[0, rows, :], g_ref[...]).astype(BF16), w_ref[...]) for rows in parts]
    first = lax.broadcasted_iota(jnp.int32, zs[0].shape, 0) == 0
    last_row = carry_ref[0:1, :]
    for rows, z in zip(parts, zs):
        prev = jnp.where(first, last_row, pltpu.roll(z, 1, axis=0))
        last_row = z[pm - 1:pm, :]
        u = z + mu_ref[...] * (prev - z)

        r = u[:, :W]
        k = u[:, W:2 * W]
        v = u[:, 2 * W:3 * W]
        lora = u[:, 3 * W:3 * W + LANES]
        xg = u[:, 3 * W + LANES:]
        w_log = -_softplus(-(w0_ref[...] + _dot(jnp.tanh(lora).astype(BF16), w2_ref[...]))) - 0.5
        iclr = _sigmoid(a0_ref[...] + _dot(lora.astype(BF16), a2_ref[...]))
        rv_ref[0, rows, :] = _pack_pair(r, v)
        ka_ref[0, rows, :] = _pack_pair(k, iclr)
        w_out_ref[0, rows, :] = _pack_rows(-jnp.exp(w_log))
        gate_ref[0, rows, :] = _dot(_sigmoid(xg).astype(BF16), g2_ref[...]).astype(BF16)
    carry_ref[0:1, :] = last_row


def _rwkv_proj(x, gain, w, mu, w0, w2p, a0, a2p, g2, tm):
    B, S, D = x.shape
    W = w0.shape[-1]
    tok = lambda width: pl.BlockSpec((1, tm, width), lambda b, i: (b, i, 0))
    full = lambda a: pl.BlockSpec(a.shape, lambda b, i: (0,) * a.ndim)
    consts = (gain, w, mu, w0, w2p, a0, a2p, g2)
    return pl.pallas_call(
        _rwkv_proj_kernel,
        grid=(B, S // tm),
        in_specs=[tok(D)] + [full(c) for c in consts],
        out_specs=[tok(W), tok(W // 2), tok(W), tok(W)],
        out_shape=[jax.ShapeDtypeStruct((B, S, W), U32), jax.ShapeDtypeStruct((B, S, W // 2), U32),
                   jax.ShapeDtypeStruct((B, S, W), U32), jax.ShapeDtypeStruct((B, S, W), BF16)],
        scratch_shapes=[pltpu.VMEM((8, w.shape[1]), F32)],
        compiler_params=_cparams("parallel", "arbitrary"),
        name="rwkv_proj",
    )(x, *consts)


def _rwkv_scan_kernel(rv_ref, ld_ref, kia_ref, kian_ref, kk_ref, ka_ref, rk_ref, lnw_ref, lnb_ref,
                      y_ref, st_ref, sa_ref, av_ref, bv_ref, km_ref, r_ref, v_ref, yr_ref, w_ref):
    tt, G, N, _ = rv_ref.shape
    half = N // 2

    def step_vectors(packed):
        k_raw, iclr = _unpack_rows(packed)
        kk = k_raw * kk_ref[...]
        kk = kk / jnp.maximum(jnp.sqrt(jnp.sum(kk * kk, axis=0, keepdims=True)), 1e-12)
        return -kk, kk * iclr, k_raw * (1.0 + (iclr - 1.0) * ka_ref[...])

    @pl.when(pl.program_id(0) == 0)
    def _():
        st_ref[...] = jnp.zeros_like(st_ref)
        sa_ref[...] = jnp.zeros_like(sa_ref)
        for g in range(G):
            _, bv_ref[0, g], km_ref[0, g] = step_vectors(kia_ref[0, g])

    t0 = pl.program_id(0) * tt

    def group_step(tg, carry):
        t = tg // G
        g = tg % G
        cur = (t0 + t) % 2
        last = t + 1 >= tt
        t_nx = jnp.minimum(t + 1, tt - 1)
        av_ref[...], bv_ref[1 - cur, g], km_ref[1 - cur, g] = step_vectors(
            jnp.where(last, kian_ref[0, g], kia_ref[t_nx, g]))
        r_ref[...], v_ref[...] = _unpack_rows(rv_ref[t, g])
        w_ref[...] = jnp.exp(jnp.concatenate(_unpack_rows(ld_ref[t, g]), axis=0))

        def half_step(ih, c):
            rows = slice(ih * half, (ih + 1) * half)
            sa_h = sa_ref[g, rows, :]
            vt = v_ref[rows, :]
            y = jnp.zeros((half, LANES), F32)
            sa_nx = jnp.zeros((half, LANES), F32)
            for j in range(N):
                row = slice(j, j + 1)
                s_new = (st_ref[g, j, rows, :] * w_ref[row, :] + sa_h * bv_ref[cur, g, row, :]
                         + vt * km_ref[cur, g, row, :])
                st_ref[g, j, rows, :] = s_new
                y = y + s_new * r_ref[row, :]
                sa_nx = sa_nx + s_new * av_ref[row, :]
            yr_ref[rows, :] = y
            sa_ref[g, rows, :] = sa_nx
            return c

        for ih in range(2):
            half_step(ih, 0)

        y = yr_ref[...]
        mean = jnp.mean(y, axis=0, keepdims=True)
        d = y - mean
        var = jnp.mean(d * d, axis=0, keepdims=True)
        bonus = jnp.sum(r_ref[...] * km_ref[cur, g] * rk_ref[...], axis=0, keepdims=True)
        out = d * lax.rsqrt(var + RWKV_GN_EPS) * lnw_ref[...] + lnb_ref[...] + bonus * v_ref[...]
        y_ref[t, g] = _pack_pair(out[:half], out[half:])
        return carry

    lax.fori_loop(0, tt * G, group_step, 0)


def _rwkv_scan(rv, log_decay, kia, k_k, k_a, r_k, ln_w, ln_b, tt):
    S, G, N, _ = rv.shape
    blk = pl.BlockSpec((tt, G, N, LANES), lambda i: (i, 0, 0, 0))
    pair = pl.BlockSpec((tt, G, N // 2, LANES), lambda i: (i, 0, 0, 0))
    nxt = pl.BlockSpec((1, G, N, LANES), lambda i: (jnp.minimum((i + 1) * tt, S - 1), 0, 0, 0))
    par = pl.BlockSpec((N, LANES), lambda i: (0, 0))
    tile = pltpu.VMEM((N, LANES), F32)
    per_g = pltpu.VMEM((G, N, LANES), F32)
    two_g = pltpu.VMEM((2, G, N, LANES), F32)
    return pl.pallas_call(
        _rwkv_scan_kernel,
        grid=(S // tt,),
        in_specs=[blk, pair, blk, nxt] + [par] * 5,
        out_specs=pair,
        out_shape=jax.ShapeDtypeStruct((S, G, N // 2, LANES), U32),
        scratch_shapes=[pltpu.VMEM((G, N, N, LANES), F32), per_g, tile, two_g, two_g, tile, tile, tile, tile],
        compiler_params=_cparams("arbitrary"),
        name="rwkv_scan",
    )(rv, log_decay, kia, kia, k_k, k_a, r_k, ln_w, ln_b)


def _merge_kernel(x_ref, ya_ref, ys_ref, gate_ref, g_mix_ref, wg_ref, wa_ref, wb_ref, wo_ref, g_ffn_ref,
                  rwh_ref, rwl_ref, rb_ref,
                  x1_ref, h2_ref, e_ref, rank_ref, p_ref, cnt_ref, carry_ref, *, n_experts):
    tm, D = x_ref.shape

    @pl.when(pl.program_id(0) == 0)
    def _():
        carry_ref[...] = jnp.zeros_like(carry_ref)

    hm = tm // MERGE_PARTS
    lane = lax.broadcasted_iota(jnp.int32, (hm, LANES), 1)

    def mix(rows):
        x = x_ref[rows, :]
        h = _rms(x, g_mix_ref[...]).astype(BF16)
        gates = _sigmoid(_dot(h, wg_ref[...]))
        ys = jnp.concatenate(_unpack_rows(ys_ref[rows, :]), axis=1)
        yb = (ys * gate_ref[rows, :].astype(F32)).astype(BF16)
        merged = gates[:, :D] * _dot(ya_ref[rows, :], wa_ref[...]) + gates[:, D:] * _dot(yb, wb_ref[...])
        x1 = x + _dot(merged.astype(BF16), wo_ref[...])
        x1_ref[rows, :] = x1
        h2 = _rms(x1, g_ffn_ref[...])
        _store_packed(h2_ref.at[rows], h2)
        h_hi = h2.astype(BF16)
        h_lo = (h2 - h_hi.astype(F32)).astype(BF16)
        logits = _dot(h_hi, rwh_ref[...]) + _dot(h_lo, rwh_ref[...]) + _dot(h_hi, rwl_ref[...]) + rb_ref[...]
        return jnp.where(lane < n_experts, logits, -jnp.inf)

    def route(rows, logits, seen):
        vals, idxs, hots = [], [], []
        for _ in range(TOP_K):
            m = jnp.max(logits, axis=-1, keepdims=True)
            idx = jnp.min(jnp.where(logits == m, lane, LANES), axis=-1, keepdims=True)
            hot = lane == idx
            vals.append(m)
            idxs.append(idx)
            hots.append(hot)
            logits = jnp.where(hot, -jnp.inf, logits)
        exps = [jnp.exp(val - vals[0]) for val in vals]
        denom = functools.reduce(lambda s, e: s + e, exps)
        chosen = functools.reduce(lambda s, e: s + e, [hot.astype(F32) for hot in hots])
        r_i = lax.broadcasted_iota(jnp.int32, (hm, hm), 0)
        c_i = lax.broadcasted_iota(jnp.int32, (hm, hm), 1)
        before = _dot((r_i > c_i).astype(BF16), chosen.astype(BF16)) + seen
        e_out = jnp.zeros((hm, LANES), jnp.int32)
        rank_out = jnp.zeros((hm, LANES), jnp.int32)
        p_out = jnp.zeros((hm, LANES), F32)
        for s in range(TOP_K):
            rank = jnp.sum(jnp.where(hots[s], before, 0.0), axis=-1, keepdims=True)
            e_out = jnp.where(lane == s, idxs[s], e_out)
            rank_out = jnp.where(lane == s, rank.astype(jnp.int32), rank_out)
            p_out = jnp.where(lane == s, exps[s] / denom, p_out)
        e_ref[rows, :] = e_out
        rank_ref[rows, :] = rank_out
        p_ref[rows, :] = p_out
        return seen + jnp.sum(chosen, axis=0, keepdims=True)

    parts = [slice(i * hm, (i + 1) * hm) for i in range(MERGE_PARTS)]
    logits = [mix(rows) for rows in parts]
    seen = carry_ref[0:1, :]
    for rows, lg in zip(parts, logits):
        seen = route(rows, lg, seen)
    carry_ref[0:1, :] = seen
    cnt_ref[...] = jnp.broadcast_to(seen, cnt_ref.shape).astype(jnp.int32)


def _merge(x, ya, ys, gate, consts, n_experts, tm):
    T, D = x.shape
    W = gate.shape[-1]
    tok = lambda width: pl.BlockSpec((tm, width), lambda i: (i, 0))
    full = lambda a: pl.BlockSpec(a.shape, lambda i: (0,) * a.ndim)
    return pl.pallas_call(
        functools.partial(_merge_kernel, n_experts=n_experts),
        grid=(T // tm,),
        in_specs=[tok(D), tok(ya.shape[-1]), tok(ys.shape[-1]), tok(W)] + [full(c) for c in consts],
        out_specs=[tok(D), pl.BlockSpec((tm, D // 2 // LANES, LANES), lambda i: (i, 0, 0)),
                   tok(LANES), tok(LANES), tok(LANES), pl.BlockSpec((8, LANES), lambda i: (0, 0))],
        out_shape=[jax.ShapeDtypeStruct((T, D), F32), jax.ShapeDtypeStruct((T, D // 2 // LANES, LANES), U32),
                   jax.ShapeDtypeStruct((T, LANES), jnp.int32), jax.ShapeDtypeStruct((T, LANES), jnp.int32),
                   jax.ShapeDtypeStruct((T, LANES), F32), jax.ShapeDtypeStruct((8, LANES), jnp.int32)],
        scratch_shapes=[pltpu.VMEM((8, LANES), F32)],
        compiler_params=_cparams("arbitrary"),
        name="merge_router",
    )(x, ya, ys, gate, *consts)


def _idx_copy(dest_ref, idx_ref, sem, tile, n_idx):
    slot = tile % 2
    return pltpu.make_async_copy(dest_ref.at[pl.ds(pl.multiple_of(tile * n_idx, n_idx), n_idx)],
                                 idx_ref.at[pl.ds(pl.multiple_of(slot * n_idx, n_idx), n_idx)], sem.at[slot])


def _dispatch_kernel(dest_ref, rows_ref, init_ref, xs_ref, idx_ref, idx_sem, row_sem, *, tm):
    del init_ref
    i = pl.program_id(0)
    n = pl.num_programs(0)
    n_idx = tm * TOP_K
    idx_copy = functools.partial(_idx_copy, dest_ref, idx_ref, idx_sem, n_idx=n_idx)

    @pl.when(i == 0)
    def _():
        idx_copy(i).start()

    idx_copy(i).wait()

    @pl.when(i + 1 < n)
    def _():
        idx_copy(i + 1).start()

    base = (i % 2) * n_idx

    def issue(g, c):
        r0 = g * ISSUE_GROUP
        dst = [idx_ref[base + r0 * TOP_K + k] for k in range(ISSUE_GROUP * TOP_K)]
        for k, d in enumerate(dst):
            pltpu.make_async_copy(rows_ref.at[r0 + k // TOP_K], xs_ref.at[d], row_sem).start(priority=k % 2)
        return c

    lax.fori_loop(0, tm // ISSUE_GROUP, issue, 0)

    for _ in range(TOP_K):
        pltpu.make_async_copy(rows_ref, xs_ref.at[pl.ds(0, tm)], row_sem).wait()


def _dispatch(dest, rows, n_rows, tm):
    T = rows.shape[0]
    packed = (n_rows,) + rows.shape[1:]
    any_spec = pl.BlockSpec(memory_space=pl.ANY)
    return pl.pallas_call(
        functools.partial(_dispatch_kernel, tm=tm),
        grid=(T // tm,),
        in_specs=[any_spec, pl.BlockSpec((tm,) + rows.shape[1:], lambda i: (i, 0, 0)), any_spec],
        out_specs=any_spec,
        out_shape=jax.ShapeDtypeStruct(packed, U32),
        input_output_aliases={2: 0},
        scratch_shapes=[pltpu.SMEM((2 * tm * TOP_K,), jnp.int32), pltpu.SemaphoreType.DMA((2,)),
                        pltpu.SemaphoreType.DMA],
        compiler_params=_cparams("arbitrary"),
        name="moe_dispatch",
    )(dest, rows, jnp.zeros(packed, U32))


def _split_w1_kernel(w_ref, glu_ref, lin_ref):
    cols = w_ref.shape[-1]
    src = lax.broadcasted_iota(jnp.int32, (cols, cols // 2), 0)
    dst = lax.broadcasted_iota(jnp.int32, (cols, cols // 2), 1)
    w = w_ref[0].astype(BF16)
    glu_ref[0] = _dot(w, (src == 2 * dst).astype(BF16)).astype(BF16)
    lin_ref[0] = _dot(w, (src == 2 * dst + 1).astype(BF16)).astype(BF16)


def _split_w1(w1, cols=512):
    E, D, FF2 = w1.shape
    out = jax.ShapeDtypeStruct((E, D, FF2 // 2), BF16)
    return pl.pallas_call(
        _split_w1_kernel,
        grid=(E, FF2 // cols),
        in_specs=[pl.BlockSpec((1, D, cols), lambda e, c: (e, 0, c))],
        out_specs=[pl.BlockSpec((1, D, cols // 2), lambda e, c: (e, 0, c))] * 2,
        out_shape=[out, out],
        compiler_params=_cparams("parallel", "parallel"),
        name="split_w1",
    )(w1)


def _expert_kernel(be_ref, nu_ref, x_ref, w1g_ref, w1l_ref, b1g_ref, b1l_ref, w2_ref, b2_ref, o_ref):
    del be_ref
    used = pl.program_id(0) < nu_ref[0]

    @pl.when(used)
    def _():
        lo, hi = _load_packed(x_ref)
        x = jnp.concatenate([lo.astype(BF16), hi.astype(BF16)], axis=1)
        glu = jnp.minimum(_dot(x, w1g_ref[0]) + b1g_ref[0], SWIGLU_LIMIT)
        lin = jnp.clip(_dot(x, w1l_ref[0]) + b1l_ref[0], -SWIGLU_LIMIT, SWIGLU_LIMIT)
        act = glu * _sigmoid(SWIGLU_ALPHA * glu) * (lin + 1.0)
        _store_packed(o_ref, _dot(act.astype(BF16), w2_ref[0]) + b2_ref[0])

    @pl.when(jnp.logical_not(used))
    def _():
        o_ref[...] = jnp.zeros_like(o_ref)


def _experts(block_expert, n_used, xs, w1g, w1l, b1g, b1l, w2, b2):
    R = xs.shape[0]
    D, FF = w1g.shape[1:]
    nb = R // MOE_ROWS
    per_e = lambda shape: pl.BlockSpec((1,) + shape, lambda i, be, nu: (be[i], 0, 0))
    rows = pl.BlockSpec((MOE_ROWS,) + xs.shape[1:], lambda i, be, nu: (i, 0, 0))
    return pl.pallas_call(
        _expert_kernel,
        grid_spec=pltpu.PrefetchScalarGridSpec(
            num_scalar_prefetch=2,
            grid=(nb,),
            in_specs=[rows, per_e((D, FF)), per_e((D, FF)), per_e((1, FF)), per_e((1, FF)),
                      per_e((FF, D)), per_e((1, D))],
            out_specs=rows,
        ),
        out_shape=jax.ShapeDtypeStruct(xs.shape, U32),
        compiler_params=_cparams("arbitrary"),
        name="expert_ffn",
    )(block_expert, n_used, xs, w1g, w1l, b1g, b1l, w2, b2)


def _combine_kernel(dest_ref, rows_ref, x1_ref, p_ref, g_ref, o_ref, buf_ref, idx_ref, idx_sem, row_sem):
    tm, D = x1_ref.shape
    i = pl.program_id(0)
    n = pl.num_programs(0)
    n_idx = tm * TOP_K
    idx_copy = functools.partial(_idx_copy, dest_ref, idx_ref, idx_sem, n_idx=n_idx)

    def gather(tile):
        slot = tile % 2
        base = slot * n_idx

        def issue(g, c):
            r0 = g * ISSUE_GROUP
            src = [idx_ref[base + r0 * TOP_K + k] for k in range(ISSUE_GROUP * TOP_K)]
            for k, d in enumerate(src):
                pltpu.make_async_copy(rows_ref.at[d], buf_ref.at[slot, k % TOP_K, r0 + k // TOP_K],
                                      row_sem.at[slot]).start(priority=k % 2)
            return c

        lax.fori_loop(0, tm // ISSUE_GROUP, issue, 0)

    @pl.when(i == 0)
    def _():
        idx_copy(i).start()
        idx_copy(i).wait()
        gather(i)

        @pl.when(n > 1)
        def _():
            idx_copy(i + 1).start()

    @pl.when(i + 1 < n)
    def _():
        idx_copy(i + 1).wait()
        gather(i + 1)

    @pl.when(i + 2 < n)
    def _():
        idx_copy(i + 2).start()

    slot = i % 2
    for s in range(TOP_K):
        pltpu.make_async_copy(rows_ref.at[pl.ds(0, tm)], buf_ref.at[slot, s], row_sem.at[slot]).wait()

    half = D // 2
    x1 = x1_ref[...]
    acc_lo, acc_hi = x1[:, :half], x1[:, half:]
    p = p_ref[...]
    for s in range(TOP_K):
        lo, hi = _load_packed(buf_ref.at[slot, s])
        acc_lo = acc_lo + p[:, s:s + 1] * lo
        acc_hi = acc_hi + p[:, s:s + 1] * hi
    ms = (jnp.sum(acc_lo * acc_lo, axis=-1, keepdims=True) + jnp.sum(acc_hi * acc_hi, axis=-1, keepdims=True)) / D
    scale = lax.rsqrt(ms + RMS_EPS)
    o_ref[:, :half] = acc_lo * scale * g_ref[:, :half]
    o_ref[:, half:] = acc_hi * scale * g_ref[:, half:]


def _combine(dest, rows, x1, p, gain, tm):
    T, D = x1.shape
    any_spec = pl.BlockSpec(memory_space=pl.ANY)
    return pl.pallas_call(
        _combine_kernel,
        grid=(T // tm,),
        in_specs=[any_spec, any_spec, pl.BlockSpec((tm, D), lambda i: (i, 0)),
                  pl.BlockSpec((tm, LANES), lambda i: (i, 0)), pl.BlockSpec(gain.shape, lambda i: (0, 0))],
        out_specs=pl.BlockSpec((tm, D), lambda i: (i, 0)),
        out_shape=jax.ShapeDtypeStruct((T, D), F32),
        scratch_shapes=[pltpu.VMEM((2, TOP_K, tm) + rows.shape[1:], U32), pltpu.SMEM((2 * tm * TOP_K,), jnp.int32),
                        pltpu.SemaphoreType.DMA((2,)), pltpu.SemaphoreType.DMA((2,))],
        compiler_params=_cparams("arbitrary"),
        name="moe_combine_norm",
    )(dest, rows, x1, p, gain)


def _pick_tile(n, pref):
    t = min(n, pref)
    assert n % t == 0, (n, t)
    return t


def kernel(x, norm_mix_g, w_in, gla_gate_up, gla_gate_bias, gla_norm_g, rwkv_mu, rwkv_w0, rwkv_w2, rwkv_a0, rwkv_a2, rwkv_g2, rwkv_k_k, rwkv_k_a, rwkv_r_k, rwkv_ln_w, rwkv_ln_b, w_branch_a, w_branch_b, w_out, norm_ffn_g, router_w, router_b, expert_w1, expert_b1, expert_w2, expert_b2, norm_final_g):
    assert norm_mix_g.shape[0] == 1, "single-layer block"
    B, S, D = x.shape
    T = B * S
    qk = gla_gate_bias.shape[-1]
    vw = GLA_HEADS * GLA_DV
    W = rwkv_w0.shape[-1]
    H = W // RWKV_HEAD_DIM
    E = router_w.shape[-1]
    assert qk == GLA_HEADS * GLA_DK and B * H % LANES == 0 and E <= LANES
    row = lambda a: a.reshape(1, -1)

    wi = w_in[0]
    gla_cols = 2 * qk + vw + GLA_GATE_RANK + vw
    rwkv_cols = rwkv_mu.shape[-1]
    o_q, o_k, o_v, o_lr, o_og = 0, qk, 2 * qk, 2 * qk + vw, 2 * qk + vw + GLA_GATE_RANK
    w_gla = jnp.concatenate(
        [wi[:, o_q:o_lr], wi[:, o_og:gla_cols], wi[:, o_lr:o_og],
         jnp.zeros((D, LANES - GLA_GATE_RANK), F32)], axis=1).astype(BF16)
    up = jnp.concatenate([gla_gate_up[0], jnp.zeros((LANES - GLA_GATE_RANK, qk), F32)], axis=0).astype(BF16)
    w_rwkv = wi[:, gla_cols:gla_cols + rwkv_cols].astype(BF16)
    w_gate = wi[:, gla_cols + rwkv_cols:].astype(BF16)
    n_w, n_a = rwkv_w2.shape[1], rwkv_a2.shape[1]
    assert n_w + n_a == LANES
    w2p = jnp.concatenate([rwkv_w2[0], jnp.zeros((n_a, W), F32)], axis=0).astype(BF16)
    a2p = jnp.concatenate([jnp.zeros((n_w, W), F32), rwkv_a2[0]], axis=0).astype(BF16)

    tm_gla = _pick_tile(S, 512)
    q, k, v, og, la = _gla_proj(x, row(norm_mix_g), w_gla, up, row(gla_gate_bias), tm_gla)
    ya = _gla(q, k, v, og, la, row(gla_norm_g), _pick_tile(S, 256))

    tm_r = _pick_tile(S, 512)
    hn = RWKV_HEAD_DIM // 2
    col = jnp.arange(W)
    scan_order = (col % (W // 2)) // hn * RWKV_HEAD_DIM + col // (W // 2) * hn + col % hn
    rv_, ld_, kia_, gate_ = _rwkv_proj(x, row(norm_mix_g), w_rwkv, row(rwkv_mu), row(rwkv_w0)[:, scan_order],
                                       w2p[:, scan_order], row(rwkv_a0), a2p,
                                       rwkv_g2[0][:, scan_order].astype(BF16), tm_r)
    G = B * H // LANES
    bg = B // G
    to_scan = lambda t, n: t.reshape(G, bg, S, H, n).transpose(2, 0, 4, 1, 3).reshape(S, G, n, LANES)
    par = lambda p: jnp.tile(p.reshape(H, RWKV_HEAD_DIM).T, (1, bg))
    ys = _rwkv_scan(to_scan(rv_, RWKV_HEAD_DIM), to_scan(ld_, hn), to_scan(kia_, RWKV_HEAD_DIM),
                    par(rwkv_k_k), par(rwkv_k_a), par(rwkv_r_k), par(rwkv_ln_w), par(rwkv_ln_b), _pick_tile(S, 32))
    ys = ys.reshape(S, G, hn, bg, H).transpose(1, 3, 0, 4, 2).reshape(T, W // 2)

    rw = jnp.concatenate([router_w[0], jnp.zeros((D, LANES - E), F32)], axis=1)
    rb = jnp.concatenate([router_b[0], jnp.zeros((LANES - E,), F32)]).reshape(1, LANES)
    rw_hi = rw.astype(BF16)
    rw_lo = (rw - rw_hi.astype(F32)).astype(BF16)
    consts = (row(norm_mix_g), w_gate, w_branch_a[0].astype(BF16), w_branch_b[0][scan_order, :].astype(BF16),
              w_out[0].astype(BF16), row(norm_ffn_g), rw_hi, rw_lo, rb)
    tm_m = _pick_tile(T, 512)
    x1, h2, e_sel, rank, p_sel, counts = _merge(
        x.reshape(T, D), ya.reshape(T, vw), ys, gate_.reshape(T, W), consts, E, tm_m)

    counts = counts[0, :E]
    padded = (counts + MOE_ROWS - 1) // MOE_ROWS * MOE_ROWS
    pad_ends = jnp.cumsum(padded)
    pad_starts = pad_ends - padded
    nb = -(-T * TOP_K // MOE_ROWS) + E
    dest = pad_starts[e_sel[:, :TOP_K]] + rank[:, :TOP_K]
    block_start = jnp.arange(nb, dtype=jnp.int32) * MOE_ROWS
    block_expert = jnp.minimum(jnp.sum(pad_ends[None, :] <= block_start[:, None], axis=1), E - 1).astype(jnp.int32)
    n_used = (pad_ends[-1:] // MOE_ROWS).astype(jnp.int32)
    dest = dest.reshape(-1).astype(jnp.int32)
    tm_d = _pick_tile(T, 256)
    xs = _dispatch(dest, h2, nb * MOE_ROWS, _pick_tile(T, 1024))

    FF = expert_w2.shape[2]
    w1g, w1l = _split_w1(expert_w1[0])
    b1 = expert_b1[0].reshape(E, 1, FF, 2)
    outs_e = _experts(block_expert, n_used, xs, w1g, w1l, b1[..., 0], b1[..., 1],
                      expert_w2[0].astype(BF16), expert_b2[0].reshape(E, 1, D))

    out = _combine(dest, outs_e, x1, p_sel, row(norm_final_g), tm_d)
    return out.reshape(B, S, D)
```

```python
import functools

import jax
import jax.numpy as jnp
from jax import lax
from jax.experimental import pallas as pl
from jax.experimental.pallas import tpu as pltpu

F32 = jnp.float32
BF16 = jnp.bfloat16
U32 = jnp.uint32

LANES = 128
VMEM_LIMIT = 56 * 1024 * 1024

RMS_EPS = 1e-6
GLA_HEADS = 4
GLA_DK = 128
GLA_DV = 256
GLA_GATE_RANK = 16
GLA_TAU = 16.0
GLA_CHUNK = 64
GLA_NORM_EPS = 1e-5
RWKV_HEAD_DIM = 64
RWKV_GN_EPS = 64e-5
TOP_K = 4
SWIGLU_ALPHA = 1.702
SWIGLU_LIMIT = 7.0
MOE_ROWS = 512
ISSUE_GROUP = 8
MERGE_PARTS = 2
PROJ_PARTS = 2

def _cparams(*sem):
    return pltpu.CompilerParams(dimension_semantics=sem, vmem_limit_bytes=VMEM_LIMIT)


def _rms(xf, gain):
    return xf * lax.rsqrt(jnp.mean(xf * xf, axis=-1, keepdims=True) + RMS_EPS) * gain


def _softplus(y):
    return jnp.maximum(y, 0.0) + jnp.log1p(jnp.exp(-jnp.abs(y)))


def _sigmoid(y):
    return 1.0 / (1.0 + jnp.exp(-y))


def _dot(a, b):
    return jnp.dot(a, b, preferred_element_type=F32)


def _pack_pair(lo, hi):
    bits = lambda t: lax.bitcast_convert_type(t.astype(BF16).astype(F32), U32)
    return (bits(lo) >> 16) | bits(hi)


def _pack_rows(x):
    n = x.shape[-1] // 2
    return _pack_pair(x[:, :n], x[:, n:])


def _unpack_rows(w):
    lo = lax.bitcast_convert_type(w << 16, F32)
    hi = lax.bitcast_convert_type(w & jnp.uint32(0xFFFF0000), F32)
    return lo, hi


def _store_packed(ref, x):
    packed = _pack_rows(x)
    for c in range(ref.shape[1]):
        ref[:, c, :] = packed[:, c * LANES:(c + 1) * LANES]


def _load_packed(ref):
    return _unpack_rows(jnp.concatenate([ref[:, c, :] for c in range(ref.shape[1])], axis=1))


def _gla_proj_kernel(x_ref, g_ref, w_ref, up_ref, bias_ref, q_ref, k_ref, v_ref, og_ref, la_ref):
    qk = q_ref.shape[-1]
    vw = v_ref.shape[-1]
    pm = x_ref.shape[1] // PROJ_PARTS
    parts = [slice(i * pm, (i + 1) * pm) for i in range(PROJ_PARTS)]
    zs = [_dot(_rms(x_ref[0, rows, :], g_ref[...]).astype(BF16), w_ref[...]) for rows in parts]
    for rows, z in zip(parts, zs):
        q_ref[0, rows, :] = z[:, :qk].astype(BF16)
        k_ref[0, rows, :] = z[:, qk:2 * qk].astype(BF16)
        v_ref[0, rows, :] = z[:, 2 * qk:2 * qk + vw].astype(BF16)
        og_ref[0, rows, :] = z[:, 2 * qk + vw:2 * qk + 2 * vw].astype(BF16)
        lr = z[:, 2 * qk + 2 * vw:].astype(BF16)
        pre = _dot(lr, up_ref[...]) + bias_ref[...]
        la_ref[0, rows, :] = -_softplus(-pre) * (1.0 / GLA_TAU)


def _gla_proj(x, gain, w, up, bias, tm):
    B, S, D = x.shape
    qk = bias.shape[-1]
    vw = (w.shape[1] - 2 * qk - LANES) // 2
    tok = lambda width: pl.BlockSpec((1, tm, width), lambda b, i: (b, i, 0))
    full = lambda a: pl.BlockSpec(a.shape, lambda b, i: (0,) * a.ndim)
    return pl.pallas_call(
        _gla_proj_kernel,
        grid=(B, S // tm),
        in_specs=[tok(D), full(gain), full(w), full(up), full(bias)],
        out_specs=[tok(qk), tok(qk), tok(vw), tok(vw), tok(qk)],
        out_shape=[jax.ShapeDtypeStruct((B, S, qk), BF16), jax.ShapeDtypeStruct((B, S, qk), BF16),
                   jax.ShapeDtypeStruct((B, S, vw), BF16), jax.ShapeDtypeStruct((B, S, vw), BF16),
                   jax.ShapeDtypeStruct((B, S, qk), F32)],
        compiler_params=_cparams("parallel", "parallel"),
        name="gla_proj",
    )(x, gain, w, up, bias)


def _gla_kernel(q_ref, k_ref, v_ref, og_ref, la_ref, ng_ref, y_ref, st_ref, qa_ref, ka_ref, qi_ref, ks_ref, dec_ref,
                sc_ref):
    C = GLA_CHUNK
    tc = q_ref.shape[1]

    @pl.when(pl.program_id(1) == 0)
    def _():
        st_ref[...] = jnp.zeros_like(st_ref)

    row = lax.broadcasted_iota(jnp.int32, (tc, tc), 0)
    col = lax.broadcasted_iota(jnp.int32, (tc, tc), 1)
    same_chunk_before = jnp.logical_and(row >= col, row // C == col // C)
    tri = same_chunk_before.astype(BF16)
    la = la_ref[0]
    b_all = jnp.zeros_like(la)
    for _ in range(3):
        term = la.astype(BF16)
        b_all = b_all + _dot(tri, term)
        la = la - term.astype(F32)
    scale = GLA_DK ** -0.5
    for c in range(tc // C):
        rows = slice(c * C, (c + 1) * C)
        b = b_all[rows]
        b_mid = b[C // 2:C // 2 + 1]
        b_end = b[C - 1:C]
        q = q_ref[0, rows, :].astype(F32) * scale
        k = k_ref[0, rows, :].astype(F32)
        qa_ref[rows, :] = (q * jnp.exp(b - b_mid)).astype(BF16)
        ka_ref[rows, :] = (k * jnp.exp(b_mid - b)).astype(BF16)
        qi_ref[rows, :] = (q * jnp.exp(b)).astype(BF16)
        ks_ref[rows, :] = (k * jnp.exp(b_end - b)).astype(BF16)
        dec_ref[c:c + 1, :] = jnp.exp(b_end)

    r_c = lax.broadcasted_iota(jnp.int32, (C, C), 0)
    c_c = lax.broadcasted_iota(jnp.int32, (C, C), 1)
    causal = r_c >= c_c
    for c in range(tc // C):
        rows = slice(c * C, (c + 1) * C)
        for h in range(GLA_HEADS):
            kc = slice(h * GLA_DK, (h + 1) * GLA_DK)
            scores = lax.dot_general(qa_ref[rows, kc], ka_ref[rows, kc], (((1,), (1,)), ((), ())),
                                     preferred_element_type=F32)
            sc_ref[rows, h * C:(h + 1) * C] = jnp.where(causal, scores, 0.0).astype(BF16)

    for c in range(tc // C):
        rows = slice(c * C, (c + 1) * C)
        for h in range(GLA_HEADS):
            kc = slice(h * GLA_DK, (h + 1) * GLA_DK)
            vc = slice(h * GLA_DV, (h + 1) * GLA_DV)
            vh = v_ref[0, rows, vc]
            st = st_ref[h]
            o = _dot(sc_ref[rows, h * C:(h + 1) * C], vh) + lax.dot_general(
                qi_ref[rows, kc], st.astype(BF16), (((1,), (1,)), ((), ())), preferred_element_type=F32)
            st_ref[h] = st * dec_ref[c:c + 1, kc] + lax.dot_general(
                vh, ks_ref[rows, kc], (((0,), (0,)), ((), ())), preferred_element_type=F32)
            o = o * lax.rsqrt(jnp.mean(o * o, axis=-1, keepdims=True) + GLA_NORM_EPS) * ng_ref[...]
            og = og_ref[0, rows, vc].astype(F32)
            y_ref[0, rows, vc] = (o * (og * _sigmoid(og))).astype(BF16)


def _gla(q, k, v, og, la, norm_g, tc):
    B, S, qk = q.shape
    vw = v.shape[-1]
    tok = lambda width: pl.BlockSpec((1, tc, width), lambda b, i: (b, i, 0))
    return pl.pallas_call(
        _gla_kernel,
        grid=(B, S // tc),
        in_specs=[tok(qk), tok(qk), tok(vw), tok(vw), tok(qk),
                  pl.BlockSpec(norm_g.shape, lambda b, i: (0, 0))],
        out_specs=tok(vw),
        out_shape=jax.ShapeDtypeStruct((B, S, vw), BF16),
        scratch_shapes=[pltpu.VMEM((GLA_HEADS, GLA_DV, GLA_DK), F32)] + [pltpu.VMEM((tc, qk), BF16)] * 4
        + [pltpu.VMEM((8, qk), F32), pltpu.VMEM((tc, GLA_HEADS * GLA_CHUNK), BF16)],
        compiler_params=_cparams("parallel", "arbitrary"),
        name="gla_chunk",
    )(q, k, v, og, la, norm_g)


def _rwkv_proj_kernel(x_ref, g_ref, w_ref, mu_ref, w0_ref, w2_ref, a0_ref, a2_ref, g2_ref,
                      rv_ref, w_out_ref, ka_ref, gate_ref, carry_ref):
    W = rv_ref.shape[-1]
    tm = x_ref.shape[1]

    @pl.when(pl.program_id(1) == 0)
    def _():
        carry_ref[...] = jnp.zeros_like(carry_ref)

    pm = tm // PROJ_PARTS
    parts = [slice(i * pm, (i + 1) * pm) for i in range(PROJ_PARTS)]
    zs = [_dot(_rms(x_ref[0, rows, :], g_ref[...]).astype(BF16), w_ref[...]) for rows in parts]
    first = lax.broadcasted_iota(jnp.int32, zs[0].shape, 0) == 0
    last_row = carry_ref[0:1, :]
    for rows, z in zip(parts, zs):
        prev = jnp.where(first, last_row, pltpu.roll(z, 1, axis=0))
        last_row = z[pm - 1:pm, :]
        u = z + mu_ref[...] * (prev - z)

        r = u[:, :W]
        k = u[:, W:2 * W]
        v = u[:, 2 * W:3 * W]
        lora = u[:, 3 * W:3 * W + LANES]
        xg = u[:, 3 * W + LANES:]
        w_log = -_softplus(-(w0_ref[...] + _dot(jnp.tanh(lora).astype(BF16), w2_ref[...]))) - 0.5
        iclr = _sigmoid(a0_ref[...] + _dot(lora.astype(BF16), a2_ref[...]))
        rv_ref[0, rows, :] = _pack_pair(r, v)
        ka_ref[0, rows, :] = _pack_pair(k, iclr)
        w_out_ref[0, rows, :] = _pack_rows(-jnp.exp(w_log))
        gate_ref[0, rows, :] = _dot(_sigmoid(xg).astype(BF16), g2_ref[...]).astype(BF16)
    carry_ref[0:1, :] = last_row


def _rwkv_proj(x, gain, w, mu, w0, w2p, a0, a2p, g2, tm):
    B, S, D = x.shape
    W = w0.shape[-1]
    tok = lambda width: pl.BlockSpec((1, tm, width), lambda b, i: (b, i, 0))
    full = lambda a: pl.BlockSpec(a.shape, lambda b, i: (0,) * a.ndim)
    consts = (gain, w, mu, w0, w2p, a0, a2p, g2)
    return pl.pallas_call(
        _rwkv_proj_kernel,
        grid=(B, S // tm),
        in_specs=[tok(D)] + [full(c) for c in consts],
        out_specs=[tok(W), tok(W // 2), tok(W), tok(W)],
        out_shape=[jax.ShapeDtypeStruct((B, S, W), U32), jax.ShapeDtypeStruct((B, S, W // 2), U32),
                   jax.ShapeDtypeStruct((B, S, W), U32), jax.ShapeDtypeStruct((B, S, W), BF16)],
        scratch_shapes=[pltpu.VMEM((8, w.shape[1]), F32)],
        compiler_params=_cparams("parallel", "arbitrary"),
        name="rwkv_proj",
    )(x, *consts)


def _rwkv_scan_kernel(rv_ref, ld_ref, kia_ref, kian_ref, kk_ref, ka_ref, rk_ref, lnw_ref, lnb_ref,
                      y_ref, st_ref, sa_ref, av_ref, bv_ref, km_ref, r_ref, v_ref, yr_ref, w_ref):
    tt, G, N, _ = rv_ref.shape
    half = N // 2

    def step_vectors(packed):
        k_raw, iclr = _unpack_rows(packed)
        kk = k_raw * kk_ref[...]
        kk = kk / jnp.maximum(jnp.sqrt(jnp.sum(kk * kk, axis=0, keepdims=True)), 1e-12)
        return -kk, kk * iclr, k_raw * (1.0 + (iclr - 1.0) * ka_ref[...])

    @pl.when(pl.program_id(0) == 0)
    def _():
        st_ref[...] = jnp.zeros_like(st_ref)
        sa_ref[...] = jnp.zeros_like(sa_ref)
        for g in range(G):
            _, bv_ref[0, g], km_ref[0, g] = step_vectors(kia_ref[0, g])

    t0 = pl.program_id(0) * tt
    n_steps = tt * G
    assert n_steps % 2 == 0

    def prepare(tg):
        slot = tg % 2
        tg = jnp.minimum(tg, n_steps - 1)
        t = tg // G
        g = tg % G
        cur = (t0 + t) % 2
        last = t + 1 >= tt
        t_nx = jnp.minimum(t + 1, tt - 1)
        av_ref[slot], bv_ref[1 - cur, g], km_ref[1 - cur, g] = step_vectors(
            jnp.where(last, kian_ref[0, g], kia_ref[t_nx, g]))
        r_ref[slot], v_ref[slot] = _unpack_rows(rv_ref[t, g])
        w_ref[slot] = jnp.exp(jnp.concatenate(_unpack_rows(ld_ref[t, g]), axis=0))

    prepare(0)

    def group_step(tg, carry):
        t = tg // G
        g = tg % G
        cur = (t0 + t) % 2
        slot = tg % 2

        def half_step(ih):
            rows = slice(ih * half, (ih + 1) * half)
            sa_h = sa_ref[g, rows, :]
            vt = v_ref[slot, rows, :]
            y = jnp.zeros((half, LANES), F32)
            sa_nx = jnp.zeros((half, LANES), F32)
            for j in range(N):
                row = slice(j, j + 1)
                s_new = (st_ref[g, j, rows, :] * w_ref[slot, row, :] + sa_h * bv_ref[cur, g, row, :]
                         + vt * km_ref[cur, g, row, :])
                st_ref[g, j, rows, :] = s_new
                y = y + s_new * r_ref[slot, row, :]
                sa_nx = sa_nx + s_new * av_ref[slot, row, :]
            yr_ref[rows, :] = y
            sa_ref[g, rows, :] = sa_nx

        for ih in range(2):
            half_step(ih)

        y = yr_ref[...]
        mean = jnp.mean(y, axis=0, keepdims=True)
        d = y - mean
        var = jnp.mean(d * d, axis=0, keepdims=True)
        bonus = jnp.sum(r_ref[slot] * km_ref[cur, g] * rk_ref[...], axis=0, keepdims=True)
        out = d * lax.rsqrt(var + RWKV_GN_EPS) * lnw_ref[...] + lnb_ref[...] + bonus * v_ref[slot]
        y_ref[t, g] = _pack_pair(out[:half], out[half:])
        prepare(tg + 1)
        return carry

    lax.fori_loop(0, n_steps, group_step, 0)


def _rwkv_scan(rv, log_decay, kia, k_k, k_a, r_k, ln_w, ln_b, tt):
    S, G, N, _ = rv.shape
    blk = pl.BlockSpec((tt, G, N, LANES), lambda i: (i, 0, 0, 0))
    pair = pl.BlockSpec((tt, G, N // 2, LANES), lambda i: (i, 0, 0, 0))
    nxt = pl.BlockSpec((1, G, N, LANES), lambda i: (jnp.minimum((i + 1) * tt, S - 1), 0, 0, 0))
    par = pl.BlockSpec((N, LANES), lambda i: (0, 0))
    tile = pltpu.VMEM((N, LANES), F32)
    two = pltpu.VMEM((2, N, LANES), F32)
    per_g = pltpu.VMEM((G, N, LANES), F32)
    two_g = pltpu.VMEM((2, G, N, LANES), F32)
    return pl.pallas_call(
        _rwkv_scan_kernel,
        grid=(S // tt,),
        in_specs=[blk, pair, blk, nxt] + [par] * 5,
        out_specs=pair,
        out_shape=jax.ShapeDtypeStruct((S, G, N // 2, LANES), U32),
        scratch_shapes=[pltpu.VMEM((G, N, N, LANES), F32), per_g, two, two_g, two_g, two, two, tile, two],
        compiler_params=_cparams("arbitrary"),
        name="rwkv_scan",
    )(rv, log_decay, kia, kia, k_k, k_a, r_k, ln_w, ln_b)


def _merge_kernel(x_ref, ya_ref, ys_ref, gate_ref, g_mix_ref, wg_ref, wa_ref, wb_ref, wo_ref, g_ffn_ref,
                  rwh_ref, rwl_ref, rb_ref,
                  x1_ref, h2_ref, e_ref, rank_ref, p_ref, cnt_ref, carry_ref, *, n_experts):
    tm, D = x_ref.shape

    @pl.when(pl.program_id(0) == 0)
    def _():
        carry_ref[...] = jnp.zeros_like(carry_ref)

    hm = tm // MERGE_PARTS
    lane = lax.broadcasted_iota(jnp.int32, (hm, LANES), 1)

    def mix(rows):
        x = x_ref[rows, :]
        h = _rms(x, g_mix_ref[...]).astype(BF16)
        gates = _sigmoid(_dot(h, wg_ref[...]))
        ys = jnp.concatenate(_unpack_rows(ys_ref[rows, :]), axis=1)
        yb = (ys * gate_ref[rows, :].astype(F32)).astype(BF16)
        merged = gates[:, :D] * _dot(ya_ref[rows, :], wa_ref[...]) + gates[:, D:] * _dot(yb, wb_ref[...])
        x1 = x + _dot(merged.astype(BF16), wo_ref[...])
        x1_ref[rows, :] = x1
        h2 = _rms(x1, g_ffn_ref[...])
        _store_packed(h2_ref.at[rows], h2)
        h_hi = h2.astype(BF16)
        h_lo = (h2 - h_hi.astype(F32)).astype(BF16)
        logits = _dot(h_hi, rwh_ref[...]) + _dot(h_lo, rwh_ref[...]) + _dot(h_hi, rwl_ref[...]) + rb_ref[...]
        return jnp.where(lane < n_experts, logits, -jnp.inf)

    def route(rows, logits, seen):
        vals, idxs, hots = [], [], []
        for _ in range(TOP_K):
            m = jnp.max(logits, axis=-1, keepdims=True)
            idx = jnp.min(jnp.where(logits == m, lane, LANES), axis=-1, keepdims=True)
            hot = lane == idx
            vals.append(m)
            idxs.append(idx)
            hots.append(hot)
            logits = jnp.where(hot, -jnp.inf, logits)
        exps = [jnp.exp(val - vals[0]) for val in vals]
        denom = functools.reduce(lambda s, e: s + e, exps)
        chosen = functools.reduce(lambda s, e: s + e, [hot.astype(F32) for hot in hots])
        r_i = lax.broadcasted_iota(jnp.int32, (hm, hm), 0)
        c_i = lax.broadcasted_iota(jnp.int32, (hm, hm), 1)
        before = _dot((r_i > c_i).astype(BF16), chosen.astype(BF16)) + seen
        e_out = jnp.zeros((hm, LANES), jnp.int32)
        rank_out = jnp.zeros((hm, LANES), jnp.int32)
        p_out = jnp.zeros((hm, LANES), F32)
        for s in range(TOP_K):
            rank = jnp.sum(jnp.where(hots[s], before, 0.0), axis=-1, keepdims=True)
            e_out = jnp.where(lane == s, idxs[s], e_out)
            rank_out = jnp.where(lane == s, rank.astype(jnp.int32), rank_out)
            p_out = jnp.where(lane == s, exps[s] / denom, p_out)
        e_ref[rows, :] = e_out
        rank_ref[rows, :] = rank_out
        p_ref[rows, :] = p_out
        return seen + jnp.sum(chosen, axis=0, keepdims=True)

    parts = [slice(i * hm, (i + 1) * hm) for i in range(MERGE_PARTS)]
    logits = [mix(rows) for rows in parts]
    seen = carry_ref[0:1, :]
    for rows, lg in zip(parts, logits):
        seen = route(rows, lg, seen)
    carry_ref[0:1, :] = seen
    cnt_ref[...] = jnp.broadcast_to(seen, cnt_ref.shape).astype(jnp.int32)


def _merge(x, ya, ys, gate, consts, n_experts, tm):
    T, D = x.shape
    W = gate.shape[-1]
    tok = lambda width: pl.BlockSpec((tm, width), lambda i: (i, 0))
    full = lambda a: pl.BlockSpec(a.shape, lambda i: (0,) * a.ndim)
    return pl.pallas_call(
        functools.partial(_merge_kernel, n_experts=n_experts),
        grid=(T // tm,),
        in_specs=[tok(D), tok(ya.shape[-1]), tok(ys.shape[-1]), tok(W)] + [full(c) for c in consts],
        out_specs=[tok(D), pl.BlockSpec((tm, D // 2 // LANES, LANES), lambda i: (i, 0, 0)),
                   tok(LANES), tok(LANES), tok(LANES), pl.BlockSpec((8, LANES), lambda i: (0, 0))],
        out_shape=[jax.ShapeDtypeStruct((T, D), F32), jax.ShapeDtypeStruct((T, D // 2 // LANES, LANES), U32),
                   jax.ShapeDtypeStruct((T, LANES), jnp.int32), jax.ShapeDtypeStruct((T, LANES), jnp.int32),
                   jax.ShapeDtypeStruct((T, LANES), F32), jax.ShapeDtypeStruct((8, LANES), jnp.int32)],
        scratch_shapes=[pltpu.VMEM((8, LANES), F32)],
        compiler_params=_cparams("arbitrary"),
        name="merge_router",
    )(x, ya, ys, gate, *consts)


def _idx_copy(dest_ref, idx_ref, sem, tile, n_idx):
    slot = tile % 2
    return pltpu.make_async_copy(dest_ref.at[pl.ds(pl.multiple_of(tile * n_idx, n_idx), n_idx)],
                                 idx_ref.at[pl.ds(pl.multiple_of(slot * n_idx, n_idx), n_idx)], sem.at[slot])


def _dispatch_kernel(dest_ref, rows_ref, init_ref, xs_ref, idx_ref, idx_sem, row_sem, *, tm):
    del init_ref
    i = pl.program_id(0)
    n = pl.num_programs(0)
    n_idx = tm * TOP_K
    idx_copy = functools.partial(_idx_copy, dest_ref, idx_ref, idx_sem, n_idx=n_idx)

    @pl.when(i == 0)
    def _():
        idx_copy(i).start()

    idx_copy(i).wait()

    @pl.when(i + 1 < n)
    def _():
        idx_copy(i + 1).start()

    base = (i % 2) * n_idx

    def issue(g, c):
        r0 = g * ISSUE_GROUP
        dst = [idx_ref[base + r0 * TOP_K + k] for k in range(ISSUE_GROUP * TOP_K)]
        for k, d in enumerate(dst):
            pltpu.make_async_copy(rows_ref.at[r0 + k // TOP_K], xs_ref.at[d], row_sem).start(priority=k % 2)
        return c

    lax.fori_loop(0, tm // ISSUE_GROUP, issue, 0)

    for _ in range(TOP_K):
        pltpu.make_async_copy(rows_ref, xs_ref.at[pl.ds(0, tm)], row_sem).wait()


def _dispatch(dest, rows, n_rows, tm):
    T = rows.shape[0]
    packed = (n_rows,) + rows.shape[1:]
    any_spec = pl.BlockSpec(memory_space=pl.ANY)
    return pl.pallas_call(
        functools.partial(_dispatch_kernel, tm=tm),
        grid=(T // tm,),
        in_specs=[any_spec, pl.BlockSpec((tm,) + rows.shape[1:], lambda i: (i, 0, 0)), any_spec],
        out_specs=any_spec,
        out_shape=jax.ShapeDtypeStruct(packed, U32),
        input_output_aliases={2: 0},
        scratch_shapes=[pltpu.SMEM((2 * tm * TOP_K,), jnp.int32), pltpu.SemaphoreType.DMA((2,)),
                        pltpu.SemaphoreType.DMA],
        compiler_params=_cparams("arbitrary"),
        name="moe_dispatch",
    )(dest, rows, jnp.zeros(packed, U32))


def _split_w1_kernel(w_ref, glu_ref, lin_ref):
    cols = w_ref.shape[-1]
    src = lax.broadcasted_iota(jnp.int32, (cols, cols // 2), 0)
    dst = lax.broadcasted_iota(jnp.int32, (cols, cols // 2), 1)
    w = w_ref[0].astype(BF16)
    glu_ref[0] = _dot(w, (src == 2 * dst).astype(BF16)).astype(BF16)
    lin_ref[0] = _dot(w, (src == 2 * dst + 1).astype(BF16)).astype(BF16)


def _split_w1(w1, cols=512):
    E, D, FF2 = w1.shape
    out = jax.ShapeDtypeStruct((E, D, FF2 // 2), BF16)
    return pl.pallas_call(
        _split_w1_kernel,
        grid=(E, FF2 // cols),
        in_specs=[pl.BlockSpec((1, D, cols), lambda e, c: (e, 0, c))],
        out_specs=[pl.BlockSpec((1, D, cols // 2), lambda e, c: (e, 0, c))] * 2,
        out_shape=[out, out],
        compiler_params=_cparams("parallel", "parallel"),
        name="split_w1",
    )(w1)


def _expert_kernel(be_ref, nu_ref, x_ref, w1g_ref, w1l_ref, b1g_ref, b1l_ref, w2_ref, b2_ref, o_ref):
    del be_ref
    used = pl.program_id(0) < nu_ref[0]

    @pl.when(used)
    def _():
        lo, hi = _load_packed(x_ref)
        x = jnp.concatenate([lo.astype(BF16), hi.astype(BF16)], axis=1)
        glu = jnp.minimum(_dot(x, w1g_ref[0]) + b1g_ref[0], SWIGLU_LIMIT)
        lin = jnp.clip(_dot(x, w1l_ref[0]) + b1l_ref[0], -SWIGLU_LIMIT, SWIGLU_LIMIT)
        act = glu * _sigmoid(SWIGLU_ALPHA * glu) * (lin + 1.0)
        _store_packed(o_ref, _dot(act.astype(BF16), w2_ref[0]) + b2_ref[0])

    @pl.when(jnp.logical_not(used))
    def _():
        o_ref[...] = jnp.zeros_like(o_ref)


def _experts(block_expert, n_used, xs, w1g, w1l, b1g, b1l, w2, b2):
    R = xs.shape[0]
    D, FF = w1g.shape[1:]
    nb = R // MOE_ROWS
    per_e = lambda shape: pl.BlockSpec((1,) + shape, lambda i, be, nu: (be[i], 0, 0))
    rows = pl.BlockSpec((MOE_ROWS,) + xs.shape[1:], lambda i, be, nu: (i, 0, 0))
    return pl.pallas_call(
        _expert_kernel,
        grid_spec=pltpu.PrefetchScalarGridSpec(
            num_scalar_prefetch=2,
            grid=(nb,),
            in_specs=[rows, per_e((D, FF)), per_e((D, FF)), per_e((1, FF)), per_e((1, FF)),
                      per_e((FF, D)), per_e((1, D))],
            out_specs=rows,
        ),
        out_shape=jax.ShapeDtypeStruct(xs.shape, U32),
        compiler_params=_cparams("arbitrary"),
        name="expert_ffn",
    )(block_expert, n_used, xs, w1g, w1l, b1g, b1l, w2, b2)


def _combine_kernel(dest_ref, rows_ref, x1_ref, p_ref, g_ref, o_ref, buf_ref, idx_ref, idx_sem, row_sem):
    tm, D = x1_ref.shape
    i = pl.program_id(0)
    n = pl.num_programs(0)
    n_idx = tm * TOP_K
    idx_copy = functools.partial(_idx_copy, dest_ref, idx_ref, idx_sem, n_idx=n_idx)

    def gather(tile):
        slot = tile % 2
        base = slot * n_idx

        def issue(g, c):
            r0 = g * ISSUE_GROUP
            src = [idx_ref[base + r0 * TOP_K + k] for k in range(ISSUE_GROUP * TOP_K)]
            for k, d in enumerate(src):
                pltpu.make_async_copy(rows_ref.at[d], buf_ref.at[slot, k % TOP_K, r0 + k // TOP_K],
                                      row_sem.at[slot]).start(priority=k % 2)
            return c

        lax.fori_loop(0, tm // ISSUE_GROUP, issue, 0)

    @pl.when(i == 0)
    def _():
        idx_copy(i).start()
        idx_copy(i).wait()
        gather(i)

        @pl.when(n > 1)
        def _():
            idx_copy(i + 1).start()

    @pl.when(i + 1 < n)
    def _():
        idx_copy(i + 1).wait()
        gather(i + 1)

    @pl.when(i + 2 < n)
    def _():
        idx_copy(i + 2).start()

    slot = i % 2
    for s in range(TOP_K):
        pltpu.make_async_copy(rows_ref.at[pl.ds(0, tm)], buf_ref.at[slot, s], row_sem.at[slot]).wait()

    half = D // 2
    x1 = x1_ref[...]
    acc_lo, acc_hi = x1[:, :half], x1[:, half:]
    p = p_ref[...]
    for s in range(TOP_K):
        lo, hi = _load_packed(buf_ref.at[slot, s])
        acc_lo = acc_lo + p[:, s:s + 1] * lo
        acc_hi = acc_hi + p[:, s:s + 1] * hi
    ms = (jnp.sum(acc_lo * acc_lo, axis=-1, keepdims=True) + jnp.sum(acc_hi * acc_hi, axis=-1, keepdims=True)) / D
    scale = lax.rsqrt(ms + RMS_EPS)
    o_ref[:, :half] = acc_lo * scale * g_ref[:, :half]
    o_ref[:, half:] = acc_hi * scale * g_ref[:, half:]


def _combine(dest, rows, x1, p, gain, tm):
    T, D = x1.shape
    any_spec = pl.BlockSpec(memory_space=pl.ANY)
    return pl.pallas_call(
        _combine_kernel,
        grid=(T // tm,),
        in_specs=[any_spec, any_spec, pl.BlockSpec((tm, D), lambda i: (i, 0)),
                  pl.BlockSpec((tm, LANES), lambda i: (i, 0)), pl.BlockSpec(gain.shape, lambda i: (0, 0))],
        out_specs=pl.BlockSpec((tm, D), lambda i: (i, 0)),
        out_shape=jax.ShapeDtypeStruct((T, D), F32),
        scratch_shapes=[pltpu.VMEM((2, TOP_K, tm) + rows.shape[1:], U32), pltpu.SMEM((2 * tm * TOP_K,), jnp.int32),
                        pltpu.SemaphoreType.DMA((2,)), pltpu.SemaphoreType.DMA((2,))],
        compiler_params=_cparams("arbitrary"),
        name="moe_combine_norm",
    )(dest, rows, x1, p, gain)


def _pick_tile(n, pref):
    t = min(n, pref)
    assert n % t == 0, (n, t)
    return t


def kernel(x, norm_mix_g, w_in, gla_gate_up, gla_gate_bias, gla_norm_g, rwkv_mu, rwkv_w0, rwkv_w2, rwkv_a0, rwkv_a2, rwkv_g2, rwkv_k_k, rwkv_k_a, rwkv_r_k, rwkv_ln_w, rwkv_ln_b, w_branch_a, w_branch_b, w_out, norm_ffn_g, router_w, router_b, expert_w1, expert_b1, expert_w2, expert_b2, norm_final_g):
    assert norm_mix_g.shape[0] == 1, "single-layer block"
    B, S, D = x.shape
    T = B * S
    qk = gla_gate_bias.shape[-1]
    vw = GLA_HEADS * GLA_DV
    W = rwkv_w0.shape[-1]
    H = W // RWKV_HEAD_DIM
    E = router_w.shape[-1]
    assert qk == GLA_HEADS * GLA_DK and B * H % LANES == 0 and E <= LANES
    row = lambda a: a.reshape(1, -1)

    wi = w_in[0]
    gla_cols = 2 * qk + vw + GLA_GATE_RANK + vw
    rwkv_cols = rwkv_mu.shape[-1]
    o_q, o_k, o_v, o_lr, o_og = 0, qk, 2 * qk, 2 * qk + vw, 2 * qk + vw + GLA_GATE_RANK
    w_gla = jnp.concatenate(
        [wi[:, o_q:o_lr], wi[:, o_og:gla_cols], wi[:, o_lr:o_og],
         jnp.zeros((D, LANES - GLA_GATE_RANK), F32)], axis=1).astype(BF16)
    up = jnp.concatenate([gla_gate_up[0], jnp.zeros((LANES - GLA_GATE_RANK, qk), F32)], axis=0).astype(BF16)
    w_rwkv = wi[:, gla_cols:gla_cols + rwkv_cols].astype(BF16)
    w_gate = wi[:, gla_cols + rwkv_cols:].astype(BF16)
    n_w, n_a = rwkv_w2.shape[1], rwkv_a2.shape[1]
    assert n_w + n_a == LANES
    w2p = jnp.concatenate([rwkv_w2[0], jnp.zeros((n_a, W), F32)], axis=0).astype(BF16)
    a2p = jnp.concatenate([jnp.zeros((n_w, W), F32), rwkv_a2[0]], axis=0).astype(BF16)

    tm_gla = _pick_tile(S, 512)
    q, k, v, og, la = _gla_proj(x, row(norm_mix_g), w_gla, up, row(gla_gate_bias), tm_gla)
    ya = _gla(q, k, v, og, la, row(gla_norm_g), _pick_tile(S, 256))

    tm_r = _pick_tile(S, 512)
    hn = RWKV_HEAD_DIM // 2
    col = jnp.arange(W)
    scan_order = (col % (W // 2)) // hn * RWKV_HEAD_DIM + col // (W // 2) * hn + col % hn
    rv_, ld_, kia_, gate_ = _rwkv_proj(x, row(norm_mix_g), w_rwkv, row(rwkv_mu), row(rwkv_w0)[:, scan_order],
                                       w2p[:, scan_order], row(rwkv_a0), a2p,
                                       rwkv_g2[0][:, scan_order].astype(BF16), tm_r)
    G = B * H // LANES
    bg = B // G
    to_scan = lambda t, n: t.reshape(G, bg, S, H, n).transpose(2, 0, 4, 1, 3).reshape(S, G, n, LANES)
    par = lambda p: jnp.tile(p.reshape(H, RWKV_HEAD_DIM).T, (1, bg))
    ys = _rwkv_scan(to_scan(rv_, RWKV_HEAD_DIM), to_scan(ld_, hn), to_scan(kia_, RWKV_HEAD_DIM),
                    par(rwkv_k_k), par(rwkv_k_a), par(rwkv_r_k), par(rwkv_ln_w), par(rwkv_ln_b), _pick_tile(S, 32))
    ys = ys.reshape(S, G, hn, bg, H).transpose(1, 3, 0, 4, 2).reshape(T, W // 2)

    rw = jnp.concatenate([router_w[0], jnp.zeros((D, LANES - E), F32)], axis=1)
    rb = jnp.concatenate([router_b[0], jnp.zeros((LANES - E,), F32)]).reshape(1, LANES)
    rw_hi = rw.astype(BF16)
    rw_lo = (rw - rw_hi.astype(F32)).astype(BF16)
    consts = (row(norm_mix_g), w_gate, w_branch_a[0].astype(BF16), w_branch_b[0][scan_order, :].astype(BF16),
              w_out[0].astype(BF16), row(norm_ffn_g), rw_hi, rw_lo, rb)
    tm_m = _pick_tile(T, 512)
    x1, h2, e_sel, rank, p_sel, counts = _merge(
        x.reshape(T, D), ya.reshape(T, vw), ys, gate_.reshape(T, W), consts, E, tm_m)

    counts = counts[0, :E]
    padded = (counts + MOE_ROWS - 1) // MOE_ROWS * MOE_ROWS
    pad_ends = jnp.cumsum(padded)
    pad_starts = pad_ends - padded
    nb = -(-T * TOP_K // MOE_ROWS) + E
    dest = pad_starts[e_sel[:, :TOP_K]] + rank[:, :TOP_K]
    block_start = jnp.arange(nb, dtype=jnp.int32) * MOE_ROWS
    block_expert = jnp.minimum(jnp.sum(pad_ends[None, :] <= block_start[:, None], axis=1), E - 1).astype(jnp.int32)
    n_used = (pad_ends[-1:] // MOE_ROWS).astype(jnp.int32)
    dest = dest.reshape(-1).astype(jnp.int32)
    tm_d = _pick_tile(T, 256)
    xs = _dispatch(dest, h2, nb * MOE_ROWS, _pick_tile(T, 1024))

    FF = expert_w2.shape[2]
    w1g, w1l = _split_w1(expert_w1[0])
    b1 = expert_b1[0].reshape(E, 1, FF, 2)
    outs_e = _experts(block_expert, n_used, xs, w1g, w1l, b1[..., 0], b1[..., 1],
                      expert_w2[0].astype(BF16), expert_b2[0].reshape(E, 1, D))

    out = _combine(dest, outs_e, x1, p_sel, row(norm_final_g), tm_d)
    return out.reshape(B, S, D)
```
